```python
import jax, jax.numpy as jnp
from jax import lax
import numpy as np

D_MODEL = 2048
BATCH = 4
SEQ = 2048
DEPTH = 1
DEC_BATCH = 128
DEC_SEQ = 8
PAST_LEN = 16384
PAGE_SIZE = 128

D_MIX = D_MODEL
LRU_WIDTH = D_MIX // 2
LRU_HEADS = 8
LRU_HEAD_DIM = LRU_WIDTH // LRU_HEADS
LRU_C = 8.0
CONV_WIDTH = 4
SGU_WIDTH = D_MIX - LRU_WIDTH
SGU_HEADS = 8
SGU_HEAD_DIM = SGU_WIDTH // SGU_HEADS
CHUNK = 128
PROJ_WIDTH = 2 * LRU_WIDTH + 3 * SGU_WIDTH
EPS = 1e-6

kernel_name = "hymba_rglru_chunk_sgu_decode_step"


def rms_norm(x, g):
    xf = x.astype(jnp.float32)
    y = xf * lax.rsqrt(jnp.mean(xf * xf, axis=-1, keepdims=True) + EPS)
    return (y * g.astype(jnp.float32)).astype(x.dtype)


def causal_depthwise_conv(xb, buf, w, b):
    T = xb.shape[1]
    xp = jnp.concatenate([buf.astype(xb.dtype), xb], axis=1)
    y = b + xp[:, 0:T] * w[0]
    for k in range(1, CONV_WIDTH):
        y = y + xp[:, k:k + T] * w[k]
    return y, xp[:, -(CONV_WIDTH - 1):]


def rg_lru(x, reset, h0, w_r, b_r, w_i, b_i, lam):
    N, T, W = x.shape
    xh = x.reshape(N, T, LRU_HEADS, LRU_HEAD_DIM)
    r = jax.nn.sigmoid(jnp.einsum('nthi,hij->nthj', xh, w_r) + b_r).reshape(N, T, W).astype(jnp.float32)
    i = jax.nn.sigmoid(jnp.einsum('nthi,hij->nthj', xh, w_i) + b_i).reshape(N, T, W).astype(jnp.float32)
    log_a = -LRU_C * r * jax.nn.softplus(-lam.astype(jnp.float32))
    a = jnp.exp(log_a)
    mult = jnp.sqrt(-jnp.expm1(2.0 * log_a))
    rs = reset[None, :, None]
    mult = jnp.where(rs, 1.0, mult)
    a = jnp.where(rs, 0.0, a)
    bterm = mult * i * x.astype(jnp.float32)
    bterm = bterm.at[:, 0].add(a[:, 0] * h0.astype(jnp.float32))

    def combine(c1, c2):
        a1, b1 = c1
        a2, b2 = c2
        return a1 * a2, a2 * b1 + b2

    _, h = lax.associative_scan(combine, (a, bterm), axis=1)
    return h.astype(x.dtype), h[:, -1].astype(h0.dtype)


def chunk_spatial_gating(u, v, w_s, b_s):
    N, T, W = u.shape
    L = min(T, CHUNK)
    n_chunks = T // L
    mask = jnp.tril(jnp.ones((L, L), dtype=bool))
    ws = jnp.where(mask[None], w_s[:, :L, :L], 0.0).astype(v.dtype)
    vc = v.reshape(N, n_chunks, L, SGU_HEADS, SGU_HEAD_DIM)
    s = jnp.einsum('hts,ncshd->ncthd', ws, vc) + b_s[:, :L].T[None, None, :, :, None]
    return u * s.reshape(N, T, W)


def mixer_layer(x, pos0, conv_buf, h0, pre_g, post_g, w_in, conv_w, conv_b,
                w_r, b_r, w_i, b_i, lam, sgu_g, w_s, b_s, w_out):
    T = x.shape[1]
    z = rms_norm(x, pre_g)
    proj = z @ w_in
    o1 = LRU_WIDTH
    o2 = o1 + LRU_WIDTH
    o3 = o2 + SGU_WIDTH
    o4 = o3 + SGU_WIDTH
    xr, gr, u, v, gs = (proj[..., :o1], proj[..., o1:o2], proj[..., o2:o3],
                        proj[..., o3:o4], proj[..., o4:])
    xc, new_buf = causal_depthwise_conv(xr, conv_buf, conv_w, conv_b)
    reset = (pos0 + jnp.arange(T)) == 0
    hr, h_last = rg_lru(xc, reset, h0, w_r, b_r, w_i, b_i, lam)
    br = hr * jax.nn.silu(gr)
    u = jax.nn.gelu(u)
    v = rms_norm(jax.nn.gelu(v), sgu_g)
    bs = chunk_spatial_gating(u, v, w_s, b_s) * jax.nn.silu(gs)
    out = jnp.concatenate([br, bs], axis=-1) @ w_out
    y = x + rms_norm(out, post_g)
    return y, new_buf, h_last, v


def setup_inputs(seed: int = 0) -> dict:
    key = jax.random.key(seed)
    ks = jax.random.split(key, 20)
    f32 = jnp.float32
    a_c = jax.random.uniform(ks[9], (DEPTH, LRU_WIDTH), f32, 0.9, 0.999)
    a = a_c ** (1.0 / LRU_C)
    lam = jnp.log(a) - jnp.log1p(-a)
    return {
        "x_prompt": jax.random.normal(ks[0], (BATCH, SEQ, D_MODEL), f32),
        "x_sample": jax.random.normal(ks[1], (DEC_BATCH, DEC_SEQ, D_MODEL), f32),
        "state_rglru_conv": jax.random.normal(ks[2], (DEPTH, DEC_BATCH, CONV_WIDTH - 1, LRU_WIDTH), f32),
        "state_rglru_h": 0.5 * jax.random.normal(ks[3], (DEPTH, DEC_BATCH, LRU_WIDTH), f32),
        "pre_norm_g": 1.0 + 0.02 * jax.random.normal(ks[4], (DEPTH, D_MODEL), f32),
        "post_norm_g": 1.0 + 0.02 * jax.random.normal(ks[5], (DEPTH, D_MODEL), f32),
        "w_in": jax.random.normal(ks[6], (DEPTH, D_MODEL, PROJ_WIDTH), f32) * D_MODEL ** -0.5,
        "conv_w": jax.random.normal(ks[7], (DEPTH, CONV_WIDTH, LRU_WIDTH), f32) * CONV_WIDTH ** -0.5,
        "conv_b": 0.01 * jax.random.normal(ks[8], (DEPTH, LRU_WIDTH), f32),
        "w_rgate": jax.random.normal(ks[10], (DEPTH, LRU_HEADS, LRU_HEAD_DIM, LRU_HEAD_DIM), f32) * LRU_HEAD_DIM ** -0.5,
        "b_rgate": 0.01 * jax.random.normal(ks[11], (DEPTH, LRU_HEADS, LRU_HEAD_DIM), f32),
        "w_igate": jax.random.normal(ks[12], (DEPTH, LRU_HEADS, LRU_HEAD_DIM, LRU_HEAD_DIM), f32) * LRU_HEAD_DIM ** -0.5,
        "b_igate": 0.01 * jax.random.normal(ks[13], (DEPTH, LRU_HEADS, LRU_HEAD_DIM), f32),
        "lru_lambda": lam,
        "sgu_norm_g": 1.0 + 0.02 * jax.random.normal(ks[14], (DEPTH, SGU_WIDTH), f32),
        "w_spatial": jax.random.normal(ks[15], (DEPTH, SGU_HEADS, CHUNK, CHUNK), f32) * CHUNK ** -0.5,
        "b_spatial": 1.0 + 0.02 * jax.random.normal(ks[16], (DEPTH, SGU_HEADS, CHUNK), f32),
        "w_out": jax.random.normal(ks[17], (DEPTH, D_MIX, D_MODEL), f32) * D_MIX ** -0.5,
    }


def reference(x_prompt, x_sample, state_rglru_conv, state_rglru_h, pre_norm_g, post_norm_g,
              w_in, conv_w, conv_b, w_rgate, b_rgate, w_igate, b_igate, lru_lambda,
              sgu_norm_g, w_spatial, b_spatial, w_out):
    yp = x_prompt
    ys = x_sample
    conv_p, h_p, conv_s, h_s, v_s = [], [], [], [], []
    for l in range(DEPTH):
        params = (pre_norm_g[l], post_norm_g[l], w_in[l], conv_w[l], conv_b[l],
                  w_rgate[l], b_rgate[l], w_igate[l], b_igate[l], lru_lambda[l],
                  sgu_norm_g[l], w_spatial[l], b_spatial[l], w_out[l])
        buf0 = jnp.zeros((yp.shape[0], CONV_WIDTH - 1, LRU_WIDTH), yp.dtype)
        hz = jnp.zeros((yp.shape[0], LRU_WIDTH), state_rglru_h.dtype)
        yp, cb_p, hl_p, _ = mixer_layer(yp, 0, buf0, hz, *params)
        ys, cb_s, hl_s, vs = mixer_layer(ys, PAST_LEN, state_rglru_conv[l], state_rglru_h[l], *params)
        conv_p.append(cb_p)
        h_p.append(hl_p)
        conv_s.append(cb_s)
        h_s.append(hl_s)
        v_s.append(vs)
    return (yp, ys, jnp.stack(conv_p), jnp.stack(h_p), jnp.stack(conv_s), jnp.stack(h_s), jnp.stack(v_s))
```

```python
import functools
import math

import jax
import jax.numpy as jnp
from jax import lax
from jax.experimental import pallas as pl
from jax.experimental.pallas import tpu as pltpu

F32 = jnp.float32
BF16 = jnp.bfloat16

EPS = 1e-6
LRU_C = 8.0
CONV_WIDTH = 4
N_HEADS = 8
HEAD_DIM = 128
CHUNK = 128
LANES = 128
SUBLANES = 8

TM = 256
SEG = TM // SUBLANES
SEG_PITCH = SEG + SUBLANES
DEC_NB = 32
VMEM_LIMIT_BYTES = 56 * 1024 * 1024


def _sigmoid(x):
    return 0.5 * jnp.tanh(0.5 * x) + 0.5


def _silu(x):
    return x * _sigmoid(x)


def _gelu(x):
    c = math.sqrt(2.0 / math.pi)
    return x * (0.5 * (1.0 + jnp.tanh(c * (x + 0.044715 * (x * x * x)))))


def _rms_norm(x, g):
    return x * lax.rsqrt(jnp.mean(x * x, axis=-1, keepdims=True) + EPS) * g


def _lru_coeffs(xp_scr, a_scr, b_scr, conv_w, conv_b, wg, b_r, b_i, lam, rows_per_step, reset_rows):
    R = rows_per_step
    for h in range(N_HEADS):
        ls = slice(h * HEAD_DIM, (h + 1) * HEAD_DIM)
        xc = conv_b[:, ls] + xp_scr[0:TM, ls] * conv_w[0:1, ls]
        for k in range(1, CONV_WIDTH):
            xc = xc + xp_scr[k * R:k * R + TM, ls] * conv_w[k:k + 1, ls]
        g = jnp.dot(xc.astype(BF16), wg[h], preferred_element_type=F32)
        r = _sigmoid(g[:, :HEAD_DIM] + b_r[:, ls])
        i = _sigmoid(g[:, HEAD_DIM:] + b_i[:, ls])
        lam_h = lam[:, ls]
        softplus_neg = jnp.maximum(-lam_h, 0.0) + jnp.log1p(jnp.exp(-jnp.abs(lam_h)))
        log_a = r * (-LRU_C * softplus_neg)
        a = jnp.exp(log_a)
        mult = jnp.sqrt(-jnp.tanh(log_a) * (a * a + 1.0))
        ix = i * xc
        b = mult * ix
        a_scr[:, ls] = a
        b_scr[:, ls] = b
        if reset_rows is not None:
            a_scr[0:SUBLANES, ls] = jnp.where(reset_rows, 0.0, a[0:SUBLANES])
            b_scr[0:SUBLANES, ls] = jnp.where(reset_rows, ix[0:SUBLANES], b[0:SUBLANES])


def _in_proj(z_scr, w_in, col):
    return jnp.dot(z_scr[...], w_in[:, col * 1024:(col + 1) * 1024], preferred_element_type=F32)


def _out_proj(x, cat_scr, w_out, post_g):
    out = jnp.dot(cat_scr[...], w_out[...], preferred_element_type=F32)
    return x + _rms_norm(out, post_g[...])


def _prompt_kernel(x_ref, pre_g, post_g, w_in, conv_w, conv_b, wg, b_r, b_i, lam, sgu_g, w_s, bs_rows, w_out,
                   y_ref, conv_out, h_out,
                   z_scr, xr_scr, xp_scr, a_scr, b_scr, hp_scr, u_scr, g_scr, vb_scr, cat_scr, carry_h, carry_x):
    t = pl.program_id(1)

    @pl.when(t == 0)
    def _():
        carry_h[...] = jnp.zeros_like(carry_h)
        carry_x[...] = jnp.zeros_like(carry_x)

    x = x_ref[...]
    z_scr[...] = _rms_norm(x, pre_g[...]).astype(BF16)

    xr = _in_proj(z_scr, w_in, 0)
    conv_out[...] = xr[TM - (CONV_WIDTH - 1):TM, :]
    for j in range(SUBLANES):
        for c in range(N_HEADS):
            xr_scr[c, j * SEG_PITCH:j * SEG_PITCH + SEG, :] = xr[j * SEG:(j + 1) * SEG, c * LANES:(c + 1) * LANES]
    hist = (CONV_WIDTH - 1) * SUBLANES
    for k in range(SEG):
        for c in range(N_HEADS):
            xp_scr[hist + k * SUBLANES:hist + (k + 1) * SUBLANES, c * LANES:(c + 1) * LANES] = (
                xr_scr[c, pl.ds(k, SUBLANES, stride=SEG_PITCH), :])
    sub = lax.broadcasted_iota(jnp.int32, (SUBLANES, 1024), 0)
    for m in range(CONV_WIDTH - 1):
        src = hist + (SEG - (CONV_WIDTH - 1) + m) * SUBLANES
        rolled = pltpu.roll(xp_scr[src:src + SUBLANES, :], 1, 0)
        rs = slice(m * SUBLANES, (m + 1) * SUBLANES)
        xp_scr[rs, :] = jnp.where(sub == 0, carry_x[rs, :], rolled)
        carry_x[rs, :] = rolled

    reset_rows = (lax.broadcasted_iota(jnp.int32, (SUBLANES, HEAD_DIM), 0) == 0) & (t == 0)
    _lru_coeffs(xp_scr, a_scr, b_scr, conv_w, conv_b, wg, b_r, b_i, lam, SUBLANES, reset_rows)

    hl = jnp.zeros((SUBLANES, 1024), F32)
    pr = jnp.ones((SUBLANES, 1024), F32)
    for k in range(SEG):
        a_k = a_scr[k * SUBLANES:(k + 1) * SUBLANES, :]
        hl = a_k * hl + b_scr[k * SUBLANES:(k + 1) * SUBLANES, :]
        pr = a_k * pr
    c_in = carry_h[0:1, :]
    h0 = jnp.zeros((SUBLANES, 1024), F32)
    for j in range(SUBLANES):
        h0 = jnp.where(sub == j, c_in, h0)
        c_in = hl[j:j + 1, :] + pr[j:j + 1, :] * c_in
    carry_h[0:1, :] = c_in
    h_out[...] = c_in
    h = h0
    for k in range(SEG):
        h = a_scr[k * SUBLANES:(k + 1) * SUBLANES, :] * h + b_scr[k * SUBLANES:(k + 1) * SUBLANES, :]
        for c in range(N_HEADS):
            hp_scr[c, k * SUBLANES:(k + 1) * SUBLANES, :] = h[:, c * LANES:(c + 1) * LANES]
    for j in range(SUBLANES):
        for m in range(SEG // SUBLANES):
            for c in range(N_HEADS):
                a_scr[j * SEG + m * SUBLANES:j * SEG + (m + 1) * SUBLANES, c * LANES:(c + 1) * LANES] = (
                    hp_scr[c, pl.ds(m * SUBLANES * SUBLANES + j, SUBLANES, stride=SUBLANES), :])
    gr = _in_proj(z_scr, w_in, 1)
    cat_scr[:, 0:1024] = (a_scr[...] * _silu(gr)).astype(BF16)

    u_scr[...] = _in_proj(z_scr, w_in, 2)
    v = _rms_norm(_gelu(_in_proj(z_scr, w_in, 3)), sgu_g[...])
    vb_scr[...] = v.astype(BF16)
    g_scr[...] = _in_proj(z_scr, w_in, 4)
    causal = (lax.broadcasted_iota(jnp.int32, (CHUNK, CHUNK), 0) >= lax.broadcasted_iota(jnp.int32, (CHUNK, CHUNK), 1))
    for hd in range(N_HEADS):
        ls = slice(hd * HEAD_DIM, (hd + 1) * HEAD_DIM)
        ws_h = jnp.where(causal, w_s[hd], 0.0).astype(BF16)
        for ch in range(TM // CHUNK):
            rs = slice(ch * CHUNK, (ch + 1) * CHUNK)
            s = jnp.dot(ws_h, vb_scr[rs, ls], preferred_element_type=F32) + bs_rows[:, ls]
            cat_scr[rs, 1024 + hd * HEAD_DIM:1024 + (hd + 1) * HEAD_DIM] = (
                _gelu(u_scr[rs, ls]) * s * _silu(g_scr[rs, ls])).astype(BF16)

    y_ref[...] = _out_proj(x, cat_scr, w_out, post_g)


def _decode_kernel(n_steps,
                   x_ref, cs_ref, h0_ref, pre_g, post_g, w_in, conv_w, conv_b, wg, b_r, b_i, lam, sgu_g,
                   ws_rows, bs_rows, w_out,
                   y_ref, conv_out, h_out, v_out,
                   z_scr, xp_scr, a_scr, b_scr, u_scr, g_scr, v_scr, cat_scr):
    R = DEC_NB
    hist = (CONV_WIDTH - 1) * R
    for s in range(n_steps):
        z_scr[s * R:(s + 1) * R, :] = _rms_norm(x_ref[:, s * 2048:(s + 1) * 2048], pre_g[...]).astype(BF16)

    xp_scr[hist:hist + TM, :] = _in_proj(z_scr, w_in, 0)
    for m in range(CONV_WIDTH - 1):
        xp_scr[m * R:(m + 1) * R, :] = cs_ref[:, m * 1024:(m + 1) * 1024]
    for m in range(CONV_WIDTH - 1):
        conv_out[:, m * 1024:(m + 1) * 1024] = xp_scr[TM + m * R:TM + (m + 1) * R, :]
    _lru_coeffs(xp_scr, a_scr, b_scr, conv_w, conv_b, wg, b_r, b_i, lam, R, None)
    gr = _in_proj(z_scr, w_in, 1)
    for c in range(N_HEADS):
        ls = slice(c * LANES, (c + 1) * LANES)
        h = h0_ref[:, ls]
        for s in range(n_steps):
            rs = slice(s * R, (s + 1) * R)
            h = a_scr[rs, ls] * h + b_scr[rs, ls]
            cat_scr[rs, ls] = (h * _silu(gr[rs, ls])).astype(BF16)
        h_out[:, ls] = h

    u_scr[...] = _in_proj(z_scr, w_in, 2)
    v_scr[...] = _rms_norm(_gelu(_in_proj(z_scr, w_in, 3)), sgu_g[...])
    g_scr[...] = _in_proj(z_scr, w_in, 4)
    for s in range(n_steps):
        v_out[:, s * 1024:(s + 1) * 1024] = v_scr[s * R:(s + 1) * R, :]
    for tt in range(n_steps):
        rs = slice(tt * R, (tt + 1) * R)
        mix = bs_rows[tt:tt + 1, :] + ws_rows[tt * n_steps:tt * n_steps + 1, :] * v_scr[0:R, :]
        for s in range(1, tt + 1):
            mix = mix + ws_rows[tt * n_steps + s:tt * n_steps + s + 1, :] * v_scr[s * R:(s + 1) * R, :]
        cat_scr[rs, 1024:2048] = (_gelu(u_scr[rs, :]) * mix * _silu(g_scr[rs, :])).astype(BF16)

    out = _rms_norm(jnp.dot(cat_scr[...], w_out[...], preferred_element_type=F32), post_g[...])
    for s in range(n_steps):
        y_ref[:, s * 2048:(s + 1) * 2048] = x_ref[:, s * 2048:(s + 1) * 2048] + out[s * R:(s + 1) * R, :]


def _full(shape):
    return pl.BlockSpec(shape, lambda *_: (0,) * len(shape))


def _layer(x_prompt, x_sample, conv_state, h_state, pre_g, post_g, w_in, conv_w, conv_b, w_r, b_r, w_i, b_i, lam,
           sgu_g, w_s, b_s, w_out):
    n_seq, seq_len, d_model = x_prompt.shape
    dec_b, dec_t, _ = x_sample.shape
    lru_w = conv_w.shape[-1]
    assert d_model == 2048 and lru_w == 1024 and seq_len % TM == 0 and dec_t * DEC_NB == TM and dec_b % DEC_NB == 0
    assert dec_t <= CHUNK

    w_in_b = w_in.astype(BF16)
    w_out_b = w_out.astype(BF16)
    wg = jnp.concatenate([w_r, w_i], axis=-1).astype(BF16)
    row = lambda p: p.reshape(1, -1)
    common = (row(pre_g), row(post_g), w_in_b, conv_w, row(conv_b), wg, row(b_r), row(b_i), row(lam), row(sgu_g))
    common_specs = [_full((1, d_model)), _full((1, d_model)), _full(w_in_b.shape), _full(conv_w.shape),
                    _full((1, lru_w)), _full(wg.shape), _full((1, lru_w)), _full((1, lru_w)), _full((1, lru_w)),
                    _full((1, lru_w))]

    bs_prompt = jnp.repeat(b_s.T, HEAD_DIM, axis=1)
    tiles = seq_len // TM
    y_p, conv_p, h_p = pl.pallas_call(
        _prompt_kernel,
        grid=(n_seq, tiles),
        in_specs=[pl.BlockSpec((None, TM, d_model), lambda s, t: (s, t, 0))] + common_specs
                 + [_full(w_s.shape), _full(bs_prompt.shape), _full(w_out_b.shape)],
        out_specs=[pl.BlockSpec((None, TM, d_model), lambda s, t: (s, t, 0)),
                   pl.BlockSpec((None, CONV_WIDTH - 1, lru_w), lambda s, t: (s, 0, 0)),
                   pl.BlockSpec((None, 1, lru_w), lambda s, t: (s, 0, 0))],
        out_shape=[jax.ShapeDtypeStruct(x_prompt.shape, F32),
                   jax.ShapeDtypeStruct((n_seq, CONV_WIDTH - 1, lru_w), F32),
                   jax.ShapeDtypeStruct((n_seq, 1, lru_w), F32)],
        scratch_shapes=[pltpu.VMEM((TM, d_model), BF16),
                        pltpu.VMEM((N_HEADS, SUBLANES * SEG_PITCH, LANES), F32),
                        pltpu.VMEM((TM + (CONV_WIDTH - 1) * SUBLANES, lru_w), F32),
                        pltpu.VMEM((TM, lru_w), F32), pltpu.VMEM((TM, lru_w), F32),
                        pltpu.VMEM((N_HEADS, TM, LANES), F32),
                        pltpu.VMEM((TM, lru_w), F32), pltpu.VMEM((TM, lru_w), F32),
                        pltpu.VMEM((TM, lru_w), BF16),
                        pltpu.VMEM((TM, d_model), BF16),
                        pltpu.VMEM((SUBLANES, lru_w), F32),
                        pltpu.VMEM(((CONV_WIDTH - 1) * SUBLANES, lru_w), F32)],
        compiler_params=pltpu.CompilerParams(dimension_semantics=("arbitrary", "arbitrary"),
                                             vmem_limit_bytes=VMEM_LIMIT_BYTES),
        name="prompt_layer",
    )(x_prompt, *common, w_s, bs_prompt, w_out_b)

    ws_dec = jnp.repeat(jnp.transpose(w_s[:, :dec_t, :dec_t], (1, 2, 0)), HEAD_DIM, axis=-1).reshape(dec_t * dec_t, -1)
    bs_dec = jnp.repeat(b_s[:, :dec_t].T, HEAD_DIM, axis=1)
    kern = functools.partial(_decode_kernel, dec_t)
    y_s, conv_s, h_s, v_s = pl.pallas_call(
        kern,
        grid=(dec_b // DEC_NB,),
        in_specs=[pl.BlockSpec((DEC_NB, dec_t * d_model), lambda i: (i, 0)),
                  pl.BlockSpec((DEC_NB, (CONV_WIDTH - 1) * lru_w), lambda i: (i, 0)),
                  pl.BlockSpec((DEC_NB, lru_w), lambda i: (i, 0))] + common_specs
                 + [_full(ws_dec.shape), _full(bs_dec.shape), _full(w_out_b.shape)],
        out_specs=[pl.BlockSpec((DEC_NB, dec_t * d_model), lambda i: (i, 0)),
                   pl.BlockSpec((DEC_NB, (CONV_WIDTH - 1) * lru_w), lambda i: (i, 0)),
                   pl.BlockSpec((DEC_NB, lru_w), lambda i: (i, 0)),
                   pl.BlockSpec((DEC_NB, dec_t * lru_w), lambda i: (i, 0))],
        out_shape=[jax.ShapeDtypeStruct((dec_b, dec_t * d_model), F32),
                   jax.ShapeDtypeStruct((dec_b, (CONV_WIDTH - 1) * lru_w), F32),
                   jax.ShapeDtypeStruct((dec_b, lru_w), F32),
                   jax.ShapeDtypeStruct((dec_b, dec_t * lru_w), F32)],
        scratch_shapes=[pltpu.VMEM((TM, d_model), BF16),
                        pltpu.VMEM((TM + (CONV_WIDTH - 1) * DEC_NB, lru_w), F32),
                        pltpu.VMEM((TM, lru_w), F32), pltpu.VMEM((TM, lru_w), F32),
                        pltpu.VMEM((TM, lru_w), F32), pltpu.VMEM((TM, lru_w), F32),
                        pltpu.VMEM((TM, lru_w), F32),
                        pltpu.VMEM((TM, d_model), BF16)],
        compiler_params=pltpu.CompilerParams(dimension_semantics=("arbitrary",),
                                             vmem_limit_bytes=VMEM_LIMIT_BYTES),
        name="decode_layer",
    )(x_sample.reshape(dec_b, dec_t * d_model), conv_state.reshape(dec_b, -1), h_state, *common, ws_dec, bs_dec,
      w_out_b)
    return (y_p, y_s.reshape(x_sample.shape), conv_p, h_p.reshape(n_seq, lru_w),
            conv_s.reshape(dec_b, CONV_WIDTH - 1, lru_w), h_s, v_s.reshape(dec_b, dec_t, lru_w))


def kernel(x_prompt, x_sample, state_rglru_conv, state_rglru_h, pre_norm_g, post_norm_g, w_in, conv_w, conv_b,
           w_rgate, b_rgate, w_igate, b_igate, lru_lambda, sgu_norm_g, w_spatial, b_spatial, w_out):
    depth = w_in.shape[0]
    yp, ys = x_prompt, x_sample
    conv_p, h_p, conv_s, h_s, v_s = [], [], [], [], []
    for l in range(depth):
        yp, ys, cp, hp, cs, hs, vs = _layer(
            yp, ys, state_rglru_conv[l], state_rglru_h[l], pre_norm_g[l], post_norm_g[l], w_in[l], conv_w[l],
            conv_b[l], w_rgate[l], b_rgate[l], w_igate[l], b_igate[l], lru_lambda[l], sgu_norm_g[l], w_spatial[l],
            b_spatial[l], w_out[l])
        conv_p.append(cp)
        h_p.append(hp)
        conv_s.append(cs)
        h_s.append(hs)
        v_s.append(vs)
    return (yp, ys, jnp.stack(conv_p), jnp.stack(h_p), jnp.stack(conv_s), jnp.stack(h_s), jnp.stack(v_s))
```

```python
import collections
import functools
import math

import jax
import jax.numpy as jnp
from jax import lax
from jax.experimental import pallas as pl
from jax.experimental.pallas import tpu as pltpu

F32 = jnp.float32
BF16 = jnp.bfloat16

EPS = 1e-6
LRU_C = 8.0
CONV_WIDTH = 4
HIST = CONV_WIDTH - 1
N_HEADS = 8
HEAD_DIM = 128
CHUNK = 128
LANES = 128
SUBLANES = 8
N_SLABS = 8

D_MODEL = 2048
WIDTH = 1024
TM = 256
MM_CHUNK = 512
XR, GR, U, V, GS = range(5)
N_STAGE = 4
VMEM_LIMIT_BYTES = 62 * 1024 * 1024

Geometry = collections.namedtuple("Geometry", "n_seg seg_len seg_pitch step_pitch")
PROMPT = Geometry(n_seg=SUBLANES, seg_len=TM // SUBLANES, seg_pitch=TM // SUBLANES + SUBLANES, step_pitch=SUBLANES)
DEC_NB = 32


def _decode_geometry(n_steps):
    return Geometry(n_seg=TM // n_steps, seg_len=n_steps, seg_pitch=n_steps, step_pitch=TM // n_steps + SUBLANES)


def _sigmoid(x):
    return 0.5 * jnp.tanh(0.5 * x) + 0.5


def _silu(x):
    return x * _sigmoid(x)


def _gelu(x):
    c = math.sqrt(2.0 / math.pi)
    return x * (0.5 * (1.0 + jnp.tanh(c * (x + 0.044715 * (x * x * x)))))


def _rms_norm(x, g):
    return x * lax.rsqrt(jnp.mean(x * x, axis=-1, keepdims=True) + EPS) * g


def _to_step_order(xr, xr_scr, xp_scr, geo):
    R, L, P = geo.n_seg, geo.seg_len, geo.seg_pitch
    for s in range(R):
        for c in range(N_SLABS):
            xr_scr[c, s * P:s * P + L, :] = xr[s * L:(s + 1) * L, c * LANES:(c + 1) * LANES]
    for k in range(L):
        for g in range(R // SUBLANES):
            row = HIST * R + k * R + g * SUBLANES
            for c in range(N_SLABS):
                xp_scr[row:row + SUBLANES, c * LANES:(c + 1) * LANES] = (
                    xr_scr[c, pl.ds(g * SUBLANES * P + k, SUBLANES, stride=P), :])


def _to_natural_order(hp_scr, dst_scr, geo):
    R, L, P = geo.n_seg, geo.seg_len, geo.step_pitch
    for s in range(R):
        for m in range(L // SUBLANES):
            row = s * L + m * SUBLANES
            for c in range(N_SLABS):
                dst_scr[row:row + SUBLANES, c * LANES:(c + 1) * LANES] = (
                    hp_scr[c, pl.ds(m * SUBLANES * P + s, SUBLANES, stride=P), :])


def _lru_coeffs(xp_scr, a_scr, b_scr, conv_w, conv_b, wg, b_r, b_i, lam, rows_per_step, reset_rows, before_head=None):
    R = rows_per_step
    for h in range(N_HEADS):
        if before_head is not None:
            before_head(h)
        ls = slice(h * HEAD_DIM, (h + 1) * HEAD_DIM)
        xc = conv_b[:, ls] + xp_scr[0:TM, ls] * conv_w[0:1, ls]
        for k in range(1, CONV_WIDTH):
            xc = xc + xp_scr[k * R:k * R + TM, ls] * conv_w[k:k + 1, ls]
        g = jnp.dot(xc.astype(BF16), wg[h], preferred_element_type=F32)
        r = _sigmoid(g[:, :HEAD_DIM] + b_r[:, ls])
        i = _sigmoid(g[:, HEAD_DIM:] + b_i[:, ls])
        lam_h = lam[:, ls]
        softplus_neg = jnp.maximum(-lam_h, 0.0) + jnp.log1p(jnp.exp(-jnp.abs(lam_h)))
        log_a = r * (-LRU_C * softplus_neg)
        a = jnp.exp(log_a)
        mult = jnp.sqrt(-jnp.tanh(log_a) * (a * a + 1.0))
        ix = i * xc
        b = mult * ix
        a_scr[:, ls] = a
        b_scr[:, ls] = b
        if reset_rows is not None:
            a_scr[0:SUBLANES, ls] = jnp.where(reset_rows, 0.0, a[0:SUBLANES])
            b_scr[0:SUBLANES, ls] = jnp.where(reset_rows, ix[0:SUBLANES], b[0:SUBLANES])


def _sgu_branch(pre_scr, vb_scr, w_mix, bias_rows, cat, before_head=None):
    causal = (lax.broadcasted_iota(jnp.int32, (CHUNK, CHUNK), 0) >= lax.broadcasted_iota(jnp.int32, (CHUNK, CHUNK), 1))
    for hd in range(N_HEADS):
        if before_head is not None:
            before_head(hd)
        ls = slice(hd * HEAD_DIM, (hd + 1) * HEAD_DIM)
        w_h = jnp.where(causal, w_mix[hd], 0.0).astype(BF16)
        for ch in range(TM // CHUNK):
            rs = slice(ch * CHUNK, (ch + 1) * CHUNK)
            s = jnp.dot(w_h, vb_scr[rs, ls], preferred_element_type=F32) + bias_rows[:, ls]
            cat[rs, WIDTH + hd * HEAD_DIM:WIDTH + (hd + 1) * HEAD_DIM] = (
                _gelu(pre_scr[U - 1, rs, ls]) * s * _silu(pre_scr[GS - 1, rs, ls])).astype(BF16)


def _load_weight(w_hbm, w_scr, stage, sem):
    n_row = w_hbm.shape[0] // TM
    n_chunks = w_scr.shape[0] * n_row

    def aligned(v, m):
        return v if isinstance(v, int) else pl.multiple_of(v, m)

    def copy(k, slot):
        rows = pl.ds(aligned((k % n_row) * TM, TM), TM)
        cols = pl.ds(aligned((k // n_row) * WIDTH, WIDTH), WIDTH)
        return pltpu.make_async_copy(w_hbm.at[rows, cols], stage.at[slot], sem.at[slot])

    for k in range(N_STAGE):
        copy(k, k).start()

    def body(k, carry):
        slot = k % N_STAGE
        copy(k, slot).wait()
        w_scr[k // n_row, pl.ds(aligned((k % n_row) * TM, TM), TM), :] = stage[slot].astype(BF16)

        @pl.when(k + N_STAGE < n_chunks)
        def _():
            copy(k + N_STAGE, slot).start()
        return carry

    lax.fori_loop(0, n_chunks, body, 0)


def _layer_kernel(tiles_per_seq, n_tiles, dec_t,
                  x_ref, xprev_ref, xs_hbm, cs_hbm, hs_hbm, w_in_hbm, w_out_hbm,
                  pre_g, post_g, conv_w, conv_b, wg, b_r, b_i, lam, sgu_g, w_s, bs_rows, ws_dec, bs_dec,
                  y_ref, conv_out, h_out, ys_hbm, xrs_hbm, hs_out_hbm, vs_hbm,
                  w_in, w_out, z_scr, xr_scr, xp_scr, a_scr, b_scr, pre_scr, out_scr, vb_scr, cat_scr,
                  carry_h, carry_x, xs_buf, h0_buf, hn_buf, w_sem, in_sem, out_sem):
    hp_scr = xr_scr
    vs_buf = b_scr
    i = pl.program_id(0)
    t = i % tiles_per_seq
    cat_prev = cat_scr.at[(i + 1) % 2]
    cat_cur = cat_scr.at[i % 2]

    def out_piece(c):
        half = slice((c % 2) * MM_CHUNK, (c % 2 + 1) * MM_CHUNK)
        out_scr[:, c * MM_CHUNK:(c + 1) * MM_CHUNK] = jnp.dot(cat_prev[...], w_out[c // 2, :, half],
                                                              preferred_element_type=F32)

    def in_piece(slab, half):
        cs = slice(half * MM_CHUNK, (half + 1) * MM_CHUNK)
        pre_scr[slab - 1, :, cs] = jnp.dot(z_scr[...], w_in[slab, :, cs], preferred_element_type=F32)

    def finish_prev():
        y_ref[...] = xprev_ref[...] + _rms_norm(out_scr[...], post_g[...])

    @pl.when(i == 0)
    def _():
        _load_weight(w_in_hbm, w_in, pre_scr, w_sem)
        _load_weight(w_out_hbm, w_out, pre_scr, w_sem)
        cat_scr[1] = jnp.zeros(cat_scr.shape[1:], BF16)

    @pl.when(t == 0)
    def _():
        carry_h[...] = jnp.zeros_like(carry_h)
        carry_x[...] = jnp.zeros_like(carry_x)

    @pl.when(i == n_tiles)
    def _():
        for c in range(D_MODEL // MM_CHUNK):
            out_piece(c)
        finish_prev()

    @pl.when(i > n_tiles)
    def _():
        _decode_tile(i - n_tiles - 1, _decode_geometry(dec_t), xs_hbm, cs_hbm, hs_hbm, pre_g, post_g, conv_w, conv_b,
                     wg, b_r, b_i, lam, sgu_g, ws_dec, bs_dec, ys_hbm, xrs_hbm, hs_out_hbm, vs_hbm, w_in, w_out, z_scr,
                     xr_scr, xp_scr, a_scr, b_scr, hp_scr, pre_scr, out_scr, vb_scr, cat_scr.at[0], xs_buf, vs_buf,
                     h0_buf, hn_buf, in_sem, out_sem)

    @pl.when(i < n_tiles)
    def _():
        geo = PROMPT
        R = geo.n_seg
        out_piece(0)
        z_scr[...] = _rms_norm(x_ref[...], pre_g[...]).astype(BF16)

        xr = jnp.dot(z_scr[...], w_in[XR], preferred_element_type=F32)
        conv_out[...] = xr[TM - HIST:TM, :]
        out_piece(1)
        _to_step_order(xr, xr_scr, xp_scr, geo)
        sub = lax.broadcasted_iota(jnp.int32, (R, WIDTH), 0)
        for m in range(HIST):
            src = HIST * R + (geo.seg_len - HIST + m) * R
            rolled = pltpu.roll(xp_scr[src:src + R, :], 1, 0)
            rs = slice(m * R, (m + 1) * R)
            xp_scr[rs, :] = jnp.where(sub == 0, carry_x[rs, :], rolled)
            carry_x[rs, :] = rolled

        pieces = {0: (GR, 0), 2: (GR, 1), 4: (V, 0), 6: (V, 1)}
        reset_rows = (lax.broadcasted_iota(jnp.int32, (SUBLANES, HEAD_DIM), 0) == 0) & (t == 0)
        _lru_coeffs(xp_scr, a_scr, b_scr, conv_w, conv_b, wg, b_r, b_i, lam, R, reset_rows,
                    before_head=lambda h: in_piece(*pieces[h]) if h in pieces else None)

        in_piece(U, 0)
        hl = jnp.zeros((R, WIDTH), F32)
        pr = jnp.ones((R, WIDTH), F32)
        for k in range(geo.seg_len):
            a_k = a_scr[k * R:(k + 1) * R, :]
            hl = a_k * hl + b_scr[k * R:(k + 1) * R, :]
            pr = a_k * pr
        c_in = carry_h[0:1, :]
        h0 = jnp.zeros((R, WIDTH), F32)
        for j in range(R):
            h0 = jnp.where(sub == j, c_in, h0)
            c_in = hl[j:j + 1, :] + pr[j:j + 1, :] * c_in
        carry_h[0:1, :] = c_in
        h_out[...] = c_in
        h = h0
        for k in range(geo.seg_len):
            h = a_scr[k * R:(k + 1) * R, :] * h + b_scr[k * R:(k + 1) * R, :]
            for c in range(N_SLABS):
                hp_scr[c, k * geo.step_pitch:k * geo.step_pitch + R, :] = h[:, c * LANES:(c + 1) * LANES]
        in_piece(U, 1)
        _to_natural_order(hp_scr, a_scr, geo)
        cat_cur[:, 0:WIDTH] = (a_scr[...] * _silu(pre_scr[GR - 1])).astype(BF16)

        in_piece(GS, 0)
        vb_scr[...] = _rms_norm(_gelu(pre_scr[V - 1]), sgu_g[...]).astype(BF16)
        between = {1: lambda: in_piece(GS, 1), 2: lambda: out_piece(2), 5: lambda: out_piece(3)}
        _sgu_branch(pre_scr, vb_scr, w_s, bs_rows, cat_cur,
                    before_head=lambda hd: between[hd]() if hd in between else None)
        finish_prev()


def _decode_tile(d, geo, xs_hbm, cs_hbm, hs_hbm, pre_g, post_g, conv_w, conv_b, wg, b_r, b_i, lam, sgu_g,
                 w_mix, bias_rows, ys_hbm, xrs_hbm, hs_out_hbm, vs_hbm, w_in, w_out, z_scr, xr_scr, xp_scr, a_scr, b_scr,
                 hp_scr, pre_scr, out_scr, vb_scr, cat, xs_buf, vs_buf, h0_buf, hn_buf, in_sem, out_sem):
    R, L = geo.n_seg, geo.seg_len
    rows = pl.ds(pl.multiple_of(d * TM, TM), TM)
    seqs = pl.ds(pl.multiple_of(d * R, R), R)

    copies_in = [pltpu.make_async_copy(xs_hbm.at[rows, :], xs_buf, in_sem.at[0]),
                 pltpu.make_async_copy(hs_hbm.at[seqs, :], h0_buf, in_sem.at[1])]
    copies_in += [pltpu.make_async_copy(cs_hbm.at[m, seqs, :], xp_scr.at[pl.ds(m * R, R), :], in_sem.at[2 + m])
                  for m in range(HIST)]
    for cp in copies_in:
        cp.start()
    for cp in copies_in:
        cp.wait()

    z_scr[...] = _rms_norm(xs_buf[...], pre_g[...]).astype(BF16)

    xr = jnp.dot(z_scr[...], w_in[XR], preferred_element_type=F32)
    xr_stage = out_scr.at[:, pl.ds(0, WIDTH)]
    xr_stage[...] = xr
    xr_copy = pltpu.make_async_copy(xr_stage, xrs_hbm.at[rows, :], out_sem.at[0])
    xr_copy.start()
    _to_step_order(xr, xr_scr, xp_scr, geo)
    _lru_coeffs(xp_scr, a_scr, b_scr, conv_w, conv_b, wg, b_r, b_i, lam, R, None)
    for slab in (GR, U, V, GS):
        pre_scr[slab - 1] = jnp.dot(z_scr[...], w_in[slab], preferred_element_type=F32)
    for c in range(N_SLABS):
        ls = slice(c * LANES, (c + 1) * LANES)
        h = h0_buf[:, ls]
        for s in range(L):
            h = a_scr[s * R:(s + 1) * R, ls] * h + b_scr[s * R:(s + 1) * R, ls]
            hp_scr[c, s * geo.step_pitch:s * geo.step_pitch + R, :] = h
        hn_buf[:, ls] = h
    _to_natural_order(hp_scr, a_scr, geo)
    cat[:, 0:WIDTH] = (a_scr[...] * _silu(pre_scr[GR - 1])).astype(BF16)

    vs_buf[...] = _rms_norm(_gelu(pre_scr[V - 1]), sgu_g[...])
    vb_scr[...] = vs_buf[...].astype(BF16)
    _sgu_branch(pre_scr, vb_scr, w_mix, bias_rows, cat)

    xr_copy.wait()
    for c in range(D_MODEL // WIDTH):
        out_scr[:, c * WIDTH:(c + 1) * WIDTH] = jnp.dot(cat[...], w_out[c], preferred_element_type=F32)
    xs_buf[...] = xs_buf[...] + _rms_norm(out_scr[...], post_g[...])

    copies_out = [pltpu.make_async_copy(xs_buf, ys_hbm.at[rows, :], out_sem.at[1]),
                  pltpu.make_async_copy(vs_buf, vs_hbm.at[rows, :], out_sem.at[2]),
                  pltpu.make_async_copy(hn_buf, hs_out_hbm.at[seqs, :], out_sem.at[3])]
    for cp in copies_out:
        cp.start()
    for cp in copies_out:
        cp.wait()


def _full(shape):
    return pl.BlockSpec(shape, lambda *_: (0,) * len(shape))


def _layer(x_prompt, x_sample, conv_state, h_state, pre_g, post_g, w_in, conv_w, conv_b, w_r, b_r, w_i, b_i, lam,
           sgu_g, w_s, b_s, w_out):
    n_seq, seq_len, d_model = x_prompt.shape
    dec_b, dec_t, _ = x_sample.shape
    assert d_model == D_MODEL and conv_w.shape == (CONV_WIDTH, WIDTH) and w_in.shape == (D_MODEL, 5 * WIDTH)
    assert seq_len % TM == 0 and dec_t * DEC_NB == TM and dec_b % DEC_NB == 0
    assert dec_t % SUBLANES == 0 and CHUNK % dec_t == 0 and HIST <= dec_t

    wg = jnp.concatenate([w_r, w_i], axis=-1).astype(BF16)
    row = lambda p: p.reshape(1, -1)
    bs_prompt = jnp.repeat(b_s.T, HEAD_DIM, axis=1)
    reps = CHUNK // dec_t
    ws_dec = jax.vmap(lambda w: jnp.kron(jnp.eye(reps, dtype=w.dtype), w))(w_s[:, :dec_t, :dec_t])
    bs_dec = jnp.tile(bs_prompt[:dec_t], (reps, 1))
    params = (row(pre_g), row(post_g), conv_w, row(conv_b), wg, row(b_r), row(b_i), row(lam), row(sgu_g), w_s,
              bs_prompt, ws_dec, bs_dec)

    tiles = seq_len // TM
    n_tiles = n_seq * tiles
    n_dec = dec_b // DEC_NB
    x_rows = x_prompt.reshape(n_seq * seq_len, d_model)
    xs_rows = x_sample.reshape(dec_b * dec_t, d_model)
    cs_steps = jnp.transpose(conv_state, (1, 0, 2))
    tile_of = lambda i: jnp.minimum(i, n_tiles - 1)
    prev_of = lambda i: jnp.clip(i - 1, 0, n_tiles - 1)
    any_spec = pl.BlockSpec(memory_space=pl.ANY)
    dec_geo = _decode_geometry(dec_t)
    relayout_rows = max(g.n_seg * g.seg_pitch for g in (PROMPT, dec_geo))
    assert relayout_rows >= max(g.seg_len * g.step_pitch for g in (PROMPT, dec_geo))
    y_rows, conv_p, h_p, ys_rows, xrs_rows, h_s, vs_rows = pl.pallas_call(
        functools.partial(_layer_kernel, tiles, n_tiles, dec_t),
        grid=(n_tiles + 1 + n_dec,),
        in_specs=[pl.BlockSpec((TM, d_model), lambda i: (tile_of(i), 0)),
                  pl.BlockSpec((TM, d_model), lambda i: (prev_of(i), 0)),
                  any_spec, any_spec, any_spec, any_spec, any_spec] + [_full(p.shape) for p in params],
        out_specs=[pl.BlockSpec((TM, d_model), lambda i: (prev_of(i), 0)),
                   pl.BlockSpec((None, HIST, WIDTH), lambda i: (tile_of(i) // tiles, 0, 0)),
                   pl.BlockSpec((None, 1, WIDTH), lambda i: (tile_of(i) // tiles, 0, 0)),
                   any_spec, any_spec, any_spec, any_spec],
        out_shape=[jax.ShapeDtypeStruct(x_rows.shape, F32),
                   jax.ShapeDtypeStruct((n_seq, HIST, WIDTH), F32),
                   jax.ShapeDtypeStruct((n_seq, 1, WIDTH), F32),
                   jax.ShapeDtypeStruct(xs_rows.shape, F32),
                   jax.ShapeDtypeStruct((dec_b * dec_t, WIDTH), F32),
                   jax.ShapeDtypeStruct(h_state.shape, F32),
                   jax.ShapeDtypeStruct((dec_b * dec_t, WIDTH), F32)],
        scratch_shapes=[pltpu.VMEM((5, D_MODEL, WIDTH), BF16),
                        pltpu.VMEM((D_MODEL // WIDTH, D_MODEL, WIDTH), BF16),
                        pltpu.VMEM((TM, D_MODEL), BF16),
                        pltpu.VMEM((N_SLABS, relayout_rows, LANES), F32),
                        pltpu.VMEM((TM + HIST * DEC_NB, WIDTH), F32),
                        pltpu.VMEM((TM, WIDTH), F32), pltpu.VMEM((TM, WIDTH), F32),
                        pltpu.VMEM((N_STAGE, TM, WIDTH), F32),
                        pltpu.VMEM((TM, D_MODEL), F32),
                        pltpu.VMEM((TM, WIDTH), BF16),
                        pltpu.VMEM((2, TM, D_MODEL), BF16),
                        pltpu.VMEM((SUBLANES, WIDTH), F32),
                        pltpu.VMEM((HIST * SUBLANES, WIDTH), F32),
                        pltpu.VMEM((TM, D_MODEL), F32),
                        pltpu.VMEM((DEC_NB, WIDTH), F32), pltpu.VMEM((DEC_NB, WIDTH), F32),
                        pltpu.SemaphoreType.DMA((N_STAGE,)),
                        pltpu.SemaphoreType.DMA((2 + HIST,)),
                        pltpu.SemaphoreType.DMA((4,))],
        compiler_params=pltpu.CompilerParams(dimension_semantics=("arbitrary",),
                                             vmem_limit_bytes=VMEM_LIMIT_BYTES),
        name="hybrid_layer",
    )(x_rows, x_rows, xs_rows, cs_steps, h_state, w_in, w_out, *params)
    conv_s = xrs_rows.reshape(dec_b, dec_t, WIDTH)[:, dec_t - HIST:, :]
    return (y_rows.reshape(x_prompt.shape), ys_rows.reshape(x_sample.shape), conv_p, h_p.reshape(n_seq, WIDTH),
            conv_s, h_s, vs_rows.reshape(dec_b, dec_t, WIDTH))


def kernel(x_prompt, x_sample, state_rglru_conv, state_rglru_h, pre_norm_g, post_norm_g, w_in, conv_w, conv_b,
           w_rgate, b_rgate, w_igate, b_igate, lru_lambda, sgu_norm_g, w_spatial, b_spatial, w_out):
    depth = w_in.shape[0]
    yp, ys = x_prompt, x_sample
    conv_p, h_p, conv_s, h_s, v_s = [], [], [], [], []
    for l in range(depth):
        yp, ys, cp, hp, cs, hs, vs = _layer(
            yp, ys, state_rglru_conv[l], state_rglru_h[l], pre_norm_g[l], post_norm_g[l], w_in[l], conv_w[l],
            conv_b[l], w_rgate[l], b_rgate[l], w_igate[l], b_igate[l], lru_lambda[l], sgu_norm_g[l], w_spatial[l],
            b_spatial[l], w_out[l])
        conv_p.append(cp)
        h_p.append(hp)
        conv_s.append(cs)
        h_s.append(hs)
        v_s.append(vs)
    return (yp, ys, jnp.stack(conv_p), jnp.stack(h_p), jnp.stack(conv_s), jnp.stack(h_s), jnp.stack(v_s))
```

```python
import collections
import functools
import math

import jax
import jax.numpy as jnp
from jax import lax
from jax.experimental import pallas as pl
from jax.experimental.pallas import tpu as pltpu

F32 = jnp.float32
BF16 = jnp.bfloat16

EPS = 1e-6
LRU_C = 8.0
CONV_WIDTH = 4
HIST = CONV_WIDTH - 1
N_HEADS = 8
HEAD_DIM = 128
CHUNK = 128
LANES = 128
SUBLANES = 8
N_SLABS = 8
MXU_DIM = 256

D_MODEL = 2048
WIDTH = 1024
TM = 256
MM_CHUNK = 512
XR, GR, U, V, GS = range(5)
N_STAGE = 4
VMEM_LIMIT_BYTES = 62 * 1024 * 1024

Geometry = collections.namedtuple("Geometry", "n_seg seg_len seg_pitch step_pitch")
PROMPT = Geometry(n_seg=SUBLANES, seg_len=TM // SUBLANES, seg_pitch=TM // SUBLANES + SUBLANES, step_pitch=SUBLANES)
DEC_NB = 32


def _decode_geometry(n_steps):
    return Geometry(n_seg=TM // n_steps, seg_len=n_steps, seg_pitch=n_steps, step_pitch=TM // n_steps + SUBLANES)


def _sigmoid(x):
    return 0.5 * jnp.tanh(0.5 * x) + 0.5


def _silu(x):
    return x * _sigmoid(x)


def _gelu(x):
    c = math.sqrt(2.0 / math.pi)
    return x * (0.5 * (1.0 + jnp.tanh(c * (x + 0.044715 * (x * x * x)))))


def _rms_norm(x, g):
    return x * lax.rsqrt(jnp.mean(x * x, axis=-1, keepdims=True) + EPS) * g


def _pre_norm(x_ref, g_ref, z_scr):
    x = x_ref[...]
    inv = lax.rsqrt(jnp.mean(x * x, axis=-1, keepdims=True) + EPS)
    for kb in range(D_MODEL // MXU_DIM):
        cs = slice(kb * MXU_DIM, (kb + 1) * MXU_DIM)
        z_scr[:, cs] = (x[:, cs] * inv * g_ref[:, cs]).astype(BF16)


def _to_step_order(xr, xr_scr, xp_scr, geo):
    R, L, P = geo.n_seg, geo.seg_len, geo.seg_pitch
    for s in range(R):
        for c in range(N_SLABS):
            xr_scr[c, s * P:s * P + L, :] = xr[s * L:(s + 1) * L, c * LANES:(c + 1) * LANES]
    for k in range(L):
        for g in range(R // SUBLANES):
            row = HIST * R + k * R + g * SUBLANES
            for c in range(N_SLABS):
                xp_scr[row:row + SUBLANES, c * LANES:(c + 1) * LANES] = (
                    xr_scr[c, pl.ds(g * SUBLANES * P + k, SUBLANES, stride=P), :])


def _to_natural_order(hp_scr, dst_scr, geo):
    R, L, P = geo.n_seg, geo.seg_len, geo.step_pitch
    for s in range(R):
        for m in range(L // SUBLANES):
            row = s * L + m * SUBLANES
            for c in range(N_SLABS):
                dst_scr[row:row + SUBLANES, c * LANES:(c + 1) * LANES] = (
                    hp_scr[c, pl.ds(m * SUBLANES * P + s, SUBLANES, stride=P), :])


def _lru_coeffs(xp_scr, a_scr, b_scr, gate_scr, conv_w, conv_b, wg, b_r, b_i, lam, rows_per_step, reset_rows,
                before_head=None):
    R = rows_per_step
    for h in range(N_HEADS):
        ls = slice(h * HEAD_DIM, (h + 1) * HEAD_DIM)
        xc = conv_b[:, ls] + xp_scr[0:TM, ls] * conv_w[0:1, ls]
        for k in range(1, CONV_WIDTH):
            xc = xc + xp_scr[k * R:k * R + TM, ls] * conv_w[k:k + 1, ls]
        b_scr[:, ls] = xc
    for h in range(N_HEADS):
        ls = slice(h * HEAD_DIM, (h + 1) * HEAD_DIM)
        gate_scr[h // 4][:, (h % 4) * 2 * HEAD_DIM:(h % 4 + 1) * 2 * HEAD_DIM] = jnp.dot(
            b_scr[:, ls].astype(BF16), wg[h], preferred_element_type=F32)
    for h in range(N_HEADS):
        if before_head is not None:
            before_head(h)
        ls = slice(h * HEAD_DIM, (h + 1) * HEAD_DIM)
        xc = b_scr[:, ls]
        g = gate_scr[h // 4][:, (h % 4) * 2 * HEAD_DIM:(h % 4 + 1) * 2 * HEAD_DIM]
        r = _sigmoid(g[:, :HEAD_DIM] + b_r[:, ls])
        i = _sigmoid(g[:, HEAD_DIM:] + b_i[:, ls])
        lam_h = lam[:, ls]
        softplus_neg = jnp.maximum(-lam_h, 0.0) + jnp.log1p(jnp.exp(-jnp.abs(lam_h)))
        log_a = r * (-LRU_C * softplus_neg)
        a = jnp.exp(log_a)
        mult = jnp.sqrt(-jnp.tanh(log_a) * (a * a + 1.0))
        ix = i * xc
        b = mult * ix
        a_scr[:, ls] = a
        b_scr[:, ls] = b
        if reset_rows is not None:
            a_scr[0:SUBLANES, ls] = jnp.where(reset_rows, 0.0, a[0:SUBLANES])
            b_scr[0:SUBLANES, ls] = jnp.where(reset_rows, ix[0:SUBLANES], b[0:SUBLANES])


def _sgu_branch(pre_scr, vb_scr, w_mix, bias_rows, cat, before_head=None):
    causal = (lax.broadcasted_iota(jnp.int32, (CHUNK, CHUNK), 0) >= lax.broadcasted_iota(jnp.int32, (CHUNK, CHUNK), 1))
    for hd in range(N_HEADS):
        if before_head is not None:
            before_head(hd)
        ls = slice(hd * HEAD_DIM, (hd + 1) * HEAD_DIM)
        w_h = jnp.where(causal, w_mix[hd], 0.0).astype(BF16)
        for ch in range(TM // CHUNK):
            rs = slice(ch * CHUNK, (ch + 1) * CHUNK)
            s = jnp.dot(w_h, vb_scr[rs, ls], preferred_element_type=F32) + bias_rows[:, ls]
            cat[rs, WIDTH + hd * HEAD_DIM:WIDTH + (hd + 1) * HEAD_DIM] = (
                _gelu(pre_scr[U - 1, rs, ls]) * s * _silu(pre_scr[GS - 1, rs, ls])).astype(BF16)


def _load_weight(w_hbm, w_scr, stage, sem):
    n_row = w_hbm.shape[0] // TM
    n_chunks = w_scr.shape[0] * n_row

    def aligned(v, m):
        return v if isinstance(v, int) else pl.multiple_of(v, m)

    def copy(k, slot):
        rows = pl.ds(aligned((k % n_row) * TM, TM), TM)
        cols = pl.ds(aligned((k // n_row) * WIDTH, WIDTH), WIDTH)
        return pltpu.make_async_copy(w_hbm.at[rows, cols], stage.at[slot], sem.at[slot])

    for k in range(N_STAGE):
        copy(k, k).start()

    def body(k, carry):
        slot = k % N_STAGE
        copy(k, slot).wait()
        w_scr[k // n_row, pl.ds(aligned((k % n_row) * TM, TM), TM), :] = stage[slot].astype(BF16)

        @pl.when(k + N_STAGE < n_chunks)
        def _():
            copy(k + N_STAGE, slot).start()
        return carry

    lax.fori_loop(0, n_chunks, body, 0)


def _layer_kernel(tiles_per_seq, n_tiles, dec_t,
                  x_ref, xprev_ref, xs_hbm, cs_hbm, hs_hbm, w_in_hbm, w_out_hbm,
                  pre_g, post_g, conv_w, conv_b, wg, b_r, b_i, lam, sgu_g, w_s, bs_rows, ws_dec, bs_dec,
                  y_ref, conv_out, h_out, ys_hbm, xrs_hbm, hs_out_hbm, vs_hbm,
                  w_in, w_out, z_scr, xr_scr, xp_scr, a_scr, b_scr, pre_scr, out_scr, vb_scr, cat_scr,
                  carry_h, carry_x, xs_buf, h0_buf, hn_buf, w_sem, in_sem, out_sem):
    hp_scr = xr_scr
    vs_buf = b_scr
    i = pl.program_id(0)
    t = i % tiles_per_seq
    cat_prev = cat_scr.at[(i + 1) % 2]
    cat_cur = cat_scr.at[i % 2]

    def out_piece(c):
        half = slice((c % 2) * MM_CHUNK, (c % 2 + 1) * MM_CHUNK)
        out_scr[:, c * MM_CHUNK:(c + 1) * MM_CHUNK] = jnp.dot(cat_prev[...], w_out[c // 2, :, half],
                                                              preferred_element_type=F32)

    def in_piece(slab, half):
        cs = slice(half * MM_CHUNK, (half + 1) * MM_CHUNK)
        pre_scr[slab - 1, :, cs] = jnp.dot(z_scr[...], w_in[slab, :, cs], preferred_element_type=F32)

    def finish_prev():
        y_ref[...] = xprev_ref[...] + _rms_norm(out_scr[...], post_g[...])

    @pl.when(i == 0)
    def _():
        _load_weight(w_in_hbm, w_in, pre_scr, w_sem)
        _load_weight(w_out_hbm, w_out, pre_scr, w_sem)
        cat_scr[1] = jnp.zeros(cat_scr.shape[1:], BF16)

    @pl.when(t == 0)
    def _():
        carry_h[...] = jnp.zeros_like(carry_h)
        carry_x[...] = jnp.zeros_like(carry_x)

    @pl.when(i == n_tiles)
    def _():
        for c in range(D_MODEL // MM_CHUNK):
            out_piece(c)
        finish_prev()

    @pl.when(i > n_tiles)
    def _():
        _decode_tile(i - n_tiles - 1, _decode_geometry(dec_t), xs_hbm, cs_hbm, hs_hbm, pre_g, post_g, conv_w, conv_b,
                     wg, b_r, b_i, lam, sgu_g, ws_dec, bs_dec, ys_hbm, xrs_hbm, hs_out_hbm, vs_hbm, w_in, w_out, z_scr,
                     xr_scr, xp_scr, a_scr, b_scr, hp_scr, pre_scr, out_scr, vb_scr, cat_scr.at[0], xs_buf, vs_buf,
                     h0_buf, hn_buf, in_sem, out_sem)

    @pl.when(i < n_tiles)
    def _():
        geo = PROMPT
        R = geo.n_seg
        out_piece(0)
        _pre_norm(x_ref, pre_g, z_scr)

        xr = jnp.dot(z_scr[...], w_in[XR], preferred_element_type=F32)
        conv_out[...] = xr[TM - HIST:TM, :]
        out_piece(1)
        _to_step_order(xr, xr_scr, xp_scr, geo)
        sub = lax.broadcasted_iota(jnp.int32, (R, WIDTH), 0)
        for m in range(HIST):
            src = HIST * R + (geo.seg_len - HIST + m) * R
            rolled = pltpu.roll(xp_scr[src:src + R, :], 1, 0)
            rs = slice(m * R, (m + 1) * R)
            xp_scr[rs, :] = jnp.where(sub == 0, carry_x[rs, :], rolled)
            carry_x[rs, :] = rolled

        pieces = {0: (GR, 0), 2: (GR, 1), 4: (V, 0), 6: (V, 1)}
        reset_rows = (lax.broadcasted_iota(jnp.int32, (SUBLANES, HEAD_DIM), 0) == 0) & (t == 0)
        _lru_coeffs(xp_scr, a_scr, b_scr, (pre_scr.at[U - 1], pre_scr.at[GS - 1]), conv_w, conv_b, wg, b_r, b_i, lam,
                    R, reset_rows,
                    before_head=lambda h: in_piece(*pieces[h]) if h in pieces else None)

        in_piece(U, 0)
        hl = jnp.zeros((R, WIDTH), F32)
        pr = jnp.ones((R, WIDTH), F32)
        for k in range(geo.seg_len):
            a_k = a_scr[k * R:(k + 1) * R, :]
            hl = a_k * hl + b_scr[k * R:(k + 1) * R, :]
            pr = a_k * pr
        c_in = carry_h[0:1, :]
        h0 = jnp.zeros((R, WIDTH), F32)
        for j in range(R):
            h0 = jnp.where(sub == j, c_in, h0)
            c_in = hl[j:j + 1, :] + pr[j:j + 1, :] * c_in
        carry_h[0:1, :] = c_in
        h_out[...] = c_in
        h = h0
        for k in range(geo.seg_len):
            h = a_scr[k * R:(k + 1) * R, :] * h + b_scr[k * R:(k + 1) * R, :]
            for c in range(N_SLABS):
                hp_scr[c, k * geo.step_pitch:k * geo.step_pitch + R, :] = h[:, c * LANES:(c + 1) * LANES]
        in_piece(U, 1)
        _to_natural_order(hp_scr, a_scr, geo)
        cat_cur[:, 0:WIDTH] = (a_scr[...] * _silu(pre_scr[GR - 1])).astype(BF16)

        in_piece(GS, 0)
        vb_scr[...] = _rms_norm(_gelu(pre_scr[V - 1]), sgu_g[...]).astype(BF16)
        between = {1: lambda: in_piece(GS, 1), 2: lambda: out_piece(2), 5: lambda: out_piece(3)}
        _sgu_branch(pre_scr, vb_scr, w_s, bs_rows, cat_cur,
                    before_head=lambda hd: between[hd]() if hd in between else None)
        finish_prev()


def _decode_tile(d, geo, xs_hbm, cs_hbm, hs_hbm, pre_g, post_g, conv_w, conv_b, wg, b_r, b_i, lam, sgu_g,
                 w_mix, bias_rows, ys_hbm, xrs_hbm, hs_out_hbm, vs_hbm, w_in, w_out, z_scr, xr_scr, xp_scr, a_scr, b_scr,
                 hp_scr, pre_scr, out_scr, vb_scr, cat, xs_buf, vs_buf, h0_buf, hn_buf, in_sem, out_sem):
    R, L = geo.n_seg, geo.seg_len
    rows = pl.ds(pl.multiple_of(d * TM, TM), TM)
    seqs = pl.ds(pl.multiple_of(d * R, R), R)

    copies_in = [pltpu.make_async_copy(xs_hbm.at[rows, :], xs_buf, in_sem.at[0]),
                 pltpu.make_async_copy(hs_hbm.at[seqs, :], h0_buf, in_sem.at[1])]
    copies_in += [pltpu.make_async_copy(cs_hbm.at[m, seqs, :], xp_scr.at[pl.ds(m * R, R), :], in_sem.at[2 + m])
                  for m in range(HIST)]
    for cp in copies_in:
        cp.start()
    for cp in copies_in:
        cp.wait()

    _pre_norm(xs_buf, pre_g, z_scr)

    xr = jnp.dot(z_scr[...], w_in[XR], preferred_element_type=F32)
    xr_stage = out_scr.at[:, pl.ds(0, WIDTH)]
    xr_stage[...] = xr
    xr_copy = pltpu.make_async_copy(xr_stage, xrs_hbm.at[rows, :], out_sem.at[0])
    xr_copy.start()
    _to_step_order(xr, xr_scr, xp_scr, geo)
    _lru_coeffs(xp_scr, a_scr, b_scr, (pre_scr.at[U - 1], pre_scr.at[GS - 1]), conv_w, conv_b, wg, b_r, b_i, lam, R,
                None)
    for slab in (GR, U, V, GS):
        pre_scr[slab - 1] = jnp.dot(z_scr[...], w_in[slab], preferred_element_type=F32)
    for c in range(N_SLABS):
        ls = slice(c * LANES, (c + 1) * LANES)
        h = h0_buf[:, ls]
        for s in range(L):
            h = a_scr[s * R:(s + 1) * R, ls] * h + b_scr[s * R:(s + 1) * R, ls]
            hp_scr[c, s * geo.step_pitch:s * geo.step_pitch + R, :] = h
        hn_buf[:, ls] = h
    _to_natural_order(hp_scr, a_scr, geo)
    cat[:, 0:WIDTH] = (a_scr[...] * _silu(pre_scr[GR - 1])).astype(BF16)

    vs_buf[...] = _rms_norm(_gelu(pre_scr[V - 1]), sgu_g[...])
    vb_scr[...] = vs_buf[...].astype(BF16)
    _sgu_branch(pre_scr, vb_scr, w_mix, bias_rows, cat)

    xr_copy.wait()
    for c in range(D_MODEL // WIDTH):
        out_scr[:, c * WIDTH:(c + 1) * WIDTH] = jnp.dot(cat[...], w_out[c], preferred_element_type=F32)
    xs_buf[...] = xs_buf[...] + _rms_norm(out_scr[...], post_g[...])

    copies_out = [pltpu.make_async_copy(xs_buf, ys_hbm.at[rows, :], out_sem.at[1]),
                  pltpu.make_async_copy(vs_buf, vs_hbm.at[rows, :], out_sem.at[2]),
                  pltpu.make_async_copy(hn_buf, hs_out_hbm.at[seqs, :], out_sem.at[3])]
    for cp in copies_out:
        cp.start()
    for cp in copies_out:
        cp.wait()


def _full(shape):
    return pl.BlockSpec(shape, lambda *_: (0,) * len(shape))


def _layer(x_prompt, x_sample, conv_state, h_state, pre_g, post_g, w_in, conv_w, conv_b, w_r, b_r, w_i, b_i, lam,
           sgu_g, w_s, b_s, w_out):
    n_seq, seq_len, d_model = x_prompt.shape
    dec_b, dec_t, _ = x_sample.shape
    assert d_model == D_MODEL and conv_w.shape == (CONV_WIDTH, WIDTH) and w_in.shape == (D_MODEL, 5 * WIDTH)
    assert seq_len % TM == 0 and dec_t * DEC_NB == TM and dec_b % DEC_NB == 0
    assert dec_t % SUBLANES == 0 and CHUNK % dec_t == 0 and HIST <= dec_t

    wg = jnp.concatenate([w_r, w_i], axis=-1).astype(BF16)
    row = lambda p: p.reshape(1, -1)
    bs_prompt = jnp.repeat(b_s.T, HEAD_DIM, axis=1)
    reps = CHUNK // dec_t
    ws_dec = jax.vmap(lambda w: jnp.kron(jnp.eye(reps, dtype=w.dtype), w))(w_s[:, :dec_t, :dec_t])
    bs_dec = jnp.tile(bs_prompt[:dec_t], (reps, 1))
    params = (row(pre_g), row(post_g), conv_w, row(conv_b), wg, row(b_r), row(b_i), row(lam), row(sgu_g), w_s,
              bs_prompt, ws_dec, bs_dec)

    tiles = seq_len // TM
    n_tiles = n_seq * tiles
    n_dec = dec_b // DEC_NB
    x_rows = x_prompt.reshape(n_seq * seq_len, d_model)
    xs_rows = x_sample.reshape(dec_b * dec_t, d_model)
    cs_steps = jnp.transpose(conv_state, (1, 0, 2))
    tile_of = lambda i: jnp.minimum(i, n_tiles - 1)
    prev_of = lambda i: jnp.clip(i - 1, 0, n_tiles - 1)
    any_spec = pl.BlockSpec(memory_space=pl.ANY)
    dec_geo = _decode_geometry(dec_t)
    relayout_rows = max(g.n_seg * g.seg_pitch for g in (PROMPT, dec_geo))
    assert relayout_rows >= max(g.seg_len * g.step_pitch for g in (PROMPT, dec_geo))
    y_rows, conv_p, h_p, ys_rows, xrs_rows, h_s, vs_rows = pl.pallas_call(
        functools.partial(_layer_kernel, tiles, n_tiles, dec_t),
        grid=(n_tiles + 1 + n_dec,),
        in_specs=[pl.BlockSpec((TM, d_model), lambda i: (tile_of(i), 0)),
                  pl.BlockSpec((TM, d_model), lambda i: (prev_of(i), 0)),
                  any_spec, any_spec, any_spec, any_spec, any_spec] + [_full(p.shape) for p in params],
        out_specs=[pl.BlockSpec((TM, d_model), lambda i: (prev_of(i), 0)),
                   pl.BlockSpec((None, HIST, WIDTH), lambda i: (tile_of(i) // tiles, 0, 0)),
                   pl.BlockSpec((None, 1, WIDTH), lambda i: (tile_of(i) // tiles, 0, 0)),
                   any_spec, any_spec, any_spec, any_spec],
        out_shape=[jax.ShapeDtypeStruct(x_rows.shape, F32),
                   jax.ShapeDtypeStruct((n_seq, HIST, WIDTH), F32),
                   jax.ShapeDtypeStruct((n_seq, 1, WIDTH), F32),
                   jax.ShapeDtypeStruct(xs_rows.shape, F32),
                   jax.ShapeDtypeStruct((dec_b * dec_t, WIDTH), F32),
                   jax.ShapeDtypeStruct(h_state.shape, F32),
                   jax.ShapeDtypeStruct((dec_b * dec_t, WIDTH), F32)],
        scratch_shapes=[pltpu.VMEM((5, D_MODEL, WIDTH), BF16),
                        pltpu.VMEM((D_MODEL // WIDTH, D_MODEL, WIDTH), BF16),
                        pltpu.VMEM((TM, D_MODEL), BF16),
                        pltpu.VMEM((N_SLABS, relayout_rows, LANES), F32),
                        pltpu.VMEM((TM + HIST * DEC_NB, WIDTH), F32),
                        pltpu.VMEM((TM, WIDTH), F32), pltpu.VMEM((TM, WIDTH), F32),
                        pltpu.VMEM((N_STAGE, TM, WIDTH), F32),
                        pltpu.VMEM((TM, D_MODEL), F32),
                        pltpu.VMEM((TM, WIDTH), BF16),
                        pltpu.VMEM((2, TM, D_MODEL), BF16),
                        pltpu.VMEM((SUBLANES, WIDTH), F32),
                        pltpu.VMEM((HIST * SUBLANES, WIDTH), F32),
                        pltpu.VMEM((TM, D_MODEL), F32),
                        pltpu.VMEM((DEC_NB, WIDTH), F32), pltpu.VMEM((DEC_NB, WIDTH), F32),
                        pltpu.SemaphoreType.DMA((N_STAGE,)),
                        pltpu.SemaphoreType.DMA((2 + HIST,)),
                        pltpu.SemaphoreType.DMA((4,))],
        compiler_params=pltpu.CompilerParams(dimension_semantics=("arbitrary",),
                                             vmem_limit_bytes=VMEM_LIMIT_BYTES),
        name="hybrid_layer",
    )(x_rows, x_rows, xs_rows, cs_steps, h_state, w_in, w_out, *params)
    conv_s = xrs_rows.reshape(dec_b, dec_t, WIDTH)[:, dec_t - HIST:, :]
    return (y_rows.reshape(x_prompt.shape), ys_rows.reshape(x_sample.shape), conv_p, h_p.reshape(n_seq, WIDTH),
            conv_s, h_s, vs_rows.reshape(dec_b, dec_t, WIDTH))


def kernel(x_prompt, x_sample, state_rglru_conv, state_rglru_h, pre_norm_g, post_norm_g, w_in, conv_w, conv_b,
           w_rgate, b_rgate, w_igate, b_igate, lru_lambda, sgu_norm_g, w_spatial, b_spatial, w_out):
    depth = w_in.shape[0]
    yp, ys = x_prompt, x_sample
    conv_p, h_p, conv_s, h_s, v_s = [], [], [], [], []
    for l in range(depth):
        yp, ys, cp, hp, cs, hs, vs = _layer(
            yp, ys, state_rglru_conv[l], state_rglru_h[l], pre_norm_g[l], post_norm_g[l], w_in[l], conv_w[l],
            conv_b[l], w_rgate[l], b_rgate[l], w_igate[l], b_igate[l], lru_lambda[l], sgu_norm_g[l], w_spatial[l],
            b_spatial[l], w_out[l])
        conv_p.append(cp)
        h_p.append(hp)
        conv_s.append(cs)
        h_s.append(hs)
        v_s.append(vs)
    return (yp, ys, jnp.stack(conv_p), jnp.stack(h_p), jnp.stack(conv_s), jnp.stack(h_s), jnp.stack(v_s))
```

```python
import collections
import functools
import math

import jax
import jax.numpy as jnp
from jax import lax
from jax.experimental import pallas as pl
from jax.experimental.pallas import tpu as pltpu

F32 = jnp.float32
BF16 = jnp.bfloat16

EPS = 1e-6
LRU_C = 8.0
CONV_WIDTH = 4
HIST = CONV_WIDTH - 1
N_HEADS = 8
HEAD_DIM = 128
CHUNK = 128
LANES = 128
SUBLANES = 8
N_SLABS = 8
MXU_DIM = 256

D_MODEL = 2048
WIDTH = 1024
TM = 256
MM_CHUNK = 512
XR, GR, U, V, GS = range(5)
N_STAGE = 4
VMEM_LIMIT_BYTES = 62 * 1024 * 1024

Geometry = collections.namedtuple("Geometry", "n_seg seg_len seg_pitch step_pitch")
PROMPT = Geometry(n_seg=SUBLANES, seg_len=TM // SUBLANES, seg_pitch=TM // SUBLANES + SUBLANES, step_pitch=SUBLANES)
DEC_NB = 32


def _decode_geometry(n_steps):
    return Geometry(n_seg=TM // n_steps, seg_len=n_steps, seg_pitch=n_steps, step_pitch=TM // n_steps + SUBLANES)


def _sigmoid(x):
    return 0.5 * jnp.tanh(0.5 * x) + 0.5


def _silu(x):
    return x * _sigmoid(x)


def _gelu(x):
    c = math.sqrt(2.0 / math.pi)
    return x * (0.5 * (1.0 + jnp.tanh(c * (x + 0.044715 * (x * x * x)))))


def _rms_norm(x, g):
    return x * lax.rsqrt(jnp.mean(x * x, axis=-1, keepdims=True) + EPS) * g


def _pre_norm(x_ref, g_ref, z_scr):
    x = x_ref[...]
    inv = lax.rsqrt(jnp.mean(x * x, axis=-1, keepdims=True) + EPS)
    for kb in range(D_MODEL // MXU_DIM):
        cs = slice(kb * MXU_DIM, (kb + 1) * MXU_DIM)
        z_scr[:, cs] = (x[:, cs] * inv * g_ref[:, cs]).astype(BF16)


def _to_step_order(xr, xr_scr, xp_scr, geo):
    R, L, P = geo.n_seg, geo.seg_len, geo.seg_pitch
    for s in range(R):
        for c in range(N_SLABS):
            xr_scr[c, s * P:s * P + L, :] = xr[s * L:(s + 1) * L, c * LANES:(c + 1) * LANES]
    for k in range(L):
        for g in range(R // SUBLANES):
            row = HIST * R + k * R + g * SUBLANES
            for c in range(N_SLABS):
                xp_scr[row:row + SUBLANES, c * LANES:(c + 1) * LANES] = (
                    xr_scr[c, pl.ds(g * SUBLANES * P + k, SUBLANES, stride=P), :])


def _to_natural_order(hp_scr, dst_scr, geo):
    R, L, P = geo.n_seg, geo.seg_len, geo.step_pitch
    for s in range(R):
        for m in range(L // SUBLANES):
            row = s * L + m * SUBLANES
            for c in range(N_SLABS):
                dst_scr[row:row + SUBLANES, c * LANES:(c + 1) * LANES] = (
                    hp_scr[c, pl.ds(m * SUBLANES * P + s, SUBLANES, stride=P), :])


def _lru_coeffs(xp_scr, a_scr, b_scr, gate_scr, conv_w, conv_b, wg, b_r, b_i, lam, rows_per_step, reset_rows,
                before_head=None):
    R = rows_per_step
    for h in range(N_HEADS):
        ls = slice(h * HEAD_DIM, (h + 1) * HEAD_DIM)
        xc = conv_b[:, ls] + xp_scr[0:TM, ls] * conv_w[0:1, ls]
        for k in range(1, CONV_WIDTH):
            xc = xc + xp_scr[k * R:k * R + TM, ls] * conv_w[k:k + 1, ls]
        b_scr[:, ls] = xc
    for h in range(N_HEADS):
        ls = slice(h * HEAD_DIM, (h + 1) * HEAD_DIM)
        gate_scr[h // 4][:, (h % 4) * 2 * HEAD_DIM:(h % 4 + 1) * 2 * HEAD_DIM] = jnp.dot(
            b_scr[:, ls].astype(BF16), wg[h], preferred_element_type=F32)
    for h in range(N_HEADS):
        if before_head is not None:
            before_head(h)
        ls = slice(h * HEAD_DIM, (h + 1) * HEAD_DIM)
        xc = b_scr[:, ls]
        g = gate_scr[h // 4][:, (h % 4) * 2 * HEAD_DIM:(h % 4 + 1) * 2 * HEAD_DIM]
        r = _sigmoid(g[:, :HEAD_DIM] + b_r[:, ls])
        i = _sigmoid(g[:, HEAD_DIM:] + b_i[:, ls])
        lam_h = lam[:, ls]
        softplus_neg = jnp.maximum(-lam_h, 0.0) + jnp.log1p(jnp.exp(-jnp.abs(lam_h)))
        log_a = r * (-LRU_C * softplus_neg)
        a = jnp.exp(log_a)
        mult = jnp.sqrt(-jnp.tanh(log_a) * (a * a + 1.0))
        ix = i * xc
        b = mult * ix
        a_scr[:, ls] = a
        b_scr[:, ls] = b
        if reset_rows is not None:
            a_scr[0:SUBLANES, ls] = jnp.where(reset_rows, 0.0, a[0:SUBLANES])
            b_scr[0:SUBLANES, ls] = jnp.where(reset_rows, ix[0:SUBLANES], b[0:SUBLANES])


def _sgu_branch(pre_scr, vb_scr, w_mix, bias_rows, cat, period=CHUNK, before_head=None):
    row = lax.broadcasted_iota(jnp.int32, (CHUNK, CHUNK), 0)
    col = lax.broadcasted_iota(jnp.int32, (CHUNK, CHUNK), 1)
    keep = row >= col
    if period < CHUNK:
        shift = period.bit_length() - 1
        keep = keep & ((row >> shift) == (col >> shift))
    for hd in range(N_HEADS):
        if before_head is not None:
            before_head(hd)
        ls = slice(hd * HEAD_DIM, (hd + 1) * HEAD_DIM)
        w_h = jnp.where(keep, jnp.tile(w_mix[hd], (CHUNK // period, 1)), 0.0).astype(BF16)
        bias = jnp.tile(bias_rows[0:period, ls], (CHUNK // period, 1))
        for ch in range(TM // CHUNK):
            rs = slice(ch * CHUNK, (ch + 1) * CHUNK)
            s = jnp.dot(w_h, vb_scr[rs, ls], preferred_element_type=F32) + bias
            cat[rs, WIDTH + hd * HEAD_DIM:WIDTH + (hd + 1) * HEAD_DIM] = (
                _gelu(pre_scr[U - 1, rs, ls]) * s * _silu(pre_scr[GS - 1, rs, ls])).astype(BF16)


def _load_weight(w_hbm, w_scr, stage, sem):
    n_row = w_hbm.shape[0] // TM
    n_chunks = w_scr.shape[0] * n_row

    def aligned(v, m):
        return v if isinstance(v, int) else pl.multiple_of(v, m)

    def copy(k, slot):
        rows = pl.ds(aligned((k % n_row) * TM, TM), TM)
        cols = pl.ds(aligned((k // n_row) * WIDTH, WIDTH), WIDTH)
        return pltpu.make_async_copy(w_hbm.at[rows, cols], stage.at[slot], sem.at[slot])

    for k in range(N_STAGE):
        copy(k, k).start()

    def body(k, carry):
        slot = k % N_STAGE
        copy(k, slot).wait()
        w_scr[k // n_row, pl.ds(aligned((k % n_row) * TM, TM), TM), :] = stage[slot].astype(BF16)

        @pl.when(k + N_STAGE < n_chunks)
        def _():
            copy(k + N_STAGE, slot).start()
        return carry

    lax.fori_loop(0, n_chunks, body, 0)


def _layer_kernel(tiles_per_seq, n_tiles, n_dec, dec_t,
                  x_ref, xprev_ref, xs_hbm, cs_hbm, hs_hbm, w_in_hbm, w_out_hbm,
                  pre_g, post_g, conv_w, conv_b, wg, b_r, b_i, lam, sgu_g, w_s, bs_rows, ws_dec,
                  y_ref, conv_out, h_out, ys_hbm, xrs_hbm, hs_out_hbm, vs_hbm,
                  w_in, w_out, z_scr, xr_scr, xp_scr, a_scr, b_scr, pre_scr, out_scr, vb_scr, cat_scr,
                  carry_h, carry_x, xs_buf, h0_buf, hn_buf, w_sem, in_sem, out_sem):
    hp_scr = xr_scr
    vs_buf = b_scr
    i = pl.program_id(0)
    t = i % tiles_per_seq
    cat_prev = cat_scr.at[(i + 1) % 2]
    cat_cur = cat_scr.at[i % 2]

    def out_piece(c):
        half = slice((c % 2) * MM_CHUNK, (c % 2 + 1) * MM_CHUNK)
        out_scr[:, c * MM_CHUNK:(c + 1) * MM_CHUNK] = jnp.dot(cat_prev[...], w_out[c // 2, :, half],
                                                              preferred_element_type=F32)

    def in_piece(slab, half):
        cs = slice(half * MM_CHUNK, (half + 1) * MM_CHUNK)
        pre_scr[slab - 1, :, cs] = jnp.dot(z_scr[...], w_in[slab, :, cs], preferred_element_type=F32)

    def finish_prev():
        y_ref[...] = xprev_ref[...] + _rms_norm(out_scr[...], post_g[...])

    @pl.when(i == 0)
    def _():
        _load_weight(w_in_hbm, w_in, pre_scr, w_sem)
        _load_weight(w_out_hbm, w_out, pre_scr, w_sem)
        cat_scr[1] = jnp.zeros(cat_scr.shape[1:], BF16)

    @pl.when(t == 0)
    def _():
        carry_h[...] = jnp.zeros_like(carry_h)
        carry_x[...] = jnp.zeros_like(carry_x)

    @pl.when(i == n_tiles)
    def _():
        for c in range(D_MODEL // MM_CHUNK):
            out_piece(c)
        finish_prev()

    @pl.when(i > n_tiles)
    def _():
        _decode_tile(i - n_tiles - 1, n_dec, _decode_geometry(dec_t), xs_hbm, cs_hbm, hs_hbm, pre_g, post_g, conv_w, conv_b,
                     wg, b_r, b_i, lam, sgu_g, ws_dec, bs_rows, ys_hbm, xrs_hbm, hs_out_hbm, vs_hbm, w_in, w_out, z_scr,
                     xr_scr, xp_scr, a_scr, b_scr, hp_scr, pre_scr, out_scr, vb_scr, cat_scr.at[0], xs_buf, vs_buf,
                     h0_buf, hn_buf, in_sem, out_sem)

    @pl.when(i < n_tiles)
    def _():
        geo = PROMPT
        R = geo.n_seg
        out_piece(0)
        _pre_norm(x_ref, pre_g, z_scr)

        xr = jnp.dot(z_scr[...], w_in[XR], preferred_element_type=F32)
        conv_out[...] = xr[TM - HIST:TM, :]
        out_piece(1)
        _to_step_order(xr, xr_scr, xp_scr, geo)
        sub = lax.broadcasted_iota(jnp.int32, (R, WIDTH), 0)
        for m in range(HIST):
            src = HIST * R + (geo.seg_len - HIST + m) * R
            rolled = pltpu.roll(xp_scr[src:src + R, :], 1, 0)
            rs = slice(m * R, (m + 1) * R)
            xp_scr[rs, :] = jnp.where(sub == 0, carry_x[rs, :], rolled)
            carry_x[rs, :] = rolled

        pieces = {0: (GR, 0), 2: (GR, 1), 4: (V, 0), 6: (V, 1)}
        reset_rows = (lax.broadcasted_iota(jnp.int32, (SUBLANES, HEAD_DIM), 0) == 0) & (t == 0)
        _lru_coeffs(xp_scr, a_scr, b_scr, (pre_scr.at[U - 1], pre_scr.at[GS - 1]), conv_w, conv_b, wg, b_r, b_i, lam,
                    R, reset_rows,
                    before_head=lambda h: in_piece(*pieces[h]) if h in pieces else None)

        in_piece(U, 0)
        hl = jnp.zeros((R, WIDTH), F32)
        pr = jnp.ones((R, WIDTH), F32)
        for k in range(geo.seg_len):
            a_k = a_scr[k * R:(k + 1) * R, :]
            hl = a_k * hl + b_scr[k * R:(k + 1) * R, :]
            pr = a_k * pr
        c_in = carry_h[0:1, :]
        h0 = jnp.zeros((R, WIDTH), F32)
        for j in range(R):
            h0 = jnp.where(sub == j, c_in, h0)
            c_in = hl[j:j + 1, :] + pr[j:j + 1, :] * c_in
        carry_h[0:1, :] = c_in
        h_out[...] = c_in
        h = h0
        for k in range(geo.seg_len):
            h = a_scr[k * R:(k + 1) * R, :] * h + b_scr[k * R:(k + 1) * R, :]
            for c in range(N_SLABS):
                hp_scr[c, k * geo.step_pitch:k * geo.step_pitch + R, :] = h[:, c * LANES:(c + 1) * LANES]
        in_piece(U, 1)
        _to_natural_order(hp_scr, a_scr, geo)
        cat_cur[:, 0:WIDTH] = (a_scr[...] * _silu(pre_scr[GR - 1])).astype(BF16)

        in_piece(GS, 0)
        vb_scr[...] = _rms_norm(_gelu(pre_scr[V - 1]), sgu_g[...]).astype(BF16)
        between = {1: lambda: in_piece(GS, 1), 2: lambda: out_piece(2), 5: lambda: out_piece(3)}
        _sgu_branch(pre_scr, vb_scr, w_s, bs_rows, cat_cur,
                    before_head=lambda hd: between[hd]() if hd in between else None)
        finish_prev()


def _decode_tile(d, n_dec, geo, xs_hbm, cs_hbm, hs_hbm, pre_g, post_g, conv_w, conv_b, wg, b_r, b_i, lam, sgu_g,
                 w_mix, bias_rows, ys_hbm, xrs_hbm, hs_out_hbm, vs_hbm, w_in, w_out, z_scr, xr_scr, xp_scr, a_scr, b_scr,
                 hp_scr, pre_scr, out_scr, vb_scr, cat, xs_buf, vs_buf, h0_buf, hn_buf, in_sem, out_sem):
    R, L = geo.n_seg, geo.seg_len
    tile_rows = lambda dd: pl.ds(pl.multiple_of(dd * TM, TM), TM)
    tile_seqs = lambda dd: pl.ds(pl.multiple_of(dd * R, R), R)
    rows, seqs = tile_rows(d), tile_seqs(d)

    def copies_out(dd):
        return [pltpu.make_async_copy(out_scr, ys_hbm.at[tile_rows(dd), :], out_sem.at[1]),
                pltpu.make_async_copy(vs_buf, vs_hbm.at[tile_rows(dd), :], out_sem.at[2]),
                pltpu.make_async_copy(hn_buf, hs_out_hbm.at[tile_seqs(dd), :], out_sem.at[3])]

    copies_in = [pltpu.make_async_copy(xs_hbm.at[rows, :], xs_buf, in_sem.at[0]),
                 pltpu.make_async_copy(hs_hbm.at[seqs, :], h0_buf, in_sem.at[1])]
    copies_in += [pltpu.make_async_copy(cs_hbm.at[m, seqs, :], xp_scr.at[pl.ds(m * R, R), :], in_sem.at[2 + m])
                  for m in range(HIST)]
    for cp in copies_in:
        cp.start()

    @pl.when(d > 0)
    def _():
        for cp in copies_out(d - 1):
            cp.wait()

    for cp in copies_in:
        cp.wait()

    _pre_norm(xs_buf, pre_g, z_scr)

    xr = jnp.dot(z_scr[...], w_in[XR], preferred_element_type=F32)
    xr_stage = out_scr.at[:, pl.ds(0, WIDTH)]
    xr_stage[...] = xr
    xr_copy = pltpu.make_async_copy(xr_stage, xrs_hbm.at[rows, :], out_sem.at[0])
    xr_copy.start()
    _to_step_order(xr, xr_scr, xp_scr, geo)
    _lru_coeffs(xp_scr, a_scr, b_scr, (pre_scr.at[U - 1], pre_scr.at[GS - 1]), conv_w, conv_b, wg, b_r, b_i, lam, R,
                None)
    for slab in (GR, U, V, GS):
        pre_scr[slab - 1] = jnp.dot(z_scr[...], w_in[slab], preferred_element_type=F32)
    for c in range(N_SLABS):
        ls = slice(c * LANES, (c + 1) * LANES)
        h = h0_buf[:, ls]
        for s in range(L):
            h = a_scr[s * R:(s + 1) * R, ls] * h + b_scr[s * R:(s + 1) * R, ls]
            hp_scr[c, s * geo.step_pitch:s * geo.step_pitch + R, :] = h
        hn_buf[:, ls] = h
    _to_natural_order(hp_scr, a_scr, geo)
    cat[:, 0:WIDTH] = (a_scr[...] * _silu(pre_scr[GR - 1])).astype(BF16)

    vs_buf[...] = _rms_norm(_gelu(pre_scr[V - 1]), sgu_g[...])
    vb_scr[...] = vs_buf[...].astype(BF16)
    _sgu_branch(pre_scr, vb_scr, w_mix, bias_rows, cat, period=L)

    xr_copy.wait()
    for c in range(D_MODEL // WIDTH):
        out_scr[:, c * WIDTH:(c + 1) * WIDTH] = jnp.dot(cat[...], w_out[c], preferred_element_type=F32)
    out_scr[...] = xs_buf[...] + _rms_norm(out_scr[...], post_g[...])

    for cp in copies_out(d):
        cp.start()

    @pl.when(d == n_dec - 1)
    def _():
        for cp in copies_out(d):
            cp.wait()


def _full(shape):
    return pl.BlockSpec(shape, lambda *_: (0,) * len(shape))


def _layer(x_prompt, x_sample, conv_state, h_state, pre_g, post_g, w_in, conv_w, conv_b, w_r, b_r, w_i, b_i, lam,
           sgu_g, w_s, b_s, w_out):
    n_seq, seq_len, d_model = x_prompt.shape
    dec_b, dec_t, _ = x_sample.shape
    assert d_model == D_MODEL and conv_w.shape == (CONV_WIDTH, WIDTH) and w_in.shape == (D_MODEL, 5 * WIDTH)
    assert seq_len % TM == 0 and dec_t * DEC_NB == TM and dec_b % DEC_NB == 0
    assert dec_t % SUBLANES == 0 and CHUNK % dec_t == 0 and HIST <= dec_t and dec_t & (dec_t - 1) == 0

    wg = jnp.concatenate([w_r, w_i], axis=-1).astype(BF16)
    row = lambda p: p.reshape(1, -1)
    bs_prompt = jnp.repeat(b_s.T, HEAD_DIM, axis=1)
    ws_dec = jnp.tile(w_s[:, :dec_t, :dec_t], (1, 1, CHUNK // dec_t))
    params = (row(pre_g), row(post_g), conv_w, row(conv_b), wg, row(b_r), row(b_i), row(lam), row(sgu_g), w_s,
              bs_prompt, ws_dec)

    tiles = seq_len // TM
    n_tiles = n_seq * tiles
    n_dec = dec_b // DEC_NB
    x_rows = x_prompt.reshape(n_seq * seq_len, d_model)
    xs_rows = x_sample.reshape(dec_b * dec_t, d_model)
    cs_steps = jnp.transpose(conv_state, (1, 0, 2))
    tile_of = lambda i: jnp.minimum(i, n_tiles - 1)
    prev_of = lambda i: jnp.clip(i - 1, 0, n_tiles - 1)
    any_spec = pl.BlockSpec(memory_space=pl.ANY)
    dec_geo = _decode_geometry(dec_t)
    relayout_rows = max(g.n_seg * g.seg_pitch for g in (PROMPT, dec_geo))
    assert relayout_rows >= max(g.seg_len * g.step_pitch for g in (PROMPT, dec_geo))
    y_rows, conv_p, h_p, ys_rows, xrs_rows, h_s, vs_rows = pl.pallas_call(
        functools.partial(_layer_kernel, tiles, n_tiles, n_dec, dec_t),
        grid=(n_tiles + 1 + n_dec,),
        in_specs=[pl.BlockSpec((TM, d_model), lambda i: (tile_of(i), 0)),
                  pl.BlockSpec((TM, d_model), lambda i: (prev_of(i), 0)),
                  any_spec, any_spec, any_spec, any_spec, any_spec] + [_full(p.shape) for p in params],
        out_specs=[pl.BlockSpec((TM, d_model), lambda i: (prev_of(i), 0)),
                   pl.BlockSpec((None, HIST, WIDTH), lambda i: (tile_of(i) // tiles, 0, 0)),
                   pl.BlockSpec((None, 1, WIDTH), lambda i: (tile_of(i) // tiles, 0, 0)),
                   any_spec, any_spec, any_spec, any_spec],
        out_shape=[jax.ShapeDtypeStruct(x_rows.shape, F32),
                   jax.ShapeDtypeStruct((n_seq, HIST, WIDTH), F32),
                   jax.ShapeDtypeStruct((n_seq, 1, WIDTH), F32),
                   jax.ShapeDtypeStruct(xs_rows.shape, F32),
                   jax.ShapeDtypeStruct((dec_b * dec_t, WIDTH), F32),
                   jax.ShapeDtypeStruct(h_state.shape, F32),
                   jax.ShapeDtypeStruct((dec_b * dec_t, WIDTH), F32)],
        scratch_shapes=[pltpu.VMEM((5, D_MODEL, WIDTH), BF16),
                        pltpu.VMEM((D_MODEL // WIDTH, D_MODEL, WIDTH), BF16),
                        pltpu.VMEM((TM, D_MODEL), BF16),
                        pltpu.VMEM((N_SLABS, relayout_rows, LANES), F32),
                        pltpu.VMEM((TM + HIST * DEC_NB, WIDTH), F32),
                        pltpu.VMEM((TM, WIDTH), F32), pltpu.VMEM((TM, WIDTH), F32),
                        pltpu.VMEM((N_STAGE, TM, WIDTH), F32),
                        pltpu.VMEM((TM, D_MODEL), F32),
                        pltpu.VMEM((TM, WIDTH), BF16),
                        pltpu.VMEM((2, TM, D_MODEL), BF16),
                        pltpu.VMEM((SUBLANES, WIDTH), F32),
                        pltpu.VMEM((HIST * SUBLANES, WIDTH), F32),
                        pltpu.VMEM((TM, D_MODEL), F32),
                        pltpu.VMEM((DEC_NB, WIDTH), F32), pltpu.VMEM((DEC_NB, WIDTH), F32),
                        pltpu.SemaphoreType.DMA((N_STAGE,)),
                        pltpu.SemaphoreType.DMA((2 + HIST,)),
                        pltpu.SemaphoreType.DMA((4,))],
        compiler_params=pltpu.CompilerParams(dimension_semantics=("arbitrary",),
                                             vmem_limit_bytes=VMEM_LIMIT_BYTES),
        name="hybrid_layer",
    )(x_rows, x_rows, xs_rows, cs_steps, h_state, w_in, w_out, *params)
    conv_s = xrs_rows.reshape(dec_b, dec_t, WIDTH)[:, dec_t - HIST:, :]
    return (y_rows.reshape(x_prompt.shape), ys_rows.reshape(x_sample.shape), conv_p, h_p.reshape(n_seq, WIDTH),
            conv_s, h_s, vs_rows.reshape(dec_b, dec_t, WIDTH))


def kernel(x_prompt, x_sample, state_rglru_conv, state_rglru_h, pre_norm_g, post_norm_g, w_in, conv_w, conv_b,
           w_rgate, b_rgate, w_igate, b_igate, lru_lambda, sgu_norm_g, w_spatial, b_spatial, w_out):
    depth = w_in.shape[0]
    yp, ys = x_prompt, x_sample
    conv_p, h_p, conv_s, h_s, v_s = [], [], [], [], []
    for l in range(depth):
        yp, ys, cp, hp, cs, hs, vs = _layer(
            yp, ys, state_rglru_conv[l], state_rglru_h[l], pre_norm_g[l], post_norm_g[l], w_in[l], conv_w[l],
            conv_b[l], w_rgate[l], b_rgate[l], w_igate[l], b_igate[l], lru_lambda[l], sgu_norm_g[l], w_spatial[l],
            b_spatial[l], w_out[l])
        conv_p.append(cp)
        h_p.append(hp)
        conv_s.append(cs)
        h_s.append(hs)
        v_s.append(vs)
    return (yp, ys, jnp.stack(conv_p), jnp.stack(h_p), jnp.stack(conv_s), jnp.stack(h_s), jnp.stack(v_s))
```

```python
import collections
import functools
import math

import jax
import jax.numpy as jnp
from jax import lax
from jax.experimental import pallas as pl
from jax.experimental.pallas import tpu as pltpu

F32 = jnp.float32
BF16 = jnp.bfloat16

EPS = 1e-6
LRU_C = 8.0
CONV_WIDTH = 4
HIST = CONV_WIDTH - 1
N_HEADS = 8
HEAD_DIM = 128
CHUNK = 128
LANES = 128
SUBLANES = 8
N_SLABS = 8
MXU_DIM = 256

D_MODEL = 2048
WIDTH = 1024
TM = 256
MM_CHUNK = 512
XR, GR, U, V, GS = range(5)
N_PRE = 4
N_STAGE = N_PRE + 2 * (D_MODEL // WIDTH)
VMEM_LIMIT_BYTES = 62 * 1024 * 1024

Geometry = collections.namedtuple("Geometry", "n_seg seg_len seg_pitch step_pitch")
PROMPT = Geometry(n_seg=SUBLANES, seg_len=TM // SUBLANES, seg_pitch=TM // SUBLANES + SUBLANES, step_pitch=SUBLANES)
DEC_NB = 32


def _decode_geometry(n_steps):
    return Geometry(n_seg=TM // n_steps, seg_len=n_steps, seg_pitch=n_steps, step_pitch=TM // n_steps + SUBLANES)


def _sigmoid(x):
    return 0.5 * jnp.tanh(0.5 * x) + 0.5


def _silu(x):
    return x * _sigmoid(x)


def _gelu(x):
    c = math.sqrt(2.0 / math.pi)
    return x * (0.5 * (1.0 + jnp.tanh(c * (x + 0.044715 * (x * x * x)))))


def _rms_norm(x, g):
    return x * lax.rsqrt(jnp.mean(x * x, axis=-1, keepdims=True) + EPS) * g


def _pre_norm(x_ref, g_ref, z_scr):
    x = x_ref[...]
    inv = lax.rsqrt(jnp.mean(x * x, axis=-1, keepdims=True) + EPS)
    for kb in range(D_MODEL // MXU_DIM):
        cs = slice(kb * MXU_DIM, (kb + 1) * MXU_DIM)
        z_scr[:, cs] = (x[:, cs] * inv * g_ref[:, cs]).astype(BF16)


def _to_step_order(xr, xr_scr, xp_scr, geo):
    R, L, P = geo.n_seg, geo.seg_len, geo.seg_pitch
    for s in range(R):
        for c in range(N_SLABS):
            xr_scr[c, s * P:s * P + L, :] = xr[s * L:(s + 1) * L, c * LANES:(c + 1) * LANES]
    for k in range(L):
        for g in range(R // SUBLANES):
            row = HIST * R + k * R + g * SUBLANES
            for c in range(N_SLABS):
                xp_scr[row:row + SUBLANES, c * LANES:(c + 1) * LANES] = (
                    xr_scr[c, pl.ds(g * SUBLANES * P + k, SUBLANES, stride=P), :])


def _to_natural_order(hp_scr, dst_scr, geo):
    R, L, P = geo.n_seg, geo.seg_len, geo.step_pitch
    for s in range(R):
        for m in range(L // SUBLANES):
            row = s * L + m * SUBLANES
            for c in range(N_SLABS):
                dst_scr[row:row + SUBLANES, c * LANES:(c + 1) * LANES] = (
                    hp_scr[c, pl.ds(m * SUBLANES * P + s, SUBLANES, stride=P), :])


def _lru_coeffs(xp_scr, a_scr, b_scr, gate_scr, conv_w, conv_b, wg, b_r, b_i, lam, rows_per_step, reset_rows,
                before_head=None):
    R = rows_per_step
    for h in range(N_HEADS):
        ls = slice(h * HEAD_DIM, (h + 1) * HEAD_DIM)
        xc = conv_b[:, ls] + xp_scr[0:TM, ls] * conv_w[0:1, ls]
        for k in range(1, CONV_WIDTH):
            xc = xc + xp_scr[k * R:k * R + TM, ls] * conv_w[k:k + 1, ls]
        b_scr[:, ls] = xc
    for h in range(N_HEADS):
        ls = slice(h * HEAD_DIM, (h + 1) * HEAD_DIM)
        gate_scr[h // 4][:, (h % 4) * 2 * HEAD_DIM:(h % 4 + 1) * 2 * HEAD_DIM] = jnp.dot(
            b_scr[:, ls].astype(BF16), wg[h], preferred_element_type=F32)
    for h in range(N_HEADS):
        if before_head is not None:
            before_head(h)
        ls = slice(h * HEAD_DIM, (h + 1) * HEAD_DIM)
        xc = b_scr[:, ls]
        g = gate_scr[h // 4][:, (h % 4) * 2 * HEAD_DIM:(h % 4 + 1) * 2 * HEAD_DIM]
        r = _sigmoid(g[:, :HEAD_DIM] + b_r[:, ls])
        i = _sigmoid(g[:, HEAD_DIM:] + b_i[:, ls])
        lam_h = lam[:, ls]
        softplus_neg = jnp.maximum(-lam_h, 0.0) + jnp.log1p(jnp.exp(-jnp.abs(lam_h)))
        log_a = r * (-LRU_C * softplus_neg)
        a = jnp.exp(log_a)
        mult = jnp.sqrt(-jnp.tanh(log_a) * (a * a + 1.0))
        ix = i * xc
        b = mult * ix
        a_scr[:, ls] = a
        b_scr[:, ls] = b
        if reset_rows is not None:
            a_scr[0:SUBLANES, ls] = jnp.where(reset_rows, 0.0, a[0:SUBLANES])
            b_scr[0:SUBLANES, ls] = jnp.where(reset_rows, ix[0:SUBLANES], b[0:SUBLANES])


def _sgu_branch(pre_scr, w_mix, bias_rows, cat, period=CHUNK, before_head=None):
    row = lax.broadcasted_iota(jnp.int32, (CHUNK, CHUNK), 0)
    col = lax.broadcasted_iota(jnp.int32, (CHUNK, CHUNK), 1)
    keep = row >= col
    if period < CHUNK:
        shift = period.bit_length() - 1
        keep = keep & ((row >> shift) == (col >> shift))
    for hd in range(N_HEADS):
        if before_head is not None:
            before_head(hd)
        ls = slice(hd * HEAD_DIM, (hd + 1) * HEAD_DIM)
        w_h = jnp.where(keep, jnp.tile(w_mix[hd], (CHUNK // period, 1)), 0.0).astype(BF16)
        bias = jnp.tile(bias_rows[0:period, ls], (CHUNK // period, 1))
        for ch in range(TM // CHUNK):
            rs = slice(ch * CHUNK, (ch + 1) * CHUNK)
            s = jnp.dot(w_h, cat[rs, WIDTH + hd * HEAD_DIM:WIDTH + (hd + 1) * HEAD_DIM],
                        preferred_element_type=F32) + bias
            cat[rs, WIDTH + hd * HEAD_DIM:WIDTH + (hd + 1) * HEAD_DIM] = (
                _gelu(pre_scr[U - 1, rs, ls]) * s * _silu(pre_scr[GS - 1, rs, ls])).astype(BF16)


def _load_weight(w_hbm, w_scr, stages, sem):
    n_stage = len(stages)
    n_row = w_hbm.shape[0] // TM
    n_chunks = w_scr.shape[0] * n_row
    assert n_chunks % n_stage == 0

    def aligned(v, m):
        return v if isinstance(v, int) else pl.multiple_of(v, m)

    def copy(k, slot):
        rows = pl.ds(aligned((k % n_row) * TM, TM), TM)
        cols = pl.ds(aligned((k // n_row) * WIDTH, WIDTH), WIDTH)
        return pltpu.make_async_copy(w_hbm.at[rows, cols], stages[slot], sem.at[slot])

    for k in range(n_stage):
        copy(k, k).start()

    def body(r, carry):
        for slot in range(n_stage):
            k = r * n_stage + slot
            copy(k, slot).wait()
            w_scr[k // n_row, pl.ds(aligned((k % n_row) * TM, TM), TM), :] = stages[slot][...].astype(BF16)

            @pl.when(k + n_stage < n_chunks)
            def _():
                copy(k + n_stage, slot).start()
        return carry

    lax.fori_loop(0, n_chunks // n_stage, body, 0)


def _layer_kernel(tiles_per_seq, n_tiles, n_dec, dec_t,
                  x_ref, xprev_ref, xs_hbm, cs_hbm, hs_hbm, w_in_hbm, w_out_hbm,
                  pre_g, post_g, conv_w, conv_b, wg, b_r, b_i, lam, sgu_g, w_s, bs_rows, ws_dec,
                  y_ref, conv_out, h_out, ys_hbm, xrs_hbm, hs_out_hbm, vs_hbm,
                  w_in, w_out, z_scr, xr_scr, xp_scr, a_scr, b_scr, pre_scr, out_scr, cat_scr,
                  carry_h, carry_x, xs_buf, h0_buf, hn_buf, w_sem, in_sem, out_sem):
    hp_scr = xr_scr
    vs_buf = b_scr
    i = pl.program_id(0)
    t = i % tiles_per_seq
    cat_prev = cat_scr.at[(i + 1) % 2]
    cat_cur = cat_scr.at[i % 2]

    sq_sums = []

    def out_piece(c):
        half = slice((c % 2) * MM_CHUNK, (c % 2 + 1) * MM_CHUNK)
        piece = jnp.dot(cat_prev[...], w_out[c // 2, :, half], preferred_element_type=F32)
        out_scr[:, c * MM_CHUNK:(c + 1) * MM_CHUNK] = piece
        sq_sums.append(jnp.sum(piece * piece, axis=-1, keepdims=True))

    def in_piece(slab, half):
        cs = slice(half * MM_CHUNK, (half + 1) * MM_CHUNK)
        pre_scr[slab - 1, :, cs] = jnp.dot(z_scr[...], w_in[slab, :, cs], preferred_element_type=F32)

    def finish_prev():
        inv = lax.rsqrt(functools.reduce(lambda p, q: p + q, sq_sums) * (1.0 / D_MODEL) + EPS)
        sq_sums.clear()
        y_ref[...] = xprev_ref[...] + out_scr[...] * inv * post_g[...]

    @pl.when(i == 0)
    def _():
        halves = lambda ref: [ref.at[:, pl.ds(c * WIDTH, WIDTH)] for c in range(D_MODEL // WIDTH)]
        stages = [pre_scr.at[s] for s in range(pre_scr.shape[0])] + halves(out_scr) + halves(xs_buf.at[0])
        _load_weight(w_in_hbm, w_in, stages, w_sem)
        _load_weight(w_out_hbm, w_out, stages, w_sem)
        cat_scr[1] = jnp.zeros(cat_scr.shape[1:], BF16)

    @pl.when(t == 0)
    def _():
        carry_h[...] = jnp.zeros_like(carry_h)
        carry_x[...] = jnp.zeros_like(carry_x)

    @pl.when(i == n_tiles)
    def _():
        x_in, h_in, cs_in = _decode_copies_in(0, _decode_geometry(dec_t), xs_hbm, cs_hbm, hs_hbm, xs_buf, xp_scr, h0_buf,
                                              in_sem)
        for cp in [x_in, h_in] + cs_in:
            cp.start()
        for c in range(D_MODEL // MM_CHUNK):
            out_piece(c)
        finish_prev()

    @pl.when(i > n_tiles)
    def _():
        _decode_tile(i - n_tiles - 1, n_dec, _decode_geometry(dec_t), xs_hbm, cs_hbm, hs_hbm, pre_g, post_g, conv_w, conv_b,
                     wg, b_r, b_i, lam, sgu_g, ws_dec, bs_rows, ys_hbm, xrs_hbm, hs_out_hbm, vs_hbm, w_in, w_out, z_scr,
                     xr_scr, xp_scr, a_scr, b_scr, hp_scr, pre_scr, out_scr, cat_scr.at[0], xs_buf, vs_buf,
                     h0_buf, hn_buf, in_sem, out_sem)

    @pl.when(i < n_tiles)
    def _():
        geo = PROMPT
        R = geo.n_seg
        out_piece(0)
        _pre_norm(x_ref, pre_g, z_scr)

        xr = jnp.dot(z_scr[...], w_in[XR], preferred_element_type=F32)
        conv_out[...] = xr[TM - HIST:TM, :]
        out_piece(1)
        _to_step_order(xr, xr_scr, xp_scr, geo)
        sub = lax.broadcasted_iota(jnp.int32, (R, WIDTH), 0)
        for m in range(HIST):
            src = HIST * R + (geo.seg_len - HIST + m) * R
            rolled = pltpu.roll(xp_scr[src:src + R, :], 1, 0)
            rs = slice(m * R, (m + 1) * R)
            xp_scr[rs, :] = jnp.where(sub == 0, carry_x[rs, :], rolled)
            carry_x[rs, :] = rolled

        pieces = {0: (GR, 0), 2: (GR, 1), 4: (V, 0), 6: (V, 1)}
        reset_rows = (lax.broadcasted_iota(jnp.int32, (SUBLANES, HEAD_DIM), 0) == 0) & (t == 0)
        _lru_coeffs(xp_scr, a_scr, b_scr, (pre_scr.at[U - 1], pre_scr.at[GS - 1]), conv_w, conv_b, wg, b_r, b_i, lam,
                    R, reset_rows,
                    before_head=lambda h: in_piece(*pieces[h]) if h in pieces else None)

        in_piece(U, 0)
        hl = jnp.zeros((R, WIDTH), F32)
        pr = jnp.ones((R, WIDTH), F32)
        for k in range(geo.seg_len):
            a_k = a_scr[k * R:(k + 1) * R, :]
            hl = a_k * hl + b_scr[k * R:(k + 1) * R, :]
            pr = a_k * pr
        c_in = carry_h[0:1, :]
        h0 = jnp.zeros((R, WIDTH), F32)
        for j in range(R):
            h0 = jnp.where(sub == j, c_in, h0)
            c_in = hl[j:j + 1, :] + pr[j:j + 1, :] * c_in
        carry_h[0:1, :] = c_in
        h_out[...] = c_in
        h = h0
        for k in range(geo.seg_len):
            h = a_scr[k * R:(k + 1) * R, :] * h + b_scr[k * R:(k + 1) * R, :]
            for c in range(N_SLABS):
                hp_scr[c, k * geo.step_pitch:k * geo.step_pitch + R, :] = h[:, c * LANES:(c + 1) * LANES]
        in_piece(U, 1)
        _to_natural_order(hp_scr, a_scr, geo)
        cat_cur[:, 0:WIDTH] = (a_scr[...] * _silu(pre_scr[GR - 1])).astype(BF16)

        in_piece(GS, 0)
        cat_cur[:, WIDTH:2 * WIDTH] = _rms_norm(_gelu(pre_scr[V - 1]), sgu_g[...]).astype(BF16)
        between = {1: lambda: in_piece(GS, 1), 2: lambda: out_piece(2), 5: lambda: out_piece(3)}
        _sgu_branch(pre_scr, w_s, bs_rows, cat_cur,
                    before_head=lambda hd: between[hd]() if hd in between else None)
        finish_prev()


def _decode_copies_in(d, geo, xs_hbm, cs_hbm, hs_hbm, xs_buf, xp_scr, h0_buf, in_sem):
    R = geo.n_seg
    seqs = pl.ds(pl.multiple_of(d * R, R), R)
    x_in = pltpu.make_async_copy(xs_hbm.at[pl.ds(pl.multiple_of(d * TM, TM), TM), :], xs_buf.at[d % 2], in_sem.at[0])
    h_in = pltpu.make_async_copy(hs_hbm.at[seqs, :], h0_buf, in_sem.at[1])
    cs_in = [pltpu.make_async_copy(cs_hbm.at[m, seqs, :], xp_scr.at[pl.ds(m * R, R), :], in_sem.at[2 + m])
             for m in range(HIST)]
    return x_in, h_in, cs_in


def _decode_tile(d, n_dec, geo, xs_hbm, cs_hbm, hs_hbm, pre_g, post_g, conv_w, conv_b, wg, b_r, b_i, lam, sgu_g,
                 w_mix, bias_rows, ys_hbm, xrs_hbm, hs_out_hbm, vs_hbm, w_in, w_out, z_scr, xr_scr, xp_scr, a_scr, b_scr,
                 hp_scr, pre_scr, out_scr, cat, xs_buf, vs_buf, h0_buf, hn_buf, in_sem, out_sem):
    R, L = geo.n_seg, geo.seg_len
    tile_rows = lambda dd: pl.ds(pl.multiple_of(dd * TM, TM), TM)
    tile_seqs = lambda dd: pl.ds(pl.multiple_of(dd * R, R), R)
    rows, seqs = tile_rows(d), tile_seqs(d)

    def copies_out(dd):
        return [pltpu.make_async_copy(out_scr, ys_hbm.at[tile_rows(dd), :], out_sem.at[1]),
                pltpu.make_async_copy(vs_buf, vs_hbm.at[tile_rows(dd), :], out_sem.at[2]),
                pltpu.make_async_copy(hn_buf, hs_out_hbm.at[tile_seqs(dd), :], out_sem.at[3])]

    x_in, h_in, cs_in = _decode_copies_in(d, geo, xs_hbm, cs_hbm, hs_hbm, xs_buf, xp_scr, h0_buf, in_sem)
    x_next, h_next, cs_next = _decode_copies_in(d + 1, geo, xs_hbm, cs_hbm, hs_hbm, xs_buf, xp_scr, h0_buf, in_sem)
    has_next = d + 1 < n_dec
    xs_cur = xs_buf.at[d % 2]
    for cp in [x_in, h_in] + cs_in:
        cp.wait()

    @pl.when(has_next)
    def _():
        x_next.start()

    @pl.when(d > 0)
    def _():
        for cp in copies_out(d - 1):
            cp.wait()

    _pre_norm(xs_cur, pre_g, z_scr)

    xr = jnp.dot(z_scr[...], w_in[XR], preferred_element_type=F32)
    xr_stage = out_scr.at[:, pl.ds(0, WIDTH)]
    xr_stage[...] = xr
    xr_copy = pltpu.make_async_copy(xr_stage, xrs_hbm.at[rows, :], out_sem.at[0])
    xr_copy.start()
    _to_step_order(xr, xr_scr, xp_scr, geo)
    _lru_coeffs(xp_scr, a_scr, b_scr, (pre_scr.at[U - 1], pre_scr.at[GS - 1]), conv_w, conv_b, wg, b_r, b_i, lam, R,
                None)

    @pl.when(has_next)
    def _():
        for cp in cs_next:
            cp.start()

    for slab in (GR, U, V, GS):
        pre_scr[slab - 1] = jnp.dot(z_scr[...], w_in[slab], preferred_element_type=F32)
    for c in range(N_SLABS):
        ls = slice(c * LANES, (c + 1) * LANES)
        h = h0_buf[:, ls]
        for s in range(L):
            h = a_scr[s * R:(s + 1) * R, ls] * h + b_scr[s * R:(s + 1) * R, ls]
            hp_scr[c, s * geo.step_pitch:s * geo.step_pitch + R, :] = h
        hn_buf[:, ls] = h

    @pl.when(has_next)
    def _():
        h_next.start()

    _to_natural_order(hp_scr, a_scr, geo)
    cat[:, 0:WIDTH] = (a_scr[...] * _silu(pre_scr[GR - 1])).astype(BF16)

    vs_buf[...] = _rms_norm(_gelu(pre_scr[V - 1]), sgu_g[...])
    cat[:, WIDTH:2 * WIDTH] = vs_buf[...].astype(BF16)
    _sgu_branch(pre_scr, w_mix, bias_rows, cat, period=L)

    xr_copy.wait()
    for c in range(D_MODEL // WIDTH):
        out_scr[:, c * WIDTH:(c + 1) * WIDTH] = jnp.dot(cat[...], w_out[c], preferred_element_type=F32)
    out_scr[...] = xs_cur[...] + _rms_norm(out_scr[...], post_g[...])

    for cp in copies_out(d):
        cp.start()

    @pl.when(d == n_dec - 1)
    def _():
        for cp in copies_out(d):
            cp.wait()


def _full(shape):
    return pl.BlockSpec(shape, lambda *_: (0,) * len(shape))


def _layer(x_prompt, x_sample, conv_state, h_state, pre_g, post_g, w_in, conv_w, conv_b, w_r, b_r, w_i, b_i, lam,
           sgu_g, w_s, b_s, w_out):
    n_seq, seq_len, d_model = x_prompt.shape
    dec_b, dec_t, _ = x_sample.shape
    assert d_model == D_MODEL and conv_w.shape == (CONV_WIDTH, WIDTH) and w_in.shape == (D_MODEL, 5 * WIDTH)
    assert seq_len % TM == 0 and dec_t * DEC_NB == TM and dec_b % DEC_NB == 0
    assert dec_t % SUBLANES == 0 and CHUNK % dec_t == 0 and HIST <= dec_t and dec_t & (dec_t - 1) == 0

    wg = jnp.concatenate([w_r, w_i], axis=-1).astype(BF16)
    row = lambda p: p.reshape(1, -1)
    bs_prompt = jnp.repeat(b_s.T, HEAD_DIM, axis=1)
    ws_dec = jnp.tile(w_s[:, :dec_t, :dec_t], (1, 1, CHUNK // dec_t))
    params = (row(pre_g), row(post_g), conv_w, row(conv_b), wg, row(b_r), row(b_i), row(lam), row(sgu_g), w_s,
              bs_prompt, ws_dec)

    tiles = seq_len // TM
    n_tiles = n_seq * tiles
    n_dec = dec_b // DEC_NB
    x_rows = x_prompt.reshape(n_seq * seq_len, d_model)
    xs_rows = x_sample.reshape(dec_b * dec_t, d_model)
    cs_steps = jnp.transpose(conv_state, (1, 0, 2))
    tile_of = lambda i: jnp.minimum(i, n_tiles - 1)
    prev_of = lambda i: jnp.clip(i - 1, 0, n_tiles - 1)
    any_spec = pl.BlockSpec(memory_space=pl.ANY)
    dec_geo = _decode_geometry(dec_t)
    relayout_rows = max(g.n_seg * g.seg_pitch for g in (PROMPT, dec_geo))
    assert relayout_rows >= max(g.seg_len * g.step_pitch for g in (PROMPT, dec_geo))
    y_rows, conv_p, h_p, ys_rows, xrs_rows, h_s, vs_rows = pl.pallas_call(
        functools.partial(_layer_kernel, tiles, n_tiles, n_dec, dec_t),
        grid=(n_tiles + 1 + n_dec,),
        in_specs=[pl.BlockSpec((TM, d_model), lambda i: (tile_of(i), 0)),
                  pl.BlockSpec((TM, d_model), lambda i: (prev_of(i), 0)),
                  any_spec, any_spec, any_spec, any_spec, any_spec] + [_full(p.shape) for p in params],
        out_specs=[pl.BlockSpec((TM, d_model), lambda i: (prev_of(i), 0)),
                   pl.BlockSpec((None, HIST, WIDTH), lambda i: (tile_of(i) // tiles, 0, 0)),
                   pl.BlockSpec((None, 1, WIDTH), lambda i: (tile_of(i) // tiles, 0, 0)),
                   any_spec, any_spec, any_spec, any_spec],
        out_shape=[jax.ShapeDtypeStruct(x_rows.shape, F32),
                   jax.ShapeDtypeStruct((n_seq, HIST, WIDTH), F32),
                   jax.ShapeDtypeStruct((n_seq, 1, WIDTH), F32),
                   jax.ShapeDtypeStruct(xs_rows.shape, F32),
                   jax.ShapeDtypeStruct((dec_b * dec_t, WIDTH), F32),
                   jax.ShapeDtypeStruct(h_state.shape, F32),
                   jax.ShapeDtypeStruct((dec_b * dec_t, WIDTH), F32)],
        scratch_shapes=[pltpu.VMEM((5, D_MODEL, WIDTH), BF16),
                        pltpu.VMEM((D_MODEL // WIDTH, D_MODEL, WIDTH), BF16),
                        pltpu.VMEM((TM, D_MODEL), BF16),
                        pltpu.VMEM((N_SLABS, relayout_rows, LANES), F32),
                        pltpu.VMEM((TM + HIST * DEC_NB, WIDTH), F32),
                        pltpu.VMEM((TM, WIDTH), F32), pltpu.VMEM((TM, WIDTH), F32),
                        pltpu.VMEM((N_PRE, TM, WIDTH), F32),
                        pltpu.VMEM((TM, D_MODEL), F32),
                        pltpu.VMEM((2, TM, D_MODEL), BF16),
                        pltpu.VMEM((SUBLANES, WIDTH), F32),
                        pltpu.VMEM((HIST * SUBLANES, WIDTH), F32),
                        pltpu.VMEM((2, TM, D_MODEL), F32),
                        pltpu.VMEM((DEC_NB, WIDTH), F32), pltpu.VMEM((DEC_NB, WIDTH), F32),
                        pltpu.SemaphoreType.DMA((N_STAGE,)),
                        pltpu.SemaphoreType.DMA((2 + HIST,)),
                        pltpu.SemaphoreType.DMA((4,))],
        compiler_params=pltpu.CompilerParams(dimension_semantics=("arbitrary",),
                                             vmem_limit_bytes=VMEM_LIMIT_BYTES),
        name="hybrid_layer",
    )(x_rows, x_rows, xs_rows, cs_steps, h_state, w_in, w_out, *params)
    conv_s = xrs_rows.reshape(dec_b, dec_t, WIDTH)[:, dec_t - HIST:, :]
    return (y_rows.reshape(x_prompt.shape), ys_rows.reshape(x_sample.shape), conv_p, h_p.reshape(n_seq, WIDTH),
            conv_s, h_s, vs_rows.reshape(dec_b, dec_t, WIDTH))


def kernel(x_prompt, x_sample, state_rglru_conv, state_rglru_h, pre_norm_g, post_norm_g, w_in, conv_w, conv_b,
           w_rgate, b_rgate, w_igate, b_igate, lru_lambda, sgu_norm_g, w_spatial, b_spatial, w_out):
    depth = w_in.shape[0]
    yp, ys = x_prompt, x_sample
    conv_p, h_p, conv_s, h_s, v_s = [], [], [], [], []
    for l in range(depth):
        yp, ys, cp, hp, cs, hs, vs = _layer(
            yp, ys, state_rglru_conv[l], state_rglru_h[l], pre_norm_g[l], post_norm_g[l], w_in[l], conv_w[l],
            conv_b[l], w_rgate[l], b_rgate[l], w_igate[l], b_igate[l], lru_lambda[l], sgu_norm_g[l], w_spatial[l],
            b_spatial[l], w_out[l])
        conv_p.append(cp)
        h_p.append(hp)
        conv_s.append(cs)
        h_s.append(hs)
        v_s.append(vs)
    return (yp, ys, jnp.stack(conv_p), jnp.stack(h_p), jnp.stack(conv_s), jnp.stack(h_s), jnp.stack(v_s))
```

```python
import collections
import functools
import math

import jax
import jax.numpy as jnp
from jax import lax
from jax.experimental import pallas as pl
from jax.experimental.pallas import tpu as pltpu

F32 = jnp.float32
BF16 = jnp.bfloat16

EPS = 1e-6
LRU_C = 8.0
CONV_WIDTH = 4
HIST = CONV_WIDTH - 1
N_HEADS = 8
HEAD_DIM = 128
CHUNK = 128
LANES = 128
SUBLANES = 8
N_SLABS = 8
MXU_DIM = 256

D_MODEL = 2048
WIDTH = 1024
TM = 256
MM_CHUNK = 512
XR, GR, U, V, GS = range(5)
N_PRE = 4
N_STAGE = N_PRE + 2 * (D_MODEL // WIDTH)
VMEM_LIMIT_BYTES = 62 * 1024 * 1024

Geometry = collections.namedtuple("Geometry", "n_seg seg_len seg_pitch step_pitch")
PROMPT = Geometry(n_seg=SUBLANES, seg_len=TM // SUBLANES, seg_pitch=TM // SUBLANES + SUBLANES, step_pitch=SUBLANES)
DEC_NB = 32


def _decode_geometry(n_steps):
    return Geometry(n_seg=TM // n_steps, seg_len=n_steps, seg_pitch=n_steps, step_pitch=TM // n_steps + SUBLANES)


def _sigmoid(x):
    return 0.5 * jnp.tanh(0.5 * x) + 0.5


def _silu(x):
    return x * _sigmoid(x)


def _gelu(x):
    c = math.sqrt(2.0 / math.pi)
    return x * (0.5 * (1.0 + jnp.tanh(c * (x + 0.044715 * (x * x * x)))))


def _rms_norm(x, g):
    return x * lax.rsqrt(jnp.mean(x * x, axis=-1, keepdims=True) + EPS) * g


def _pre_norm(x_ref, g_ref, z_scr):
    x = x_ref[...]
    inv = lax.rsqrt(jnp.mean(x * x, axis=-1, keepdims=True) + EPS)
    for kb in range(D_MODEL // MXU_DIM):
        cs = slice(kb * MXU_DIM, (kb + 1) * MXU_DIM)
        z_scr[:, cs] = (x[:, cs] * inv * g_ref[:, cs]).astype(BF16)


def _to_step_order(xr, xr_scr, xp_scr, geo):
    R, L, P = geo.n_seg, geo.seg_len, geo.seg_pitch
    if R == SUBLANES:
        for s in range(R):
            for m in range(L // SUBLANES):
                for c in range(N_SLABS):
                    xp_scr[c, pl.ds(HIST * R + m * SUBLANES * R + s, SUBLANES, stride=R), :] = (
                        xr[s * L + m * SUBLANES:s * L + (m + 1) * SUBLANES, c * LANES:(c + 1) * LANES])
        return
    for s in range(R):
        for c in range(N_SLABS):
            xr_scr[c, s * P:s * P + L, :] = xr[s * L:(s + 1) * L, c * LANES:(c + 1) * LANES]
    for k in range(L):
        for g in range(R // SUBLANES):
            row = HIST * R + k * R + g * SUBLANES
            for c in range(N_SLABS):
                xp_scr[c, row:row + SUBLANES, :] = xr_scr[c, pl.ds(g * SUBLANES * P + k, SUBLANES, stride=P), :]


def _to_natural_order(hp_scr, dst_scr, geo):
    R, L, P = geo.n_seg, geo.seg_len, geo.step_pitch
    for s in range(R):
        for m in range(L // SUBLANES):
            row = s * L + m * SUBLANES
            for c in range(N_SLABS):
                dst_scr[row:row + SUBLANES, c * LANES:(c + 1) * LANES] = (
                    hp_scr[c, pl.ds(m * SUBLANES * P + s, SUBLANES, stride=P), :])


def _lru_coeffs(xp_scr, a_scr, b_scr, gate_scr, conv_w, conv_b, wg, b_r, b_i, lam, rows_per_step, reset_rows,
                before_head=None):
    R = rows_per_step
    for h in range(N_HEADS):
        ls = slice(h * HEAD_DIM, (h + 1) * HEAD_DIM)
        xc = conv_b[:, ls] + xp_scr[h, 0:TM, :] * conv_w[0:1, ls]
        for k in range(1, CONV_WIDTH):
            xc = xc + xp_scr[h, k * R:k * R + TM, :] * conv_w[k:k + 1, ls]
        b_scr[:, ls] = xc
    for h in range(N_HEADS):
        ls = slice(h * HEAD_DIM, (h + 1) * HEAD_DIM)
        gate_scr[h // 4][:, (h % 4) * 2 * HEAD_DIM:(h % 4 + 1) * 2 * HEAD_DIM] = jnp.dot(
            b_scr[:, ls].astype(BF16), wg[h], preferred_element_type=F32)
    for h in range(N_HEADS):
        if before_head is not None:
            before_head(h)
        ls = slice(h * HEAD_DIM, (h + 1) * HEAD_DIM)
        xc = b_scr[:, ls]
        g = gate_scr[h // 4][:, (h % 4) * 2 * HEAD_DIM:(h % 4 + 1) * 2 * HEAD_DIM]
        r = _sigmoid(g[:, :HEAD_DIM] + b_r[:, ls])
        i = _sigmoid(g[:, HEAD_DIM:] + b_i[:, ls])
        lam_h = lam[:, ls]
        softplus_neg = jnp.maximum(-lam_h, 0.0) + jnp.log1p(jnp.exp(-jnp.abs(lam_h)))
        log_a = r * (-LRU_C * softplus_neg)
        a = jnp.exp(log_a)
        mult = jnp.sqrt(-jnp.tanh(log_a) * (a * a + 1.0))
        ix = i * xc
        b = mult * ix
        a_scr[:, ls] = a
        b_scr[:, ls] = b
        if reset_rows is not None:
            a_scr[0:SUBLANES, ls] = jnp.where(reset_rows, 0.0, a[0:SUBLANES])
            b_scr[0:SUBLANES, ls] = jnp.where(reset_rows, ix[0:SUBLANES], b[0:SUBLANES])


def _sgu_branch(pre_scr, w_mix, bias_rows, cat, period=CHUNK, before_head=None):
    row = lax.broadcasted_iota(jnp.int32, (CHUNK, CHUNK), 0)
    col = lax.broadcasted_iota(jnp.int32, (CHUNK, CHUNK), 1)
    keep = row >= col
    if period < CHUNK:
        shift = period.bit_length() - 1
        keep = keep & ((row >> shift) == (col >> shift))
    for hd in range(N_HEADS):
        if before_head is not None:
            before_head(hd)
        ls = slice(hd * HEAD_DIM, (hd + 1) * HEAD_DIM)
        w_h = jnp.where(keep, jnp.tile(w_mix[hd], (CHUNK // period, 1)), 0.0).astype(BF16)
        bias = jnp.tile(bias_rows[0:period, ls], (CHUNK // period, 1))
        for ch in range(TM // CHUNK):
            rs = slice(ch * CHUNK, (ch + 1) * CHUNK)
            s = jnp.dot(w_h, cat[rs, WIDTH + hd * HEAD_DIM:WIDTH + (hd + 1) * HEAD_DIM],
                        preferred_element_type=F32) + bias
            cat[rs, WIDTH + hd * HEAD_DIM:WIDTH + (hd + 1) * HEAD_DIM] = (
                _gelu(pre_scr[U - 1, rs, ls]) * s * _silu(pre_scr[GS - 1, rs, ls])).astype(BF16)


def _load_weight(w_hbm, w_scr, stages, sem):
    n_stage = len(stages)
    n_row = w_hbm.shape[0] // TM
    n_chunks = w_scr.shape[0] * n_row
    assert n_chunks % n_stage == 0

    def aligned(v, m):
        return v if isinstance(v, int) else pl.multiple_of(v, m)

    def copy(k, slot):
        rows = pl.ds(aligned((k % n_row) * TM, TM), TM)
        cols = pl.ds(aligned((k // n_row) * WIDTH, WIDTH), WIDTH)
        return pltpu.make_async_copy(w_hbm.at[rows, cols], stages[slot], sem.at[slot])

    for k in range(n_stage):
        copy(k, k).start()

    def body(r, carry):
        for slot in range(n_stage):
            k = r * n_stage + slot
            copy(k, slot).wait()
            w_scr[k // n_row, pl.ds(aligned((k % n_row) * TM, TM), TM), :] = stages[slot][...].astype(BF16)

            @pl.when(k + n_stage < n_chunks)
            def _():
                copy(k + n_stage, slot).start()
        return carry

    lax.fori_loop(0, n_chunks // n_stage, body, 0)


def _layer_kernel(tiles_per_seq, n_tiles, n_dec, dec_t,
                  x_ref, xprev_ref, xs_hbm, cs_hbm, hs_hbm, w_in_hbm, w_out_hbm,
                  pre_g, post_g, conv_w, conv_b, wg, b_r, b_i, lam, sgu_g, w_s, bs_rows, ws_dec,
                  y_ref, conv_out, h_out, ys_hbm, xrs_hbm, hs_out_hbm, vs_hbm,
                  w_in, w_out, z_scr, xr_scr, xp_scr, a_scr, b_scr, pre_scr, out_scr, cat_scr,
                  carry_h, carry_x, xs_buf, h0_buf, hn_buf, w_sem, in_sem, out_sem):
    hp_scr = xr_scr
    vs_buf = b_scr
    i = pl.program_id(0)
    t = i % tiles_per_seq
    cat_prev = cat_scr.at[(i + 1) % 2]
    cat_cur = cat_scr.at[i % 2]

    sq_sums = []

    def out_piece(c):
        half = slice((c % 2) * MM_CHUNK, (c % 2 + 1) * MM_CHUNK)
        piece = jnp.dot(cat_prev[...], w_out[c // 2, :, half], preferred_element_type=F32)
        out_scr[:, c * MM_CHUNK:(c + 1) * MM_CHUNK] = piece
        sq_sums.append(jnp.sum(piece * piece, axis=-1, keepdims=True))

    def in_piece(slab, half):
        cs = slice(half * MM_CHUNK, (half + 1) * MM_CHUNK)
        pre_scr[slab - 1, :, cs] = jnp.dot(z_scr[...], w_in[slab, :, cs], preferred_element_type=F32)

    def finish_prev():
        inv = lax.rsqrt(functools.reduce(lambda p, q: p + q, sq_sums) * (1.0 / D_MODEL) + EPS)
        sq_sums.clear()
        y_ref[...] = xprev_ref[...] + out_scr[...] * inv * post_g[...]

    @pl.when(i == 0)
    def _():
        halves = lambda ref: [ref.at[:, pl.ds(c * WIDTH, WIDTH)] for c in range(D_MODEL // WIDTH)]
        stages = [pre_scr.at[s] for s in range(pre_scr.shape[0])] + halves(out_scr) + halves(xs_buf.at[0])
        _load_weight(w_in_hbm, w_in, stages, w_sem)
        _load_weight(w_out_hbm, w_out, stages, w_sem)
        cat_scr[1] = jnp.zeros(cat_scr.shape[1:], BF16)

    @pl.when(t == 0)
    def _():
        carry_h[...] = jnp.zeros_like(carry_h)
        carry_x[...] = jnp.zeros_like(carry_x)

    @pl.when(i == n_tiles)
    def _():
        x_in, h_in, cs_in = _decode_copies_in(0, _decode_geometry(dec_t), xs_hbm, cs_hbm, hs_hbm, xs_buf, xp_scr, h0_buf,
                                              in_sem)
        for cp in [x_in, h_in] + cs_in:
            cp.start()
        for c in range(D_MODEL // MM_CHUNK):
            out_piece(c)
        finish_prev()

    @pl.when(i > n_tiles)
    def _():
        _decode_tile(i - n_tiles - 1, n_dec, _decode_geometry(dec_t), xs_hbm, cs_hbm, hs_hbm, pre_g, post_g, conv_w, conv_b,
                     wg, b_r, b_i, lam, sgu_g, ws_dec, bs_rows, ys_hbm, xrs_hbm, hs_out_hbm, vs_hbm, w_in, w_out, z_scr,
                     xr_scr, xp_scr, a_scr, b_scr, hp_scr, pre_scr, out_scr, cat_scr.at[0], xs_buf, vs_buf,
                     h0_buf, hn_buf, in_sem, out_sem)

    @pl.when(i < n_tiles)
    def _():
        geo = PROMPT
        R = geo.n_seg
        out_piece(0)
        _pre_norm(x_ref, pre_g, z_scr)

        xr = jnp.dot(z_scr[...], w_in[XR], preferred_element_type=F32)
        conv_out[...] = xr[TM - HIST:TM, :]
        out_piece(1)
        _to_step_order(xr, xr_scr, xp_scr, geo)
        sub = lax.broadcasted_iota(jnp.int32, (R, WIDTH), 0)
        for m in range(HIST):
            src = HIST * R + (geo.seg_len - HIST + m) * R
            rs = slice(m * R, (m + 1) * R)
            for c in range(N_SLABS):
                ls = slice(c * LANES, (c + 1) * LANES)
                rolled = pltpu.roll(xp_scr[c, src:src + R, :], 1, 0)
                xp_scr[c, rs, :] = jnp.where(sub[:, 0:LANES] == 0, carry_x[rs, ls], rolled)
                carry_x[rs, ls] = rolled

        pieces = {0: (GR, 0), 2: (GR, 1), 4: (V, 0), 6: (V, 1)}
        reset_rows = (lax.broadcasted_iota(jnp.int32, (SUBLANES, HEAD_DIM), 0) == 0) & (t == 0)
        _lru_coeffs(xp_scr, a_scr, b_scr, (pre_scr.at[U - 1], pre_scr.at[GS - 1]), conv_w, conv_b, wg, b_r, b_i, lam,
                    R, reset_rows,
                    before_head=lambda h: in_piece(*pieces[h]) if h in pieces else None)

        in_piece(U, 0)
        hl = jnp.zeros((R, WIDTH), F32)
        pr = jnp.ones((R, WIDTH), F32)
        for k in range(geo.seg_len):
            a_k = a_scr[k * R:(k + 1) * R, :]
            hl = a_k * hl + b_scr[k * R:(k + 1) * R, :]
            pr = a_k * pr
        c_in = carry_h[0:1, :]
        h0 = jnp.zeros((R, WIDTH), F32)
        for j in range(R):
            h0 = jnp.where(sub == j, c_in, h0)
            c_in = hl[j:j + 1, :] + pr[j:j + 1, :] * c_in
        carry_h[0:1, :] = c_in
        h_out[...] = c_in
        h = h0
        for k in range(geo.seg_len):
            h = a_scr[k * R:(k + 1) * R, :] * h + b_scr[k * R:(k + 1) * R, :]
            for c in range(N_SLABS):
                hp_scr[c, pl.ds(k, R, stride=geo.seg_pitch), :] = h[:, c * LANES:(c + 1) * LANES]
        in_piece(U, 1)
        for j in range(R):
            rs = slice(j * geo.seg_len, (j + 1) * geo.seg_len)
            for c in range(N_SLABS):
                ls = slice(c * LANES, (c + 1) * LANES)
                h_nat = hp_scr[c, j * geo.seg_pitch:j * geo.seg_pitch + geo.seg_len, :]
                cat_cur[rs, ls] = (h_nat * _silu(pre_scr[GR - 1, rs, ls])).astype(BF16)

        in_piece(GS, 0)
        cat_cur[:, WIDTH:2 * WIDTH] = _rms_norm(_gelu(pre_scr[V - 1]), sgu_g[...]).astype(BF16)
        between = {1: lambda: in_piece(GS, 1), 2: lambda: out_piece(2), 5: lambda: out_piece(3)}
        _sgu_branch(pre_scr, w_s, bs_rows, cat_cur,
                    before_head=lambda hd: between[hd]() if hd in between else None)
        finish_prev()


def _decode_copies_in(d, geo, xs_hbm, cs_hbm, hs_hbm, xs_buf, xp_scr, h0_buf, in_sem):
    R = geo.n_seg
    seqs = pl.ds(pl.multiple_of(d * R, R), R)
    x_in = pltpu.make_async_copy(xs_hbm.at[pl.ds(pl.multiple_of(d * TM, TM), TM), :], xs_buf.at[d % 2], in_sem.at[0])
    h_in = pltpu.make_async_copy(hs_hbm.at[seqs, :], h0_buf, in_sem.at[1])
    cs_in = [pltpu.make_async_copy(cs_hbm.at[m, seqs, pl.ds(c * LANES, LANES)], xp_scr.at[c, pl.ds(m * R, R), :],
                                   in_sem.at[2 + m * N_SLABS + c])
             for m in range(HIST) for c in range(N_SLABS)]
    return x_in, h_in, cs_in


def _decode_tile(d, n_dec, geo, xs_hbm, cs_hbm, hs_hbm, pre_g, post_g, conv_w, conv_b, wg, b_r, b_i, lam, sgu_g,
                 w_mix, bias_rows, ys_hbm, xrs_hbm, hs_out_hbm, vs_hbm, w_in, w_out, z_scr, xr_scr, xp_scr, a_scr, b_scr,
                 hp_scr, pre_scr, out_scr, cat, xs_buf, vs_buf, h0_buf, hn_buf, in_sem, out_sem):
    R, L = geo.n_seg, geo.seg_len
    tile_rows = lambda dd: pl.ds(pl.multiple_of(dd * TM, TM), TM)
    tile_seqs = lambda dd: pl.ds(pl.multiple_of(dd * R, R), R)
    rows, seqs = tile_rows(d), tile_seqs(d)

    def copies_out(dd):
        return [pltpu.make_async_copy(out_scr, ys_hbm.at[tile_rows(dd), :], out_sem.at[1]),
                pltpu.make_async_copy(vs_buf, vs_hbm.at[tile_rows(dd), :], out_sem.at[2]),
                pltpu.make_async_copy(hn_buf, hs_out_hbm.at[tile_seqs(dd), :], out_sem.at[3])]

    x_in, h_in, cs_in = _decode_copies_in(d, geo, xs_hbm, cs_hbm, hs_hbm, xs_buf, xp_scr, h0_buf, in_sem)
    x_next, h_next, cs_next = _decode_copies_in(d + 1, geo, xs_hbm, cs_hbm, hs_hbm, xs_buf, xp_scr, h0_buf, in_sem)
    has_next = d + 1 < n_dec
    xs_cur = xs_buf.at[d % 2]
    for cp in [x_in, h_in] + cs_in:
        cp.wait()

    @pl.when(has_next)
    def _():
        x_next.start()

    @pl.when(d > 0)
    def _():
        for cp in copies_out(d - 1):
            cp.wait()

    _pre_norm(xs_cur, pre_g, z_scr)

    xr = jnp.dot(z_scr[...], w_in[XR], preferred_element_type=F32)
    xr_stage = out_scr.at[:, pl.ds(0, WIDTH)]
    xr_stage[...] = xr
    xr_copy = pltpu.make_async_copy(xr_stage, xrs_hbm.at[rows, :], out_sem.at[0])
    xr_copy.start()
    _to_step_order(xr, xr_scr, xp_scr, geo)
    _lru_coeffs(xp_scr, a_scr, b_scr, (pre_scr.at[U - 1], pre_scr.at[GS - 1]), conv_w, conv_b, wg, b_r, b_i, lam, R,
                None)

    @pl.when(has_next)
    def _():
        for cp in cs_next:
            cp.start()

    for slab in (GR, U, V, GS):
        pre_scr[slab - 1] = jnp.dot(z_scr[...], w_in[slab], preferred_element_type=F32)
    for c in range(N_SLABS):
        ls = slice(c * LANES, (c + 1) * LANES)
        h = h0_buf[:, ls]
        for s in range(L):
            h = a_scr[s * R:(s + 1) * R, ls] * h + b_scr[s * R:(s + 1) * R, ls]
            hp_scr[c, s * geo.step_pitch:s * geo.step_pitch + R, :] = h
        hn_buf[:, ls] = h

    @pl.when(has_next)
    def _():
        h_next.start()

    _to_natural_order(hp_scr, a_scr, geo)
    cat[:, 0:WIDTH] = (a_scr[...] * _silu(pre_scr[GR - 1])).astype(BF16)

    vs_buf[...] = _rms_norm(_gelu(pre_scr[V - 1]), sgu_g[...])
    cat[:, WIDTH:2 * WIDTH] = vs_buf[...].astype(BF16)
    _sgu_branch(pre_scr, w_mix, bias_rows, cat, period=L)

    xr_copy.wait()
    for c in range(D_MODEL // WIDTH):
        out_scr[:, c * WIDTH:(c + 1) * WIDTH] = jnp.dot(cat[...], w_out[c], preferred_element_type=F32)
    out_scr[...] = xs_cur[...] + _rms_norm(out_scr[...], post_g[...])

    for cp in copies_out(d):
        cp.start()

    @pl.when(d == n_dec - 1)
    def _():
        for cp in copies_out(d):
            cp.wait()


def _full(shape):
    return pl.BlockSpec(shape, lambda *_: (0,) * len(shape))


def _layer(x_prompt, x_sample, conv_state, h_state, pre_g, post_g, w_in, conv_w, conv_b, w_r, b_r, w_i, b_i, lam,
           sgu_g, w_s, b_s, w_out):
    n_seq, seq_len, d_model = x_prompt.shape
    dec_b, dec_t, _ = x_sample.shape
    assert d_model == D_MODEL and conv_w.shape == (CONV_WIDTH, WIDTH) and w_in.shape == (D_MODEL, 5 * WIDTH)
    assert seq_len % TM == 0 and dec_t * DEC_NB == TM and dec_b % DEC_NB == 0
    assert dec_t % SUBLANES == 0 and CHUNK % dec_t == 0 and HIST <= dec_t and dec_t & (dec_t - 1) == 0

    wg = jnp.concatenate([w_r, w_i], axis=-1).astype(BF16)
    row = lambda p: p.reshape(1, -1)
    bs_prompt = jnp.repeat(b_s.T, HEAD_DIM, axis=1)
    ws_dec = jnp.tile(w_s[:, :dec_t, :dec_t], (1, 1, CHUNK // dec_t))
    params = (row(pre_g), row(post_g), conv_w, row(conv_b), wg, row(b_r), row(b_i), row(lam), row(sgu_g), w_s,
              bs_prompt, ws_dec)

    tiles = seq_len // TM
    n_tiles = n_seq * tiles
    n_dec = dec_b // DEC_NB
    x_rows = x_prompt.reshape(n_seq * seq_len, d_model)
    xs_rows = x_sample.reshape(dec_b * dec_t, d_model)
    cs_steps = jnp.transpose(conv_state, (1, 0, 2))
    tile_of = lambda i: jnp.minimum(i, n_tiles - 1)
    prev_of = lambda i: jnp.clip(i - 1, 0, n_tiles - 1)
    any_spec = pl.BlockSpec(memory_space=pl.ANY)
    dec_geo = _decode_geometry(dec_t)
    relayout_rows = max(g.n_seg * g.seg_pitch for g in (PROMPT, dec_geo))
    assert relayout_rows >= max(g.seg_len * g.step_pitch for g in (PROMPT, dec_geo))
    y_rows, conv_p, h_p, ys_rows, xrs_rows, h_s, vs_rows = pl.pallas_call(
        functools.partial(_layer_kernel, tiles, n_tiles, n_dec, dec_t),
        grid=(n_tiles + 1 + n_dec,),
        in_specs=[pl.BlockSpec((TM, d_model), lambda i: (tile_of(i), 0)),
                  pl.BlockSpec((TM, d_model), lambda i: (prev_of(i), 0)),
                  any_spec, any_spec, any_spec, any_spec, any_spec] + [_full(p.shape) for p in params],
        out_specs=[pl.BlockSpec((TM, d_model), lambda i: (prev_of(i), 0)),
                   pl.BlockSpec((None, HIST, WIDTH), lambda i: (tile_of(i) // tiles, 0, 0)),
                   pl.BlockSpec((None, 1, WIDTH), lambda i: (tile_of(i) // tiles, 0, 0)),
                   any_spec, any_spec, any_spec, any_spec],
        out_shape=[jax.ShapeDtypeStruct(x_rows.shape, F32),
                   jax.ShapeDtypeStruct((n_seq, HIST, WIDTH), F32),
                   jax.ShapeDtypeStruct((n_seq, 1, WIDTH), F32),
                   jax.ShapeDtypeStruct(xs_rows.shape, F32),
                   jax.ShapeDtypeStruct((dec_b * dec_t, WIDTH), F32),
                   jax.ShapeDtypeStruct(h_state.shape, F32),
                   jax.ShapeDtypeStruct((dec_b * dec_t, WIDTH), F32)],
        scratch_shapes=[pltpu.VMEM((5, D_MODEL, WIDTH), BF16),
                        pltpu.VMEM((D_MODEL // WIDTH, D_MODEL, WIDTH), BF16),
                        pltpu.VMEM((TM, D_MODEL), BF16),
                        pltpu.VMEM((N_SLABS, relayout_rows, LANES), F32),
                        pltpu.VMEM((N_SLABS, TM + HIST * DEC_NB, LANES), F32),
                        pltpu.VMEM((TM, WIDTH), F32), pltpu.VMEM((TM, WIDTH), F32),
                        pltpu.VMEM((N_PRE, TM, WIDTH), F32),
                        pltpu.VMEM((TM, D_MODEL), F32),
                        pltpu.VMEM((2, TM, D_MODEL), BF16),
                        pltpu.VMEM((SUBLANES, WIDTH), F32),
                        pltpu.VMEM((HIST * SUBLANES, WIDTH), F32),
                        pltpu.VMEM((2, TM, D_MODEL), F32),
                        pltpu.VMEM((DEC_NB, WIDTH), F32), pltpu.VMEM((DEC_NB, WIDTH), F32),
                        pltpu.SemaphoreType.DMA((N_STAGE,)),
                        pltpu.SemaphoreType.DMA((2 + HIST * N_SLABS,)),
                        pltpu.SemaphoreType.DMA((4,))],
        compiler_params=pltpu.CompilerParams(dimension_semantics=("arbitrary",),
                                             vmem_limit_bytes=VMEM_LIMIT_BYTES),
        name="hybrid_layer",
    )(x_rows, x_rows, xs_rows, cs_steps, h_state, w_in, w_out, *params)
    conv_s = xrs_rows.reshape(dec_b, dec_t, WIDTH)[:, dec_t - HIST:, :]
    return (y_rows.reshape(x_prompt.shape), ys_rows.reshape(x_sample.shape), conv_p, h_p.reshape(n_seq, WIDTH),
            conv_s, h_s, vs_rows.reshape(dec_b, dec_t, WIDTH))


def kernel(x_prompt, x_sample, state_rglru_conv, state_rglru_h, pre_norm_g, post_norm_g, w_in, conv_w, conv_b,
           w_rgate, b_rgate, w_igate, b_igate, lru_lambda, sgu_norm_g, w_spatial, b_spatial, w_out):
    depth = w_in.shape[0]
    yp, ys = x_prompt, x_sample
    conv_p, h_p, conv_s, h_s, v_s = [], [], [], [], []
    for l in range(depth):
        yp, ys, cp, hp, cs, hs, vs = _layer(
            yp, ys, state_rglru_conv[l], state_rglru_h[l], pre_norm_g[l], post_norm_g[l], w_in[l], conv_w[l],
            conv_b[l], w_rgate[l], b_rgate[l], w_igate[l], b_igate[l], lru_lambda[l], sgu_norm_g[l], w_spatial[l],
            b_spatial[l], w_out[l])
        conv_p.append(cp)
        h_p.append(hp)
        conv_s.append(cs)
        h_s.append(hs)
        v_s.append(vs)
    return (yp, ys, jnp.stack(conv_p), jnp.stack(h_p), jnp.stack(conv_s), jnp.stack(h_s), jnp.stack(v_s))
```

```python
import collections
import functools
import math

import jax
import jax.numpy as jnp
from jax import lax
from jax.experimental import pallas as pl
from jax.experimental.pallas import tpu as pltpu

F32 = jnp.float32
BF16 = jnp.bfloat16

EPS = 1e-6
LRU_C = 8.0
CONV_WIDTH = 4
HIST = CONV_WIDTH - 1
N_HEADS = 8
HEAD_DIM = 128
CHUNK = 128
LANES = 128
SUBLANES = 8
N_SLABS = 8
MXU_DIM = 256

D_MODEL = 2048
WIDTH = 1024
TM = 256
MM_CHUNK = 512
XR, GR, U, V, GS = range(5)
N_PRE = 4
N_STAGE = N_PRE + 2 * (D_MODEL // WIDTH)
VMEM_LIMIT_BYTES = 62 * 1024 * 1024

Geometry = collections.namedtuple("Geometry", "n_seg seg_len seg_pitch step_pitch")
PROMPT = Geometry(n_seg=SUBLANES, seg_len=TM // SUBLANES, seg_pitch=TM // SUBLANES + SUBLANES, step_pitch=SUBLANES)
DEC_NB = 32


def _decode_geometry(n_steps):
    return Geometry(n_seg=TM // n_steps, seg_len=n_steps, seg_pitch=n_steps, step_pitch=TM // n_steps + SUBLANES)


def _sigmoid(x):
    return 0.5 * jnp.tanh(0.5 * x) + 0.5


def _silu(x):
    return x * _sigmoid(x)


def _gelu(x):
    c = math.sqrt(2.0 / math.pi)
    return x * (0.5 * (1.0 + jnp.tanh(c * (x + 0.044715 * (x * x * x)))))


def _rms_norm(x, g):
    return x * lax.rsqrt(jnp.mean(x * x, axis=-1, keepdims=True) + EPS) * g


def _pre_norm(x_ref, g_ref, z_scr):
    x = x_ref[...]
    inv = lax.rsqrt(jnp.mean(x * x, axis=-1, keepdims=True) + EPS)
    for kb in range(D_MODEL // MXU_DIM):
        cs = slice(kb * MXU_DIM, (kb + 1) * MXU_DIM)
        z_scr[:, cs] = (x[:, cs] * inv * g_ref[:, cs]).astype(BF16)


def _to_step_order(xr, xr_scr, xp_scr, geo):
    R, L, P = geo.n_seg, geo.seg_len, geo.seg_pitch
    if R == SUBLANES:
        for s in range(R):
            for m in range(L // SUBLANES):
                for c in range(N_SLABS):
                    xp_scr[c, pl.ds(HIST * R + m * SUBLANES * R + s, SUBLANES, stride=R), :] = (
                        xr[s * L + m * SUBLANES:s * L + (m + 1) * SUBLANES, c * LANES:(c + 1) * LANES])
        return
    for s in range(R):
        for c in range(N_SLABS):
            xr_scr[c, s * P:s * P + L, :] = xr[s * L:(s + 1) * L, c * LANES:(c + 1) * LANES]
    for k in range(L):
        for g in range(R // SUBLANES):
            row = HIST * R + k * R + g * SUBLANES
            for c in range(N_SLABS):
                xp_scr[c, row:row + SUBLANES, :] = xr_scr[c, pl.ds(g * SUBLANES * P + k, SUBLANES, stride=P), :]


def _to_natural_order(hp_scr, dst_scr, geo):
    R, L, P = geo.n_seg, geo.seg_len, geo.step_pitch
    for s in range(R):
        for m in range(L // SUBLANES):
            row = s * L + m * SUBLANES
            for c in range(N_SLABS):
                dst_scr[row:row + SUBLANES, c * LANES:(c + 1) * LANES] = (
                    hp_scr[c, pl.ds(m * SUBLANES * P + s, SUBLANES, stride=P), :])


def _lru_coeffs(xp_scr, a_scr, b_scr, gate_scr, conv_w, conv_b, wg, b_r, b_i, lam, rows_per_step, reset_rows,
                before_head=None):
    R = rows_per_step
    for h in range(N_HEADS):
        ls = slice(h * HEAD_DIM, (h + 1) * HEAD_DIM)
        xc = conv_b[:, ls] + xp_scr[h, 0:TM, :] * conv_w[0:1, ls]
        for k in range(1, CONV_WIDTH):
            xc = xc + xp_scr[h, k * R:k * R + TM, :] * conv_w[k:k + 1, ls]
        b_scr[:, ls] = xc
    for h in range(N_HEADS):
        ls = slice(h * HEAD_DIM, (h + 1) * HEAD_DIM)
        gate_scr[h // 4][:, (h % 4) * 2 * HEAD_DIM:(h % 4 + 1) * 2 * HEAD_DIM] = jnp.dot(
            b_scr[:, ls].astype(BF16), wg[h], preferred_element_type=F32)
    for h in range(N_HEADS):
        if before_head is not None:
            before_head(h)
        ls = slice(h * HEAD_DIM, (h + 1) * HEAD_DIM)
        xc = b_scr[:, ls]
        g = gate_scr[h // 4][:, (h % 4) * 2 * HEAD_DIM:(h % 4 + 1) * 2 * HEAD_DIM]
        r = _sigmoid(g[:, :HEAD_DIM] + b_r[:, ls])
        i = _sigmoid(g[:, HEAD_DIM:] + b_i[:, ls])
        lam_h = lam[:, ls]
        softplus_neg = jnp.maximum(-lam_h, 0.0) + jnp.log1p(jnp.exp(-jnp.abs(lam_h)))
        log_a = r * (-LRU_C * softplus_neg)
        a = jnp.exp(log_a)
        mult = jnp.sqrt(-jnp.tanh(log_a) * (a * a + 1.0))
        ix = i * xc
        b = mult * ix
        a_scr[:, ls] = a
        b_scr[:, ls] = b
        if reset_rows is not None:
            a_scr[0:SUBLANES, ls] = jnp.where(reset_rows, 0.0, a[0:SUBLANES])
            b_scr[0:SUBLANES, ls] = jnp.where(reset_rows, ix[0:SUBLANES], b[0:SUBLANES])


def _sgu_branch(pre_scr, w_mix, bias_rows, cat, period=CHUNK, before_head=None):
    row = lax.broadcasted_iota(jnp.int32, (CHUNK, CHUNK), 0)
    col = lax.broadcasted_iota(jnp.int32, (CHUNK, CHUNK), 1)
    keep = row >= col
    if period < CHUNK:
        shift = period.bit_length() - 1
        keep = keep & ((row >> shift) == (col >> shift))
    for hd in range(N_HEADS):
        if before_head is not None:
            before_head(hd)
        ls = slice(hd * HEAD_DIM, (hd + 1) * HEAD_DIM)
        w_h = jnp.where(keep, jnp.tile(w_mix[hd], (CHUNK // period, 1)), 0.0).astype(BF16)
        bias = jnp.tile(bias_rows[0:period, ls], (CHUNK // period, 1))
        vs = slice(WIDTH + hd * HEAD_DIM, WIDTH + (hd + 1) * HEAD_DIM)
        blocks = [slice(ch * CHUNK, (ch + 1) * CHUNK) for ch in range(TM // CHUNK)]
        mixed = jnp.dot(w_h, jnp.concatenate([cat[rs, vs] for rs in blocks], axis=1), preferred_element_type=F32)
        for ch, rs in enumerate(blocks):
            s = mixed[:, ch * HEAD_DIM:(ch + 1) * HEAD_DIM] + bias
            cat[rs, vs] = (_gelu(pre_scr[U - 1, rs, ls]) * s * _silu(pre_scr[GS - 1, rs, ls])).astype(BF16)


def _load_weight(w_hbm, w_scr, stages, sem):
    n_stage = len(stages)
    n_row = w_hbm.shape[0] // TM
    n_chunks = w_scr.shape[0] * n_row
    assert n_chunks % n_stage == 0

    def aligned(v, m):
        return v if isinstance(v, int) else pl.multiple_of(v, m)

    def copy(k, slot):
        rows = pl.ds(aligned((k % n_row) * TM, TM), TM)
        cols = pl.ds(aligned((k // n_row) * WIDTH, WIDTH), WIDTH)
        return pltpu.make_async_copy(w_hbm.at[rows, cols], stages[slot], sem.at[slot])

    for k in range(n_stage):
        copy(k, k).start()

    def body(r, carry):
        for slot in range(n_stage):
            k = r * n_stage + slot
            copy(k, slot).wait()
            w_scr[k // n_row, pl.ds(aligned((k % n_row) * TM, TM), TM), :] = stages[slot][...].astype(BF16)

            @pl.when(k + n_stage < n_chunks)
            def _():
                copy(k + n_stage, slot).start()
        return carry

    lax.fori_loop(0, n_chunks // n_stage, body, 0)


def _layer_kernel(tiles_per_seq, n_tiles, n_dec, dec_t,
                  x_ref, xprev_ref, xs_hbm, cs_hbm, hs_hbm, w_in_hbm, w_out_hbm,
                  pre_g, post_g, conv_w, conv_b, wg, b_r, b_i, lam, sgu_g, w_s, bs_rows, ws_dec,
                  y_ref, conv_out, h_out, ys_hbm, xrs_hbm, hs_out_hbm, vs_hbm,
                  w_in, w_out, z_scr, xr_scr, xp_scr, a_scr, b_scr, pre_scr, out_scr, cat_scr,
                  carry_h, carry_x, xs_buf, h0_buf, hn_buf, w_sem, in_sem, out_sem):
    hp_scr = xr_scr
    vs_buf = b_scr
    i = pl.program_id(0)
    t = i % tiles_per_seq
    cat_prev = cat_scr.at[(i + 1) % 2]
    cat_cur = cat_scr.at[i % 2]

    sq_sums = []

    def out_piece(c):
        half = slice((c % 2) * MM_CHUNK, (c % 2 + 1) * MM_CHUNK)
        piece = jnp.dot(cat_prev[...], w_out[c // 2, :, half], preferred_element_type=F32)
        out_scr[:, c * MM_CHUNK:(c + 1) * MM_CHUNK] = piece
        sq_sums.append(jnp.sum(piece * piece, axis=-1, keepdims=True))

    def in_piece(slab, half):
        cs = slice(half * MM_CHUNK, (half + 1) * MM_CHUNK)
        pre_scr[slab - 1, :, cs] = jnp.dot(z_scr[...], w_in[slab, :, cs], preferred_element_type=F32)

    def finish_prev():
        inv = lax.rsqrt(functools.reduce(lambda p, q: p + q, sq_sums) * (1.0 / D_MODEL) + EPS)
        sq_sums.clear()
        y_ref[...] = xprev_ref[...] + out_scr[...] * inv * post_g[...]

    @pl.when(i == 0)
    def _():
        halves = lambda ref: [ref.at[:, pl.ds(c * WIDTH, WIDTH)] for c in range(D_MODEL // WIDTH)]
        stages = [pre_scr.at[s] for s in range(pre_scr.shape[0])] + halves(out_scr) + halves(xs_buf.at[0])
        _load_weight(w_in_hbm, w_in, stages, w_sem)
        _load_weight(w_out_hbm, w_out, stages, w_sem)
        cat_scr[1] = jnp.zeros(cat_scr.shape[1:], BF16)

    @pl.when(t == 0)
    def _():
        carry_h[...] = jnp.zeros_like(carry_h)
        carry_x[...] = jnp.zeros_like(carry_x)

    @pl.when(i == n_tiles)
    def _():
        x_in, h_in, cs_in = _decode_copies_in(0, _decode_geometry(dec_t), xs_hbm, cs_hbm, hs_hbm, xs_buf, xp_scr, h0_buf,
                                              in_sem)
        for cp in [x_in, h_in] + cs_in:
            cp.start()
        for c in range(D_MODEL // MM_CHUNK):
            out_piece(c)
        finish_prev()

    @pl.when(i > n_tiles)
    def _():
        _decode_tile(i - n_tiles - 1, n_dec, _decode_geometry(dec_t), xs_hbm, cs_hbm, hs_hbm, pre_g, post_g, conv_w, conv_b,
                     wg, b_r, b_i, lam, sgu_g, ws_dec, bs_rows, ys_hbm, xrs_hbm, hs_out_hbm, vs_hbm, w_in, w_out, z_scr,
                     xr_scr, xp_scr, a_scr, b_scr, hp_scr, pre_scr, out_scr, cat_scr.at[0], xs_buf, vs_buf,
                     h0_buf, hn_buf, in_sem, out_sem)

    @pl.when(i < n_tiles)
    def _():
        geo = PROMPT
        R = geo.n_seg
        out_piece(0)
        _pre_norm(x_ref, pre_g, z_scr)

        xr = jnp.dot(z_scr[...], w_in[XR], preferred_element_type=F32)
        conv_out[...] = xr[TM - HIST:TM, :]
        out_piece(1)
        _to_step_order(xr, xr_scr, xp_scr, geo)
        sub = lax.broadcasted_iota(jnp.int32, (R, WIDTH), 0)
        for m in range(HIST):
            src = HIST * R + (geo.seg_len - HIST + m) * R
            rs = slice(m * R, (m + 1) * R)
            for c in range(N_SLABS):
                ls = slice(c * LANES, (c + 1) * LANES)
                rolled = pltpu.roll(xp_scr[c, src:src + R, :], 1, 0)
                xp_scr[c, rs, :] = jnp.where(sub[:, 0:LANES] == 0, carry_x[rs, ls], rolled)
                carry_x[rs, ls] = rolled

        pieces = {0: (GR, 0), 2: (GR, 1), 4: (V, 0), 6: (V, 1)}
        reset_rows = (lax.broadcasted_iota(jnp.int32, (SUBLANES, HEAD_DIM), 0) == 0) & (t == 0)
        _lru_coeffs(xp_scr, a_scr, b_scr, (pre_scr.at[U - 1], pre_scr.at[GS - 1]), conv_w, conv_b, wg, b_r, b_i, lam,
                    R, reset_rows,
                    before_head=lambda h: in_piece(*pieces[h]) if h in pieces else None)

        in_piece(U, 0)
        hl = jnp.zeros((R, WIDTH), F32)
        pr = jnp.ones((R, WIDTH), F32)
        for k in range(geo.seg_len):
            a_k = a_scr[k * R:(k + 1) * R, :]
            hl = a_k * hl + b_scr[k * R:(k + 1) * R, :]
            pr = a_k * pr
        c_in = carry_h[0:1, :]
        h0 = jnp.zeros((R, WIDTH), F32)
        for j in range(R):
            h0 = jnp.where(sub == j, c_in, h0)
            c_in = hl[j:j + 1, :] + pr[j:j + 1, :] * c_in
        carry_h[0:1, :] = c_in
        h_out[...] = c_in
        h = h0
        for k in range(geo.seg_len):
            h = a_scr[k * R:(k + 1) * R, :] * h + b_scr[k * R:(k + 1) * R, :]
            for c in range(N_SLABS):
                hp_scr[c, pl.ds(k, R, stride=geo.seg_pitch), :] = h[:, c * LANES:(c + 1) * LANES]
        in_piece(U, 1)
        for j in range(R):
            rs = slice(j * geo.seg_len, (j + 1) * geo.seg_len)
            for c in range(N_SLABS):
                ls = slice(c * LANES, (c + 1) * LANES)
                h_nat = hp_scr[c, j * geo.seg_pitch:j * geo.seg_pitch + geo.seg_len, :]
                cat_cur[rs, ls] = (h_nat * _silu(pre_scr[GR - 1, rs, ls])).astype(BF16)

        in_piece(GS, 0)
        cat_cur[:, WIDTH:2 * WIDTH] = _rms_norm(_gelu(pre_scr[V - 1]), sgu_g[...]).astype(BF16)
        between = {1: lambda: in_piece(GS, 1), 2: lambda: out_piece(2), 5: lambda: out_piece(3)}
        _sgu_branch(pre_scr, w_s, bs_rows, cat_cur,
                    before_head=lambda hd: between[hd]() if hd in between else None)
        finish_prev()


def _decode_copies_in(d, geo, xs_hbm, cs_hbm, hs_hbm, xs_buf, xp_scr, h0_buf, in_sem):
    R = geo.n_seg
    seqs = pl.ds(pl.multiple_of(d * R, R), R)
    x_in = pltpu.make_async_copy(xs_hbm.at[pl.ds(pl.multiple_of(d * TM, TM), TM), :], xs_buf.at[d % 2], in_sem.at[0])
    h_in = pltpu.make_async_copy(hs_hbm.at[seqs, :], h0_buf, in_sem.at[1])
    cs_in = [pltpu.make_async_copy(cs_hbm.at[m, seqs, pl.ds(c * LANES, LANES)], xp_scr.at[c, pl.ds(m * R, R), :],
                                   in_sem.at[2 + m * N_SLABS + c])
             for m in range(HIST) for c in range(N_SLABS)]
    return x_in, h_in, cs_in


def _decode_tile(d, n_dec, geo, xs_hbm, cs_hbm, hs_hbm, pre_g, post_g, conv_w, conv_b, wg, b_r, b_i, lam, sgu_g,
                 w_mix, bias_rows, ys_hbm, xrs_hbm, hs_out_hbm, vs_hbm, w_in, w_out, z_scr, xr_scr, xp_scr, a_scr, b_scr,
                 hp_scr, pre_scr, out_scr, cat, xs_buf, vs_buf, h0_buf, hn_buf, in_sem, out_sem):
    R, L = geo.n_seg, geo.seg_len
    tile_rows = lambda dd: pl.ds(pl.multiple_of(dd * TM, TM), TM)
    tile_seqs = lambda dd: pl.ds(pl.multiple_of(dd * R, R), R)
    rows, seqs = tile_rows(d), tile_seqs(d)

    def copies_out(dd):
        return [pltpu.make_async_copy(out_scr, ys_hbm.at[tile_rows(dd), :], out_sem.at[1]),
                pltpu.make_async_copy(vs_buf, vs_hbm.at[tile_rows(dd), :], out_sem.at[2]),
                pltpu.make_async_copy(hn_buf, hs_out_hbm.at[tile_seqs(dd), :], out_sem.at[3])]

    x_in, h_in, cs_in = _decode_copies_in(d, geo, xs_hbm, cs_hbm, hs_hbm, xs_buf, xp_scr, h0_buf, in_sem)
    x_next, h_next, cs_next = _decode_copies_in(d + 1, geo, xs_hbm, cs_hbm, hs_hbm, xs_buf, xp_scr, h0_buf, in_sem)
    has_next = d + 1 < n_dec
    xs_cur = xs_buf.at[d % 2]
    for cp in [x_in, h_in] + cs_in:
        cp.wait()

    @pl.when(has_next)
    def _():
        x_next.start()

    @pl.when(d > 0)
    def _():
        for cp in copies_out(d - 1):
            cp.wait()

    _pre_norm(xs_cur, pre_g, z_scr)

    xr = jnp.dot(z_scr[...], w_in[XR], preferred_element_type=F32)
    xr_stage = out_scr.at[:, pl.ds(0, WIDTH)]
    xr_stage[...] = xr
    xr_copy = pltpu.make_async_copy(xr_stage, xrs_hbm.at[rows, :], out_sem.at[0])
    xr_copy.start()
    _to_step_order(xr, xr_scr, xp_scr, geo)
    _lru_coeffs(xp_scr, a_scr, b_scr, (pre_scr.at[U - 1], pre_scr.at[GS - 1]), conv_w, conv_b, wg, b_r, b_i, lam, R,
                None)

    @pl.when(has_next)
    def _():
        for cp in cs_next:
            cp.start()

    for slab in (GR, U, V, GS):
        pre_scr[slab - 1] = jnp.dot(z_scr[...], w_in[slab], preferred_element_type=F32)
    for c in range(N_SLABS):
        ls = slice(c * LANES, (c + 1) * LANES)
        h = h0_buf[:, ls]
        for s in range(L):
            h = a_scr[s * R:(s + 1) * R, ls] * h + b_scr[s * R:(s + 1) * R, ls]
            hp_scr[c, s * geo.step_pitch:s * geo.step_pitch + R, :] = h
        hn_buf[:, ls] = h

    @pl.when(has_next)
    def _():
        h_next.start()

    _to_natural_order(hp_scr, a_scr, geo)
    cat[:, 0:WIDTH] = (a_scr[...] * _silu(pre_scr[GR - 1])).astype(BF16)

    vs_buf[...] = _rms_norm(_gelu(pre_scr[V - 1]), sgu_g[...])
    cat[:, WIDTH:2 * WIDTH] = vs_buf[...].astype(BF16)
    _sgu_branch(pre_scr, w_mix, bias_rows, cat, period=L)

    xr_copy.wait()
    for c in range(D_MODEL // WIDTH):
        out_scr[:, c * WIDTH:(c + 1) * WIDTH] = jnp.dot(cat[...], w_out[c], preferred_element_type=F32)
    out_scr[...] = xs_cur[...] + _rms_norm(out_scr[...], post_g[...])

    for cp in copies_out(d):
        cp.start()

    @pl.when(d == n_dec - 1)
    def _():
        for cp in copies_out(d):
            cp.wait()


def _full(shape):
    return pl.BlockSpec(shape, lambda *_: (0,) * len(shape))


def _layer(x_prompt, x_sample, conv_state, h_state, pre_g, post_g, w_in, conv_w, conv_b, w_r, b_r, w_i, b_i, lam,
           sgu_g, w_s, b_s, w_out):
    n_seq, seq_len, d_model = x_prompt.shape
    dec_b, dec_t, _ = x_sample.shape
    assert d_model == D_MODEL and conv_w.shape == (CONV_WIDTH, WIDTH) and w_in.shape == (D_MODEL, 5 * WIDTH)
    assert seq_len % TM == 0 and dec_t * DEC_NB == TM and dec_b % DEC_NB == 0
    assert dec_t % SUBLANES == 0 and CHUNK % dec_t == 0 and HIST <= dec_t and dec_t & (dec_t - 1) == 0

    wg = jnp.concatenate([w_r, w_i], axis=-1).astype(BF16)
    row = lambda p: p.reshape(1, -1)
    bs_prompt = jnp.repeat(b_s.T, HEAD_DIM, axis=1)
    ws_dec = jnp.tile(w_s[:, :dec_t, :dec_t], (1, 1, CHUNK // dec_t))
    params = (row(pre_g), row(post_g), conv_w, row(conv_b), wg, row(b_r), row(b_i), row(lam), row(sgu_g), w_s,
              bs_prompt, ws_dec)

    tiles = seq_len // TM
    n_tiles = n_seq * tiles
    n_dec = dec_b // DEC_NB
    x_rows = x_prompt.reshape(n_seq * seq_len, d_model)
    xs_rows = x_sample.reshape(dec_b * dec_t, d_model)
    cs_steps = jnp.transpose(conv_state, (1, 0, 2))
    tile_of = lambda i: jnp.minimum(i, n_tiles - 1)
    prev_of = lambda i: jnp.clip(i - 1, 0, n_tiles - 1)
    any_spec = pl.BlockSpec(memory_space=pl.ANY)
    dec_geo = _decode_geometry(dec_t)
    relayout_rows = max(g.n_seg * g.seg_pitch for g in (PROMPT, dec_geo))
    assert relayout_rows >= max(g.seg_len * g.step_pitch for g in (PROMPT, dec_geo))
    y_rows, conv_p, h_p, ys_rows, xrs_rows, h_s, vs_rows = pl.pallas_call(
        functools.partial(_layer_kernel, tiles, n_tiles, n_dec, dec_t),
        grid=(n_tiles + 1 + n_dec,),
        in_specs=[pl.BlockSpec((TM, d_model), lambda i: (tile_of(i), 0)),
                  pl.BlockSpec((TM, d_model), lambda i: (prev_of(i), 0)),
                  any_spec, any_spec, any_spec, any_spec, any_spec] + [_full(p.shape) for p in params],
        out_specs=[pl.BlockSpec((TM, d_model), lambda i: (prev_of(i), 0)),
                   pl.BlockSpec((None, HIST, WIDTH), lambda i: (tile_of(i) // tiles, 0, 0)),
                   pl.BlockSpec((None, 1, WIDTH), lambda i: (tile_of(i) // tiles, 0, 0)),
                   any_spec, any_spec, any_spec, any_spec],
        out_shape=[jax.ShapeDtypeStruct(x_rows.shape, F32),
                   jax.ShapeDtypeStruct((n_seq, HIST, WIDTH), F32),
                   jax.ShapeDtypeStruct((n_seq, 1, WIDTH), F32),
                   jax.ShapeDtypeStruct(xs_rows.shape, F32),
                   jax.ShapeDtypeStruct((dec_b * dec_t, WIDTH), F32),
                   jax.ShapeDtypeStruct(h_state.shape, F32),
                   jax.ShapeDtypeStruct((dec_b * dec_t, WIDTH), F32)],
        scratch_shapes=[pltpu.VMEM((5, D_MODEL, WIDTH), BF16),
                        pltpu.VMEM((D_MODEL // WIDTH, D_MODEL, WIDTH), BF16),
                        pltpu.VMEM((TM, D_MODEL), BF16),
                        pltpu.VMEM((N_SLABS, relayout_rows, LANES), F32),
                        pltpu.VMEM((N_SLABS, TM + HIST * DEC_NB, LANES), F32),
                        pltpu.VMEM((TM, WIDTH), F32), pltpu.VMEM((TM, WIDTH), F32),
                        pltpu.VMEM((N_PRE, TM, WIDTH), F32),
                        pltpu.VMEM((TM, D_MODEL), F32),
                        pltpu.VMEM((2, TM, D_MODEL), BF16),
                        pltpu.VMEM((SUBLANES, WIDTH), F32),
                        pltpu.VMEM((HIST * SUBLANES, WIDTH), F32),
                        pltpu.VMEM((2, TM, D_MODEL), F32),
                        pltpu.VMEM((DEC_NB, WIDTH), F32), pltpu.VMEM((DEC_NB, WIDTH), F32),
                        pltpu.SemaphoreType.DMA((N_STAGE,)),
                        pltpu.SemaphoreType.DMA((2 + HIST * N_SLABS,)),
                        pltpu.SemaphoreType.DMA((4,))],
        compiler_params=pltpu.CompilerParams(dimension_semantics=("arbitrary",),
                                             vmem_limit_bytes=VMEM_LIMIT_BYTES),
        name="hybrid_layer",
    )(x_rows, x_rows, xs_rows, cs_steps, h_state, w_in, w_out, *params)
    conv_s = xrs_rows.reshape(dec_b, dec_t, WIDTH)[:, dec_t - HIST:, :]
    return (y_rows.reshape(x_prompt.shape), ys_rows.reshape(x_sample.shape), conv_p, h_p.reshape(n_seq, WIDTH),
            conv_s, h_s, vs_rows.reshape(dec_b, dec_t, WIDTH))


def kernel(x_prompt, x_sample, state_rglru_conv, state_rglru_h, pre_norm_g, post_norm_g, w_in, conv_w, conv_b,
           w_rgate, b_rgate, w_igate, b_igate, lru_lambda, sgu_norm_g, w_spatial, b_spatial, w_out):
    depth = w_in.shape[0]
    yp, ys = x_prompt, x_sample
    conv_p, h_p, conv_s, h_s, v_s = [], [], [], [], []
    for l in range(depth):
        yp, ys, cp, hp, cs, hs, vs = _layer(
            yp, ys, state_rglru_conv[l], state_rglru_h[l], pre_norm_g[l], post_norm_g[l], w_in[l], conv_w[l],
            conv_b[l], w_rgate[l], b_rgate[l], w_igate[l], b_igate[l], lru_lambda[l], sgu_norm_g[l], w_spatial[l],
            b_spatial[l], w_out[l])
        conv_p.append(cp)
        h_p.append(hp)
        conv_s.append(cs)
        h_s.append(hs)
        v_s.append(vs)
    return (yp, ys, jnp.stack(conv_p), jnp.stack(h_p), jnp.stack(conv_s), jnp.stack(h_s), jnp.stack(v_s))
```

```python
import collections
import functools
import math

import jax
import jax.numpy as jnp
from jax import lax
from jax.experimental import pallas as pl
from jax.experimental.pallas import tpu as pltpu

F32 = jnp.float32
BF16 = jnp.bfloat16

EPS = 1e-6
LRU_C = 8.0
CONV_WIDTH = 4
HIST = CONV_WIDTH - 1
N_HEADS = 8
HEAD_DIM = 128
CHUNK = 128
LANES = 128
SUBLANES = 8
N_SLABS = 8
MXU_DIM = 256

D_MODEL = 2048
WIDTH = 1024
TM = 256
XR, GR, U, V, GS = range(5)
N_PRE = 4
N_STAGE = N_PRE + 2 * (D_MODEL // WIDTH)
VMEM_LIMIT_BYTES = 62 * 1024 * 1024

Geometry = collections.namedtuple("Geometry", "n_seg seg_len seg_pitch step_pitch")
PROMPT = Geometry(n_seg=SUBLANES, seg_len=TM // SUBLANES, seg_pitch=TM // SUBLANES + SUBLANES, step_pitch=SUBLANES)
DEC_NB = 32


def _decode_geometry(n_steps):
    return Geometry(n_seg=TM // n_steps, seg_len=n_steps, seg_pitch=n_steps, step_pitch=TM // n_steps + SUBLANES)


def _sigmoid(x):
    return 0.5 * jnp.tanh(0.5 * x) + 0.5


def _silu(x):
    return x * _sigmoid(x)


def _gelu(x):
    c = math.sqrt(2.0 / math.pi)
    return x * (0.5 * (1.0 + jnp.tanh(c * (x + 0.044715 * (x * x * x)))))


def _rms_norm(x, g):
    return x * lax.rsqrt(jnp.mean(x * x, axis=-1, keepdims=True) + EPS) * g


def _pre_norm(x_ref, g_ref, z_scr):
    x = x_ref[...]
    inv = lax.rsqrt(jnp.mean(x * x, axis=-1, keepdims=True) + EPS)
    for kb in range(D_MODEL // MXU_DIM):
        cs = slice(kb * MXU_DIM, (kb + 1) * MXU_DIM)
        z_scr[:, cs] = (x[:, cs] * inv * g_ref[:, cs]).astype(BF16)


def _to_step_order(xr, xr_scr, xp_scr, geo):
    R, L, P = geo.n_seg, geo.seg_len, geo.seg_pitch
    if R == SUBLANES:
        for s in range(R):
            for m in range(L // SUBLANES):
                for c in range(N_SLABS):
                    xp_scr[c, pl.ds(HIST * R + m * SUBLANES * R + s, SUBLANES, stride=R), :] = (
                        xr[s * L + m * SUBLANES:s * L + (m + 1) * SUBLANES, c * LANES:(c + 1) * LANES])
        return
    for s in range(R):
        for c in range(N_SLABS):
            xr_scr[c, s * P:s * P + L, :] = xr[s * L:(s + 1) * L, c * LANES:(c + 1) * LANES]
    for k in range(L):
        for g in range(R // SUBLANES):
            row = HIST * R + k * R + g * SUBLANES
            for c in range(N_SLABS):
                xp_scr[c, row:row + SUBLANES, :] = xr_scr[c, pl.ds(g * SUBLANES * P + k, SUBLANES, stride=P), :]


def _to_natural_order(hp_scr, dst_scr, geo):
    R, L, P = geo.n_seg, geo.seg_len, geo.step_pitch
    for s in range(R):
        for m in range(L // SUBLANES):
            row = s * L + m * SUBLANES
            for c in range(N_SLABS):
                dst_scr[row:row + SUBLANES, c * LANES:(c + 1) * LANES] = (
                    hp_scr[c, pl.ds(m * SUBLANES * P + s, SUBLANES, stride=P), :])


def _lru_coeffs(xp_scr, a_scr, b_scr, gate_scr, conv_w, conv_b, wg, b_r, b_i, lam, rows_per_step, reset_rows,
                before_head=None):
    R = rows_per_step
    for h in range(N_HEADS):
        ls = slice(h * HEAD_DIM, (h + 1) * HEAD_DIM)
        xc = conv_b[:, ls] + xp_scr[h, 0:TM, :] * conv_w[0:1, ls]
        for k in range(1, CONV_WIDTH):
            xc = xc + xp_scr[h, k * R:k * R + TM, :] * conv_w[k:k + 1, ls]
        b_scr[:, ls] = xc
    for h in range(N_HEADS):
        ls = slice(h * HEAD_DIM, (h + 1) * HEAD_DIM)
        gate_scr[h // 4][:, (h % 4) * 2 * HEAD_DIM:(h % 4 + 1) * 2 * HEAD_DIM] = jnp.dot(
            b_scr[:, ls].astype(BF16), wg[h], preferred_element_type=F32)
    for h in range(N_HEADS):
        if before_head is not None:
            before_head(h)
        ls = slice(h * HEAD_DIM, (h + 1) * HEAD_DIM)
        xc = b_scr[:, ls]
        g = gate_scr[h // 4][:, (h % 4) * 2 * HEAD_DIM:(h % 4 + 1) * 2 * HEAD_DIM]
        r = _sigmoid(g[:, :HEAD_DIM] + b_r[:, ls])
        i = _sigmoid(g[:, HEAD_DIM:] + b_i[:, ls])
        lam_h = lam[:, ls]
        softplus_neg = jnp.maximum(-lam_h, 0.0) + jnp.log1p(jnp.exp(-jnp.abs(lam_h)))
        log_a = r * (-LRU_C * softplus_neg)
        a = jnp.exp(log_a)
        mult = jnp.sqrt(-jnp.tanh(log_a) * (a * a + 1.0))
        ix = i * xc
        b = mult * ix
        a_scr[:, ls] = a
        b_scr[:, ls] = b
        if reset_rows is not None:
            a_scr[0:SUBLANES, ls] = jnp.where(reset_rows, 0.0, a[0:SUBLANES])
            b_scr[0:SUBLANES, ls] = jnp.where(reset_rows, ix[0:SUBLANES], b[0:SUBLANES])


def _sgu_branch(pre_scr, w_mix, bias_rows, cat, period=CHUNK, before_head=None):
    row = lax.broadcasted_iota(jnp.int32, (CHUNK, CHUNK), 0)
    col = lax.broadcasted_iota(jnp.int32, (CHUNK, CHUNK), 1)
    keep = row >= col
    if period < CHUNK:
        shift = period.bit_length() - 1
        keep = keep & ((row >> shift) == (col >> shift))
    for hd in range(N_HEADS):
        if before_head is not None:
            before_head(hd)
        ls = slice(hd * HEAD_DIM, (hd + 1) * HEAD_DIM)
        w_h = jnp.where(keep, jnp.tile(w_mix[hd], (CHUNK // period, 1)), 0.0).astype(BF16)
        bias = jnp.tile(bias_rows[0:period, ls], (CHUNK // period, 1))
        vs = slice(WIDTH + hd * HEAD_DIM, WIDTH + (hd + 1) * HEAD_DIM)
        blocks = [slice(ch * CHUNK, (ch + 1) * CHUNK) for ch in range(TM // CHUNK)]
        mixed = jnp.dot(w_h, jnp.concatenate([cat[rs, vs] for rs in blocks], axis=1), preferred_element_type=F32)
        for ch, rs in enumerate(blocks):
            s = mixed[:, ch * HEAD_DIM:(ch + 1) * HEAD_DIM] + bias
            cat[rs, vs] = (_gelu(pre_scr[U - 1, rs, ls]) * s * _silu(pre_scr[GS - 1, rs, ls])).astype(BF16)


def _load_weight(w_hbm, w_scr, stages, sem):
    n_stage = len(stages)
    n_row = w_hbm.shape[0] // TM
    n_chunks = w_scr.shape[0] * n_row
    assert n_chunks % n_stage == 0

    def aligned(v, m):
        return v if isinstance(v, int) else pl.multiple_of(v, m)

    def copy(k, slot):
        rows = pl.ds(aligned((k % n_row) * TM, TM), TM)
        cols = pl.ds(aligned((k // n_row) * WIDTH, WIDTH), WIDTH)
        return pltpu.make_async_copy(w_hbm.at[rows, cols], stages[slot], sem.at[slot])

    for k in range(n_stage):
        copy(k, k).start()

    def body(r, carry):
        for slot in range(n_stage):
            k = r * n_stage + slot
            copy(k, slot).wait()
            w_scr[k // n_row, pl.ds(aligned((k % n_row) * TM, TM), TM), :] = stages[slot][...].astype(BF16)

            @pl.when(k + n_stage < n_chunks)
            def _():
                copy(k + n_stage, slot).start()
        return carry

    lax.fori_loop(0, n_chunks // n_stage, body, 0)


def _layer_kernel(tiles_per_seq, n_tiles, n_dec, dec_t,
                  x_ref, xprev_ref, xs_hbm, cs_hbm, hs_hbm, w_in_hbm, w_out_hbm,
                  pre_g, post_g, conv_w, conv_b, wg, b_r, b_i, lam, sgu_g, w_s, bs_rows, ws_dec,
                  y_ref, conv_out, h_out, ys_hbm, xrs_hbm, hs_out_hbm, vs_hbm,
                  w_in, w_out, z_scr, xr_scr, xp_scr, a_scr, b_scr, pre_scr, out_scr, cat_scr,
                  carry_h, carry_x, xs_buf, h0_buf, hn_buf, w_sem, in_sem, out_sem):
    hp_scr = xr_scr
    vs_buf = b_scr
    i = pl.program_id(0)
    t = i % tiles_per_seq
    cat_prev = cat_scr.at[(i + 1) % 2]
    cat_cur = cat_scr.at[i % 2]

    sq_sums = []

    def out_piece(c):
        piece = jnp.dot(cat_prev[...], w_out[c], preferred_element_type=F32)
        out_scr[:, c * WIDTH:(c + 1) * WIDTH] = piece
        sq_sums.append(jnp.sum(piece * piece, axis=-1, keepdims=True))

    def in_piece(slab):
        pre_scr[slab - 1] = jnp.dot(z_scr[...], w_in[slab], preferred_element_type=F32)

    def finish_prev():
        inv = lax.rsqrt(functools.reduce(lambda p, q: p + q, sq_sums) * (1.0 / D_MODEL) + EPS)
        sq_sums.clear()
        y_ref[...] = xprev_ref[...] + out_scr[...] * inv * post_g[...]

    @pl.when(i == 0)
    def _():
        halves = lambda ref: [ref.at[:, pl.ds(c * WIDTH, WIDTH)] for c in range(D_MODEL // WIDTH)]
        stages = [pre_scr.at[s] for s in range(pre_scr.shape[0])] + halves(out_scr) + halves(xs_buf)
        _load_weight(w_in_hbm, w_in, stages, w_sem)
        _load_weight(w_out_hbm, w_out, stages, w_sem)
        cat_scr[1] = jnp.zeros(cat_scr.shape[1:], BF16)

    @pl.when(t == 0)
    def _():
        carry_h[...] = jnp.zeros_like(carry_h)
        carry_x[...] = jnp.zeros_like(carry_x)

    @pl.when(i == n_tiles)
    def _():
        x_in, h_in, cs_in = _decode_copies_in(0, _decode_geometry(dec_t), xs_hbm, cs_hbm, hs_hbm, xs_buf, xp_scr, h0_buf,
                                              in_sem)
        for cp in [x_in, h_in] + cs_in:
            cp.start()
        for c in range(D_MODEL // WIDTH):
            out_piece(c)
        finish_prev()

    @pl.when(i > n_tiles)
    def _():
        _decode_tile(i - n_tiles - 1, n_dec, _decode_geometry(dec_t), xs_hbm, cs_hbm, hs_hbm, pre_g, post_g, conv_w, conv_b,
                     wg, b_r, b_i, lam, sgu_g, ws_dec, bs_rows, ys_hbm, xrs_hbm, hs_out_hbm, vs_hbm, w_in, w_out, z_scr,
                     xr_scr, xp_scr, a_scr, b_scr, hp_scr, pre_scr, out_scr, cat_scr.at[0], xs_buf, vs_buf,
                     h0_buf, hn_buf, in_sem, out_sem)

    @pl.when(i < n_tiles)
    def _():
        geo = PROMPT
        R = geo.n_seg
        _pre_norm(x_ref, pre_g, z_scr)

        xr = jnp.dot(z_scr[...], w_in[XR], preferred_element_type=F32)
        conv_out[...] = xr[TM - HIST:TM, :]
        _to_step_order(xr, xr_scr, xp_scr, geo)
        sub = lax.broadcasted_iota(jnp.int32, (R, WIDTH), 0)
        for m in range(HIST):
            src = HIST * R + (geo.seg_len - HIST + m) * R
            rs = slice(m * R, (m + 1) * R)
            for c in range(N_SLABS):
                ls = slice(c * LANES, (c + 1) * LANES)
                rolled = pltpu.roll(xp_scr[c, src:src + R, :], 1, 0)
                xp_scr[c, rs, :] = jnp.where(sub[:, 0:LANES] == 0, carry_x[rs, ls], rolled)
                carry_x[rs, ls] = rolled

        pieces = {0: GR, 4: V}
        reset_rows = (lax.broadcasted_iota(jnp.int32, (SUBLANES, HEAD_DIM), 0) == 0) & (t == 0)
        _lru_coeffs(xp_scr, a_scr, b_scr, (pre_scr.at[U - 1], pre_scr.at[GS - 1]), conv_w, conv_b, wg, b_r, b_i, lam,
                    R, reset_rows,
                    before_head=lambda h: in_piece(pieces[h]) if h in pieces else None)

        in_piece(U)
        hl =jnp.zeros((R, WIDTH), F32)
        pr = jnp.ones((R, WIDTH), F32)
        for k in range(geo.seg_len):
            a_k = a_scr[k * R:(k + 1) * R, :]
            hl = a_k * hl + b_scr[k * R:(k + 1) * R, :]
            pr = a_k * pr
        c_in = carry_h[0:1, :]
        h0 = jnp.zeros((R, WIDTH), F32)
        for j in range(R):
            h0 = jnp.where(sub == j, c_in, h0)
            c_in = hl[j:j + 1, :] + pr[j:j + 1, :] * c_in
        carry_h[0:1, :] = c_in
        h_out[...] = c_in
        h = h0
        for k in range(geo.seg_len):
            h = a_scr[k * R:(k + 1) * R, :] * h + b_scr[k * R:(k + 1) * R, :]
            for c in range(N_SLABS):
                hp_scr[c, pl.ds(k, R, stride=geo.seg_pitch), :] = h[:, c * LANES:(c + 1) * LANES]
        for j in range(R):
            rs = slice(j * geo.seg_len, (j + 1) * geo.seg_len)
            for c in range(N_SLABS):
                ls = slice(c * LANES, (c + 1) * LANES)
                h_nat = hp_scr[c, j * geo.seg_pitch:j * geo.seg_pitch + geo.seg_len, :]
                cat_cur[rs, ls] = (h_nat * _silu(pre_scr[GR - 1, rs, ls])).astype(BF16)

        in_piece(GS)
        cat_cur[:, WIDTH:2 * WIDTH] = _rms_norm(_gelu(pre_scr[V - 1]), sgu_g[...]).astype(BF16)
        between = {0: lambda: out_piece(0), 4: lambda: out_piece(1)}
        _sgu_branch(pre_scr, w_s, bs_rows, cat_cur,
                    before_head=lambda hd: between[hd]() if hd in between else None)
        finish_prev()


def _decode_copies_in(d, geo, xs_hbm, cs_hbm, hs_hbm, xs_buf, xp_scr, h0_buf, in_sem):
    R = geo.n_seg
    seqs = pl.ds(pl.multiple_of(d * R, R), R)
    x_in = pltpu.make_async_copy(xs_hbm.at[pl.ds(pl.multiple_of(d * TM, TM), TM), :], xs_buf, in_sem.at[0])
    h_in = pltpu.make_async_copy(hs_hbm.at[seqs, :], h0_buf, in_sem.at[1])
    cs_in = [pltpu.make_async_copy(cs_hbm.at[m, seqs, pl.ds(c * LANES, LANES)], xp_scr.at[c, pl.ds(m * R, R), :],
                                   in_sem.at[2 + m * N_SLABS + c])
             for m in range(HIST) for c in range(N_SLABS)]
    return x_in, h_in, cs_in


def _decode_tile(d, n_dec, geo, xs_hbm, cs_hbm, hs_hbm, pre_g, post_g, conv_w, conv_b, wg, b_r, b_i, lam, sgu_g,
                 w_mix, bias_rows, ys_hbm, xrs_hbm, hs_out_hbm, vs_hbm, w_in, w_out, z_scr, xr_scr, xp_scr, a_scr, b_scr,
                 hp_scr, pre_scr, out_scr, cat, xs_buf, vs_buf, h0_buf, hn_buf, in_sem, out_sem):
    R, L = geo.n_seg, geo.seg_len
    tile_rows = lambda dd: pl.ds(pl.multiple_of(dd * TM, TM), TM)
    tile_seqs = lambda dd: pl.ds(pl.multiple_of(dd * R, R), R)
    rows, seqs = tile_rows(d), tile_seqs(d)

    def copies_out(dd):
        return [pltpu.make_async_copy(out_scr, ys_hbm.at[tile_rows(dd), :], out_sem.at[1]),
                pltpu.make_async_copy(vs_buf, vs_hbm.at[tile_rows(dd), :], out_sem.at[2]),
                pltpu.make_async_copy(hn_buf, hs_out_hbm.at[tile_seqs(dd), :], out_sem.at[3])]

    x_in, h_in, cs_in = _decode_copies_in(d, geo, xs_hbm, cs_hbm, hs_hbm, xs_buf, xp_scr, h0_buf, in_sem)
    x_next, h_next, cs_next = _decode_copies_in(d + 1, geo, xs_hbm, cs_hbm, hs_hbm, xs_buf, xp_scr, h0_buf, in_sem)
    has_next = d + 1 < n_dec
    for cp in [x_in, h_in] + cs_in:
        cp.wait()

    @pl.when(d > 0)
    def _():
        for cp in copies_out(d - 1):
            cp.wait()

    _pre_norm(xs_buf, pre_g, z_scr)

    xr = jnp.dot(z_scr[...], w_in[XR], preferred_element_type=F32)
    xr_stage = out_scr.at[:, pl.ds(0, WIDTH)]
    xr_stage[...] = xr
    xr_copy = pltpu.make_async_copy(xr_stage, xrs_hbm.at[rows, :], out_sem.at[0])
    xr_copy.start()
    _to_step_order(xr, xr_scr, xp_scr, geo)
    _lru_coeffs(xp_scr, a_scr, b_scr, (pre_scr.at[U - 1], pre_scr.at[GS - 1]), conv_w, conv_b, wg, b_r, b_i, lam, R,
                None)

    @pl.when(has_next)
    def _():
        for cp in cs_next:
            cp.start()

    for slab in (GR, U, V, GS):
        pre_scr[slab - 1] = jnp.dot(z_scr[...], w_in[slab], preferred_element_type=F32)
    for c in range(N_SLABS):
        ls = slice(c * LANES, (c + 1) * LANES)
        h = h0_buf[:, ls]
        for s in range(L):
            h = a_scr[s * R:(s + 1) * R, ls] * h + b_scr[s * R:(s + 1) * R, ls]
            hp_scr[c, s * geo.step_pitch:s * geo.step_pitch + R, :] = h
        hn_buf[:, ls] = h

    @pl.when(has_next)
    def _():
        h_next.start()

    _to_natural_order(hp_scr, a_scr, geo)
    cat[:, 0:WIDTH] = (a_scr[...] * _silu(pre_scr[GR - 1])).astype(BF16)

    vs_buf[...] = _rms_norm(_gelu(pre_scr[V - 1]), sgu_g[...])
    cat[:, WIDTH:2 * WIDTH] = vs_buf[...].astype(BF16)
    _sgu_branch(pre_scr, w_mix, bias_rows, cat, period=L)

    xr_copy.wait()
    for c in range(D_MODEL // WIDTH):
        out_scr[:, c * WIDTH:(c + 1) * WIDTH] = jnp.dot(cat[...], w_out[c], preferred_element_type=F32)
    out_scr[...] = xs_buf[...] + _rms_norm(out_scr[...], post_g[...])

    for cp in copies_out(d):
        cp.start()

    @pl.when(has_next)
    def _():
        x_next.start()

    @pl.when(d == n_dec - 1)
    def _():
        for cp in copies_out(d):
            cp.wait()


def _full(shape):
    return pl.BlockSpec(shape, lambda *_: (0,) * len(shape))


def _layer(x_prompt, x_sample, conv_state, h_state, pre_g, post_g, w_in, conv_w, conv_b, w_r, b_r, w_i, b_i, lam,
           sgu_g, w_s, b_s, w_out):
    n_seq, seq_len, d_model = x_prompt.shape
    dec_b, dec_t, _ = x_sample.shape
    assert d_model == D_MODEL and conv_w.shape == (CONV_WIDTH, WIDTH) and w_in.shape == (D_MODEL, 5 * WIDTH)
    assert seq_len % TM == 0 and dec_t * DEC_NB == TM and dec_b % DEC_NB == 0
    assert dec_t % SUBLANES == 0 and CHUNK % dec_t == 0 and HIST <= dec_t and dec_t & (dec_t - 1) == 0

    wg = jnp.concatenate([w_r, w_i], axis=-1).astype(BF16)
    row = lambda p: p.reshape(1, -1)
    bs_prompt = jnp.repeat(b_s.T, HEAD_DIM, axis=1)
    ws_dec = jnp.tile(w_s[:, :dec_t, :dec_t], (1, 1, CHUNK // dec_t))
    params = (row(pre_g), row(post_g), conv_w, row(conv_b), wg, row(b_r), row(b_i), row(lam), row(sgu_g), w_s,
              bs_prompt, ws_dec)

    tiles = seq_len // TM
    n_tiles = n_seq * tiles
    n_dec = dec_b // DEC_NB
    x_rows = x_prompt.reshape(n_seq * seq_len, d_model)
    xs_rows = x_sample.reshape(dec_b * dec_t, d_model)
    cs_steps = jnp.transpose(conv_state, (1, 0, 2))
    tile_of = lambda i: jnp.minimum(i, n_tiles - 1)
    prev_of = lambda i: jnp.clip(i - 1, 0, n_tiles - 1)
    any_spec = pl.BlockSpec(memory_space=pl.ANY)
    dec_geo = _decode_geometry(dec_t)
    relayout_rows = max(g.n_seg * g.seg_pitch for g in (PROMPT, dec_geo))
    assert relayout_rows >= max(g.seg_len * g.step_pitch for g in (PROMPT, dec_geo))
    y_rows, conv_p, h_p, ys_rows, xrs_rows, h_s, vs_rows = pl.pallas_call(
        functools.partial(_layer_kernel, tiles, n_tiles, n_dec, dec_t),
        grid=(n_tiles + 1 + n_dec,),
        in_specs=[pl.BlockSpec((TM, d_model), lambda i: (tile_of(i), 0)),
                  pl.BlockSpec((TM, d_model), lambda i: (prev_of(i), 0)),
                  any_spec, any_spec, any_spec, any_spec, any_spec] + [_full(p.shape) for p in params],
        out_specs=[pl.BlockSpec((TM, d_model), lambda i: (prev_of(i), 0)),
                   pl.BlockSpec((None, HIST, WIDTH), lambda i: (tile_of(i) // tiles, 0, 0)),
                   pl.BlockSpec((None, 1, WIDTH), lambda i: (tile_of(i) // tiles, 0, 0)),
                   any_spec, any_spec, any_spec, any_spec],
        out_shape=[jax.ShapeDtypeStruct(x_rows.shape, F32),
                   jax.ShapeDtypeStruct((n_seq, HIST, WIDTH), F32),
                   jax.ShapeDtypeStruct((n_seq, 1, WIDTH), F32),
                   jax.ShapeDtypeStruct(xs_rows.shape, F32),
                   jax.ShapeDtypeStruct((dec_b * dec_t, WIDTH), F32),
                   jax.ShapeDtypeStruct(h_state.shape, F32),
                   jax.ShapeDtypeStruct((dec_b * dec_t, WIDTH), F32)],
        scratch_shapes=[pltpu.VMEM((5, D_MODEL, WIDTH), BF16),
                        pltpu.VMEM((D_MODEL // WIDTH, D_MODEL, WIDTH), BF16),
                        pltpu.VMEM((TM, D_MODEL), BF16),
                        pltpu.VMEM((N_SLABS, relayout_rows, LANES), F32),
                        pltpu.VMEM((N_SLABS, TM + HIST * DEC_NB, LANES), F32),
                        pltpu.VMEM((TM, WIDTH), F32), pltpu.VMEM((TM, WIDTH), F32),
                        pltpu.VMEM((N_PRE, TM, WIDTH), F32),
                        pltpu.VMEM((TM, D_MODEL), F32),
                        pltpu.VMEM((2, TM, D_MODEL), BF16),
                        pltpu.VMEM((SUBLANES, WIDTH), F32),
                        pltpu.VMEM((HIST * SUBLANES, WIDTH), F32),
                        pltpu.VMEM((TM, D_MODEL), F32),
                        pltpu.VMEM((DEC_NB, WIDTH), F32), pltpu.VMEM((DEC_NB, WIDTH), F32),
                        pltpu.SemaphoreType.DMA((N_STAGE,)),
                        pltpu.SemaphoreType.DMA((2 + HIST * N_SLABS,)),
                        pltpu.SemaphoreType.DMA((4,))],
        compiler_params=pltpu.CompilerParams(dimension_semantics=("arbitrary",),
                                             vmem_limit_bytes=VMEM_LIMIT_BYTES),
        name="hybrid_layer",
    )(x_rows, x_rows, xs_rows, cs_steps, h_state, w_in, w_out, *params)
    conv_s = xrs_rows.reshape(dec_b, dec_t, WIDTH)[:, dec_t - HIST:, :]
    return (y_rows.reshape(x_prompt.shape), ys_rows.reshape(x_sample.shape), conv_p, h_p.reshape(n_seq, WIDTH),
            conv_s, h_s, vs_rows.reshape(dec_b, dec_t, WIDTH))


def kernel(x_prompt, x_sample, state_rglru_conv, state_rglru_h, pre_norm_g, post_norm_g, w_in, conv_w, conv_b,
           w_rgate, b_rgate, w_igate, b_igate, lru_lambda, sgu_norm_g, w_spatial, b_spatial, w_out):
    depth = w_in.shape[0]
    yp, ys = x_prompt, x_sample
    conv_p, h_p, conv_s, h_s, v_s = [], [], [], [], []
    for l in range(depth):
        yp, ys, cp, hp, cs, hs, vs = _layer(
            yp, ys, state_rglru_conv[l], state_rglru_h[l], pre_norm_g[l], post_norm_g[l], w_in[l], conv_w[l],
            conv_b[l], w_rgate[l], b_rgate[l], w_igate[l], b_igate[l], lru_lambda[l], sgu_norm_g[l], w_spatial[l],
            b_spatial[l], w_out[l])
        conv_p.append(cp)
        h_p.append(hp)
        conv_s.append(cs)
        h_s.append(hs)
        v_s.append(vs)
    return (yp, ys, jnp.stack(conv_p), jnp.stack(h_p), jnp.stack(conv_s), jnp.stack(h_s), jnp.stack(v_s))
```

```python
import collections
import functools
import math

import jax
import jax.numpy as jnp
from jax import lax
from jax.experimental import pallas as pl
from jax.experimental.pallas import tpu as pltpu

F32 = jnp.float32
BF16 = jnp.bfloat16

EPS = 1e-6
LRU_C = 8.0
CONV_WIDTH = 4
HIST = CONV_WIDTH - 1
N_HEADS = 8
HEAD_DIM = 128
CHUNK = 128
LANES = 128
SUBLANES = 8
N_SLABS = 8
MXU_DIM = 256

D_MODEL = 2048
WIDTH = 1024
TM = 256
XR, GR, U, V, GS = range(5)
N_PRE = 4
N_STAGE = N_PRE + 2 * (D_MODEL // WIDTH)
VMEM_LIMIT_BYTES = 62 * 1024 * 1024

Geometry = collections.namedtuple("Geometry", "n_seg seg_len seg_pitch step_pitch")
PROMPT = Geometry(n_seg=SUBLANES, seg_len=TM // SUBLANES, seg_pitch=TM // SUBLANES + SUBLANES, step_pitch=SUBLANES)
DEC_NB = 32


def _decode_geometry(n_steps):
    return Geometry(n_seg=TM // n_steps, seg_len=n_steps, seg_pitch=n_steps, step_pitch=TM // n_steps + SUBLANES)


def _sigmoid(x):
    return 0.5 * jnp.tanh(0.5 * x) + 0.5


def _silu(x):
    return x * _sigmoid(x)


def _gelu(x):
    c = math.sqrt(2.0 / math.pi)
    return x * (0.5 * (1.0 + jnp.tanh(c * (x + 0.044715 * (x * x * x)))))


def _rms_norm(x, g):
    return x * lax.rsqrt(jnp.mean(x * x, axis=-1, keepdims=True) + EPS) * g


def _pre_norm(x_ref, g_ref, z_scr):
    x = x_ref[...]
    inv = lax.rsqrt(jnp.mean(x * x, axis=-1, keepdims=True) + EPS)
    for kb in range(D_MODEL // MXU_DIM):
        cs = slice(kb * MXU_DIM, (kb + 1) * MXU_DIM)
        z_scr[:, cs] = (x[:, cs] * inv * g_ref[:, cs]).astype(BF16)


def _to_step_order(xr, xr_scr, xp_scr, geo):
    R, L, P = geo.n_seg, geo.seg_len, geo.seg_pitch
    if R == SUBLANES:
        for s in range(R):
            for m in range(L // SUBLANES):
                for c in range(N_SLABS):
                    xp_scr[c, pl.ds(HIST * R + m * SUBLANES * R + s, SUBLANES, stride=R), :] = (
                        xr[s * L + m * SUBLANES:s * L + (m + 1) * SUBLANES, c * LANES:(c + 1) * LANES])
        return
    for s in range(R):
        for c in range(N_SLABS):
            xr_scr[c, s * P:s * P + L, :] = xr[s * L:(s + 1) * L, c * LANES:(c + 1) * LANES]
    for k in range(L):
        for g in range(R // SUBLANES):
            row = HIST * R + k * R + g * SUBLANES
            for c in range(N_SLABS):
                xp_scr[c, row:row + SUBLANES, :] = xr_scr[c, pl.ds(g * SUBLANES * P + k, SUBLANES, stride=P), :]


def _to_natural_order(hp_scr, dst_scr, geo):
    R, L, P = geo.n_seg, geo.seg_len, geo.step_pitch
    for s in range(R):
        for m in range(L // SUBLANES):
            row = s * L + m * SUBLANES
            for c in range(N_SLABS):
                dst_scr[row:row + SUBLANES, c * LANES:(c + 1) * LANES] = (
                    hp_scr[c, pl.ds(m * SUBLANES * P + s, SUBLANES, stride=P), :])


def _lru_coeffs(xp_scr, a_scr, b_scr, gate_scr, conv_w, conv_b, wg, b_r, b_i, lam, rows_per_step, reset_rows,
                before_head=None):
    R = rows_per_step
    for h in range(N_HEADS):
        ls = slice(h * HEAD_DIM, (h + 1) * HEAD_DIM)
        xc = conv_b[:, ls] + xp_scr[h, 0:TM, :] * conv_w[0:1, ls]
        for k in range(1, CONV_WIDTH):
            xc = xc + xp_scr[h, k * R:k * R + TM, :] * conv_w[k:k + 1, ls]
        b_scr[:, ls] = xc
    for h in range(N_HEADS):
        ls = slice(h * HEAD_DIM, (h + 1) * HEAD_DIM)
        gate_scr[h // 4][:, (h % 4) * 2 * HEAD_DIM:(h % 4 + 1) * 2 * HEAD_DIM] = jnp.dot(
            b_scr[:, ls].astype(BF16), wg[h], preferred_element_type=F32)
    for h in range(N_HEADS):
        if before_head is not None:
            before_head(h)
        ls = slice(h * HEAD_DIM, (h + 1) * HEAD_DIM)
        xc = b_scr[:, ls]
        g = gate_scr[h // 4][:, (h % 4) * 2 * HEAD_DIM:(h % 4 + 1) * 2 * HEAD_DIM]
        r = _sigmoid(g[:, :HEAD_DIM] + b_r[:, ls])
        i = _sigmoid(g[:, HEAD_DIM:] + b_i[:, ls])
        lam_h = lam[:, ls]
        softplus_neg = jnp.maximum(-lam_h, 0.0) + jnp.log1p(jnp.exp(-jnp.abs(lam_h)))
        log_a = r * (-LRU_C * softplus_neg)
        a = jnp.exp(log_a)
        mult = jnp.sqrt(-jnp.tanh(log_a) * (a * a + 1.0))
        ix = i * xc
        b = mult * ix
        a_scr[:, ls] = a
        b_scr[:, ls] = b
        if reset_rows is not None:
            a_scr[0:SUBLANES, ls] = jnp.where(reset_rows, 0.0, a[0:SUBLANES])
            b_scr[0:SUBLANES, ls] = jnp.where(reset_rows, ix[0:SUBLANES], b[0:SUBLANES])


def _sgu_branch(pre_scr, w_mix, bias_rows, cat, period=CHUNK, before_head=None):
    row = lax.broadcasted_iota(jnp.int32, (CHUNK, CHUNK), 0)
    col = lax.broadcasted_iota(jnp.int32, (CHUNK, CHUNK), 1)
    keep = row >= col
    if period < CHUNK:
        shift = period.bit_length() - 1
        keep = keep & ((row >> shift) == (col >> shift))
    for hd in range(N_HEADS):
        if before_head is not None:
            before_head(hd)
        ls = slice(hd * HEAD_DIM, (hd + 1) * HEAD_DIM)
        w_h = jnp.where(keep, jnp.tile(w_mix[hd], (CHUNK // period, 1)), 0.0).astype(BF16)
        bias = jnp.tile(bias_rows[0:period, ls], (CHUNK // period, 1))
        vs = slice(WIDTH + hd * HEAD_DIM, WIDTH + (hd + 1) * HEAD_DIM)
        blocks = [slice(ch * CHUNK, (ch + 1) * CHUNK) for ch in range(TM // CHUNK)]
        mixed = jnp.dot(w_h, jnp.concatenate([cat[rs, vs] for rs in blocks], axis=1), preferred_element_type=F32)
        for ch, rs in enumerate(blocks):
            s = mixed[:, ch * HEAD_DIM:(ch + 1) * HEAD_DIM] + bias
            cat[rs, vs] = (_gelu(pre_scr[U - 1, rs, ls]) * s * _silu(pre_scr[GS - 1, rs, ls])).astype(BF16)


def _load_weight(w_hbm, w_scr, stages, sem):
    n_stage = len(stages)
    n_row = w_hbm.shape[0] // TM
    n_chunks = w_scr.shape[0] * n_row
    assert n_chunks % n_stage == 0

    def aligned(v, m):
        return v if isinstance(v, int) else pl.multiple_of(v, m)

    def copy(k, slot):
        rows = pl.ds(aligned((k % n_row) * TM, TM), TM)
        cols = pl.ds(aligned((k // n_row) * WIDTH, WIDTH), WIDTH)
        return pltpu.make_async_copy(w_hbm.at[rows, cols], stages[slot], sem.at[slot])

    for k in range(n_stage):
        copy(k, k).start()

    def body(r, carry):
        for slot in range(n_stage):
            k = r * n_stage + slot
            copy(k, slot).wait()
            w_scr[k // n_row, pl.ds(aligned((k % n_row) * TM, TM), TM), :] = stages[slot][...].astype(BF16)

            @pl.when(k + n_stage < n_chunks)
            def _():
                copy(k + n_stage, slot).start()
        return carry

    lax.fori_loop(0, n_chunks // n_stage, body, 0)


def _layer_kernel(tiles_per_seq, n_tiles, n_dec, dec_t,
                  x_ref, xprev_ref, xs_hbm, cs_hbm, hs_hbm, w_in_hbm, w_out_hbm,
                  pre_g, post_g, conv_w, conv_b, wg, b_r, b_i, lam, sgu_g, w_s, bs_rows, ws_dec,
                  y_ref, conv_out, h_out, ys_hbm, convs_hbm, hs_out_hbm, vs_hbm,
                  w_in, w_out, z_scr, xr_scr, xp_scr, a_scr, b_scr, pre_scr, out_scr, cat_scr,
                  carry_h, carry_x, xs_buf, h0_buf, hn_buf, w_sem, in_sem, out_sem):
    hp_scr = xr_scr
    vs_buf = b_scr
    i = pl.program_id(0)
    t = i % tiles_per_seq
    cat_prev = cat_scr.at[(i + 1) % 2]
    cat_cur = cat_scr.at[i % 2]

    sq_sums = []

    def out_piece(c):
        piece = jnp.dot(cat_prev[...], w_out[c], preferred_element_type=F32)
        out_scr[:, c * WIDTH:(c + 1) * WIDTH] = piece
        sq_sums.append(jnp.sum(piece * piece, axis=-1, keepdims=True))

    def in_piece(slab):
        pre_scr[slab - 1] = jnp.dot(z_scr[...], w_in[slab], preferred_element_type=F32)

    def finish_prev():
        inv = lax.rsqrt(functools.reduce(lambda p, q: p + q, sq_sums) * (1.0 / D_MODEL) + EPS)
        sq_sums.clear()
        y_ref[...] = xprev_ref[...] + out_scr[...] * inv * post_g[...]

    @pl.when(i == 0)
    def _():
        halves = lambda ref: [ref.at[:, pl.ds(c * WIDTH, WIDTH)] for c in range(D_MODEL // WIDTH)]
        stages = [pre_scr.at[s] for s in range(pre_scr.shape[0])] + halves(out_scr) + halves(xs_buf)
        _load_weight(w_in_hbm, w_in, stages, w_sem)
        _load_weight(w_out_hbm, w_out, stages, w_sem)
        cat_scr[1] = jnp.zeros(cat_scr.shape[1:], BF16)

    @pl.when(t == 0)
    def _():
        carry_h[...] = jnp.zeros_like(carry_h)
        carry_x[...] = jnp.zeros_like(carry_x)

    @pl.when(i == n_tiles)
    def _():
        x_in, h_in, cs_in = _decode_copies_in(0, _decode_geometry(dec_t), xs_hbm, cs_hbm, hs_hbm, xs_buf, xp_scr, h0_buf,
                                              in_sem)
        for cp in [x_in, h_in] + cs_in:
            cp.start()
        for c in range(D_MODEL // WIDTH):
            out_piece(c)
        finish_prev()

    @pl.when(i > n_tiles)
    def _():
        _decode_tile(i - n_tiles - 1, n_dec, _decode_geometry(dec_t), xs_hbm, cs_hbm, hs_hbm, pre_g, post_g, conv_w, conv_b,
                     wg, b_r, b_i, lam, sgu_g, ws_dec, bs_rows, ys_hbm, convs_hbm, hs_out_hbm, vs_hbm, w_in, w_out, z_scr,
                     xr_scr, xp_scr, a_scr, b_scr, hp_scr, pre_scr, out_scr, cat_scr.at[0], xs_buf, vs_buf,
                     h0_buf, hn_buf, in_sem, out_sem)

    @pl.when(i < n_tiles)
    def _():
        geo = PROMPT
        R = geo.n_seg
        _pre_norm(x_ref, pre_g, z_scr)

        xr = jnp.dot(z_scr[...], w_in[XR], preferred_element_type=F32)
        seq = i // tiles_per_seq
        for m in range(HIST):
            conv_out[m, pl.ds(seq, 1), :] = xr[TM - HIST + m:TM - HIST + m + 1, :]
        _to_step_order(xr, xr_scr, xp_scr, geo)
        sub = lax.broadcasted_iota(jnp.int32, (R, WIDTH), 0)
        for m in range(HIST):
            src = HIST * R + (geo.seg_len - HIST + m) * R
            rs = slice(m * R, (m + 1) * R)
            for c in range(N_SLABS):
                ls = slice(c * LANES, (c + 1) * LANES)
                rolled = pltpu.roll(xp_scr[c, src:src + R, :], 1, 0)
                xp_scr[c, rs, :] = jnp.where(sub[:, 0:LANES] == 0, carry_x[rs, ls], rolled)
                carry_x[rs, ls] = rolled

        pieces = {0: GR, 4: V}
        reset_rows = (lax.broadcasted_iota(jnp.int32, (SUBLANES, HEAD_DIM), 0) == 0) & (t == 0)
        _lru_coeffs(xp_scr, a_scr, b_scr, (pre_scr.at[U - 1], pre_scr.at[GS - 1]), conv_w, conv_b, wg, b_r, b_i, lam,
                    R, reset_rows,
                    before_head=lambda h: in_piece(pieces[h]) if h in pieces else None)

        in_piece(U)
        hl =jnp.zeros((R, WIDTH), F32)
        pr = jnp.ones((R, WIDTH), F32)
        for k in range(geo.seg_len):
            a_k = a_scr[k * R:(k + 1) * R, :]
            hl = a_k * hl + b_scr[k * R:(k + 1) * R, :]
            pr = a_k * pr
        c_in = carry_h[0:1, :]
        h0 = jnp.zeros((R, WIDTH), F32)
        for j in range(R):
            h0 = jnp.where(sub == j, c_in, h0)
            c_in = hl[j:j + 1, :] + pr[j:j + 1, :] * c_in
        carry_h[0:1, :] = c_in
        h_out[pl.ds(seq, 1), :] = c_in
        h = h0
        for k in range(geo.seg_len):
            h = a_scr[k * R:(k + 1) * R, :] * h + b_scr[k * R:(k + 1) * R, :]
            for c in range(N_SLABS):
                hp_scr[c, pl.ds(k, R, stride=geo.seg_pitch), :] = h[:, c * LANES:(c + 1) * LANES]
        for j in range(R):
            rs = slice(j * geo.seg_len, (j + 1) * geo.seg_len)
            for c in range(N_SLABS):
                ls = slice(c * LANES, (c + 1) * LANES)
                h_nat = hp_scr[c, j * geo.seg_pitch:j * geo.seg_pitch + geo.seg_len, :]
                cat_cur[rs, ls] = (h_nat * _silu(pre_scr[GR - 1, rs, ls])).astype(BF16)

        in_piece(GS)
        cat_cur[:, WIDTH:2 * WIDTH] = _rms_norm(_gelu(pre_scr[V - 1]), sgu_g[...]).astype(BF16)
        between = {0: lambda: out_piece(0), 4: lambda: out_piece(1)}
        _sgu_branch(pre_scr, w_s, bs_rows, cat_cur,
                    before_head=lambda hd: between[hd]() if hd in between else None)
        finish_prev()


def _decode_copies_in(d, geo, xs_hbm, cs_hbm, hs_hbm, xs_buf, xp_scr, h0_buf, in_sem):
    R = geo.n_seg
    seqs = pl.ds(pl.multiple_of(d * R, R), R)
    x_in = pltpu.make_async_copy(xs_hbm.at[pl.ds(pl.multiple_of(d * TM, TM), TM), :], xs_buf, in_sem.at[0])
    h_in = pltpu.make_async_copy(hs_hbm.at[seqs, :], h0_buf, in_sem.at[1])
    cs_in = [pltpu.make_async_copy(cs_hbm.at[m, seqs, pl.ds(c * LANES, LANES)], xp_scr.at[c, pl.ds(m * R, R), :],
                                   in_sem.at[2 + m * N_SLABS + c])
             for m in range(HIST) for c in range(N_SLABS)]
    return x_in, h_in, cs_in


def _decode_tile(d, n_dec, geo, xs_hbm, cs_hbm, hs_hbm, pre_g, post_g, conv_w, conv_b, wg, b_r, b_i, lam, sgu_g,
                 w_mix, bias_rows, ys_hbm, convs_hbm, hs_out_hbm, vs_hbm, w_in, w_out, z_scr, xr_scr, xp_scr, a_scr, b_scr,
                 hp_scr, pre_scr, out_scr, cat, xs_buf, vs_buf, h0_buf, hn_buf, in_sem, out_sem):
    R, L = geo.n_seg, geo.seg_len
    tile_rows = lambda dd: pl.ds(pl.multiple_of(dd * TM, TM), TM)
    tile_seqs = lambda dd: pl.ds(pl.multiple_of(dd * R, R), R)

    def copies_out(dd):
        main = [pltpu.make_async_copy(out_scr, ys_hbm.at[tile_rows(dd), :], out_sem.at[0]),
                pltpu.make_async_copy(vs_buf, vs_hbm.at[tile_rows(dd), :], out_sem.at[1]),
                pltpu.make_async_copy(hn_buf, hs_out_hbm.at[tile_seqs(dd), :], out_sem.at[2])]
        conv = [pltpu.make_async_copy(xp_scr.at[c, pl.ds((L + m) * R, R), :],
                                      convs_hbm.at[m, tile_seqs(dd), pl.ds(c * LANES, LANES)],
                                      out_sem.at[3 + m * N_SLABS + c])
                for m in range(HIST) for c in range(N_SLABS)]
        return main + conv

    x_in, h_in, cs_in = _decode_copies_in(d, geo, xs_hbm, cs_hbm, hs_hbm, xs_buf, xp_scr, h0_buf, in_sem)
    x_next, h_next, cs_next = _decode_copies_in(d + 1, geo, xs_hbm, cs_hbm, hs_hbm, xs_buf, xp_scr, h0_buf, in_sem)
    has_next = d + 1 < n_dec
    for cp in [x_in, h_in] + cs_in:
        cp.wait()

    @pl.when(d > 0)
    def _():
        for cp in copies_out(d - 1):
            cp.wait()

    _pre_norm(xs_buf, pre_g, z_scr)

    xr = jnp.dot(z_scr[...], w_in[XR], preferred_element_type=F32)
    _to_step_order(xr, xr_scr, xp_scr, geo)
    _lru_coeffs(xp_scr, a_scr, b_scr, (pre_scr.at[U - 1], pre_scr.at[GS - 1]), conv_w, conv_b, wg, b_r, b_i, lam, R,
                None)

    @pl.when(has_next)
    def _():
        for cp in cs_next:
            cp.start()

    for slab in (GR, U, V, GS):
        pre_scr[slab - 1] = jnp.dot(z_scr[...], w_in[slab], preferred_element_type=F32)
    for c in range(N_SLABS):
        ls = slice(c * LANES, (c + 1) * LANES)
        h = h0_buf[:, ls]
        for s in range(L):
            h = a_scr[s * R:(s + 1) * R, ls] * h + b_scr[s * R:(s + 1) * R, ls]
            hp_scr[c, s * geo.step_pitch:s * geo.step_pitch + R, :] = h
        hn_buf[:, ls] = h

    @pl.when(has_next)
    def _():
        h_next.start()

    _to_natural_order(hp_scr, a_scr, geo)
    cat[:, 0:WIDTH] = (a_scr[...] * _silu(pre_scr[GR - 1])).astype(BF16)

    vs_buf[...] = _rms_norm(_gelu(pre_scr[V - 1]), sgu_g[...])
    cat[:, WIDTH:2 * WIDTH] = vs_buf[...].astype(BF16)
    _sgu_branch(pre_scr, w_mix, bias_rows, cat, period=L)

    for c in range(D_MODEL // WIDTH):
        out_scr[:, c * WIDTH:(c + 1) * WIDTH] = jnp.dot(cat[...], w_out[c], preferred_element_type=F32)
    out_scr[...] = xs_buf[...] + _rms_norm(out_scr[...], post_g[...])

    for cp in copies_out(d):
        cp.start()

    @pl.when(has_next)
    def _():
        x_next.start()

    @pl.when(d == n_dec - 1)
    def _():
        for cp in copies_out(d):
            cp.wait()


def _full(shape):
    return pl.BlockSpec(shape, lambda *_: (0,) * len(shape))


def _layer(x_prompt, x_sample, conv_state, h_state, pre_g, post_g, w_in, conv_w, conv_b, w_r, b_r, w_i, b_i, lam,
           sgu_g, w_s, b_s, w_out):
    n_seq, seq_len, d_model = x_prompt.shape
    dec_b, dec_t, _ = x_sample.shape
    assert d_model == D_MODEL and conv_w.shape == (CONV_WIDTH, WIDTH) and w_in.shape == (D_MODEL, 5 * WIDTH)
    assert seq_len % TM == 0 and dec_t * DEC_NB == TM and dec_b % DEC_NB == 0
    assert dec_t % SUBLANES == 0 and CHUNK % dec_t == 0 and HIST <= dec_t and dec_t & (dec_t - 1) == 0

    wg = jnp.concatenate([w_r, w_i], axis=-1).astype(BF16)
    row = lambda p: p.reshape(1, -1)
    bs_prompt = jnp.repeat(b_s.T, HEAD_DIM, axis=1)
    ws_dec = jnp.tile(w_s[:, :dec_t, :dec_t], (1, 1, CHUNK // dec_t))
    params = (row(pre_g), row(post_g), conv_w, row(conv_b), wg, row(b_r), row(b_i), row(lam), row(sgu_g), w_s,
              bs_prompt, ws_dec)

    tiles = seq_len // TM
    n_tiles = n_seq * tiles
    n_dec = dec_b // DEC_NB
    x_rows = x_prompt.reshape(n_seq * seq_len, d_model)
    xs_rows = x_sample.reshape(dec_b * dec_t, d_model)
    cs_steps = jnp.transpose(conv_state, (1, 0, 2))
    tile_of = lambda i: jnp.minimum(i, n_tiles - 1)
    prev_of = lambda i: jnp.clip(i - 1, 0, n_tiles - 1)
    any_spec = pl.BlockSpec(memory_space=pl.ANY)
    dec_geo = _decode_geometry(dec_t)
    relayout_rows = max(g.n_seg * g.seg_pitch for g in (PROMPT, dec_geo))
    assert relayout_rows >= max(g.seg_len * g.step_pitch for g in (PROMPT, dec_geo))
    y_rows, conv_p, h_p, ys_rows, conv_s, h_s, vs_rows = pl.pallas_call(
        functools.partial(_layer_kernel, tiles, n_tiles, n_dec, dec_t),
        grid=(n_tiles + 1 + n_dec,),
        in_specs=[pl.BlockSpec((TM, d_model), lambda i: (tile_of(i), 0)),
                  pl.BlockSpec((TM, d_model), lambda i: (prev_of(i), 0)),
                  any_spec, any_spec, any_spec, any_spec, any_spec] + [_full(p.shape) for p in params],
        out_specs=[pl.BlockSpec((TM, d_model), lambda i: (prev_of(i), 0)),
                   _full((HIST, n_seq, WIDTH)), _full((n_seq, WIDTH)),
                   any_spec, any_spec, any_spec, any_spec],
        out_shape=[jax.ShapeDtypeStruct(x_rows.shape, F32),
                   jax.ShapeDtypeStruct((HIST, n_seq, WIDTH), F32),
                   jax.ShapeDtypeStruct((n_seq, WIDTH), F32),
                   jax.ShapeDtypeStruct(xs_rows.shape, F32),
                   jax.ShapeDtypeStruct(cs_steps.shape, F32),
                   jax.ShapeDtypeStruct(h_state.shape, F32),
                   jax.ShapeDtypeStruct((dec_b * dec_t, WIDTH), F32)],
        scratch_shapes=[pltpu.VMEM((5, D_MODEL, WIDTH), BF16),
                        pltpu.VMEM((D_MODEL // WIDTH, D_MODEL, WIDTH), BF16),
                        pltpu.VMEM((TM, D_MODEL), BF16),
                        pltpu.VMEM((N_SLABS, relayout_rows, LANES), F32),
                        pltpu.VMEM((N_SLABS, TM + HIST * DEC_NB, LANES), F32),
                        pltpu.VMEM((TM, WIDTH), F32), pltpu.VMEM((TM, WIDTH), F32),
                        pltpu.VMEM((N_PRE, TM, WIDTH), F32),
                        pltpu.VMEM((TM, D_MODEL), F32),
                        pltpu.VMEM((2, TM, D_MODEL), BF16),
                        pltpu.VMEM((SUBLANES, WIDTH), F32),
                        pltpu.VMEM((HIST * SUBLANES, WIDTH), F32),
                        pltpu.VMEM((TM, D_MODEL), F32),
                        pltpu.VMEM((DEC_NB, WIDTH), F32), pltpu.VMEM((DEC_NB, WIDTH), F32),
                        pltpu.SemaphoreType.DMA((N_STAGE,)),
                        pltpu.SemaphoreType.DMA((2 + HIST * N_SLABS,)),
                        pltpu.SemaphoreType.DMA((3 + HIST * N_SLABS,))],
        compiler_params=pltpu.CompilerParams(dimension_semantics=("arbitrary",),
                                             vmem_limit_bytes=VMEM_LIMIT_BYTES),
        name="hybrid_layer",
    )(x_rows, x_rows, xs_rows, cs_steps, h_state, w_in, w_out, *params)
    seq_major = lambda c: jnp.transpose(c, (1, 0, 2))
    return (y_rows.reshape(x_prompt.shape), ys_rows.reshape(x_sample.shape), seq_major(conv_p), h_p,
            seq_major(conv_s), h_s, vs_rows.reshape(dec_b, dec_t, WIDTH))


def kernel(x_prompt, x_sample, state_rglru_conv, state_rglru_h, pre_norm_g, post_norm_g, w_in, conv_w, conv_b,
           w_rgate, b_rgate, w_igate, b_igate, lru_lambda, sgu_norm_g, w_spatial, b_spatial, w_out):
    depth = w_in.shape[0]
    yp, ys = x_prompt, x_sample
    conv_p, h_p, conv_s, h_s, v_s = [], [], [], [], []
    for l in range(depth):
        yp, ys, cp, hp, cs, hs, vs = _layer(
            yp, ys, state_rglru_conv[l], state_rglru_h[l], pre_norm_g[l], post_norm_g[l], w_in[l], conv_w[l],
            conv_b[l], w_rgate[l], b_rgate[l], w_igate[l], b_igate[l], lru_lambda[l], sgu_norm_g[l], w_spatial[l],
            b_spatial[l], w_out[l])
        conv_p.append(cp)
        h_p.append(hp)
        conv_s.append(cs)
        h_s.append(hs)
        v_s.append(vs)
    return (yp, ys, jnp.stack(conv_p), jnp.stack(h_p), jnp.stack(conv_s), jnp.stack(h_s), jnp.stack(v_s))
```

```python
import collections
import functools
import math

import jax
import jax.numpy as jnp
from jax import lax
from jax.experimental import pallas as pl
from jax.experimental.pallas import tpu as pltpu

F32 = jnp.float32
BF16 = jnp.bfloat16

EPS = 1e-6
LRU_C = 8.0
CONV_WIDTH = 4
HIST = CONV_WIDTH - 1
N_HEADS = 8
HEAD_DIM = 128
CHUNK = 128
LANES = 128
SUBLANES = 8
N_SLABS = 8
MXU_DIM = 256

D_MODEL = 2048
WIDTH = 1024
TM = 256
XR, GR, U, V, GS = range(5)
N_PRE = 4
N_STAGE = N_PRE + 2 * (D_MODEL // WIDTH)
VMEM_LIMIT_BYTES = 62 * 1024 * 1024

Geometry = collections.namedtuple("Geometry", "n_seg seg_len seg_pitch step_pitch")
PROMPT = Geometry(n_seg=SUBLANES, seg_len=TM // SUBLANES, seg_pitch=TM // SUBLANES + SUBLANES, step_pitch=SUBLANES)
DEC_NB = 32


def _decode_geometry(n_steps):
    return Geometry(n_seg=TM // n_steps, seg_len=n_steps, seg_pitch=n_steps, step_pitch=TM // n_steps + SUBLANES)


def _sigmoid(x):
    return 0.5 * jnp.tanh(0.5 * x) + 0.5


def _silu(x):
    return x * _sigmoid(x)


def _gelu(x):
    c = math.sqrt(2.0 / math.pi)
    return x * (0.5 * (1.0 + jnp.tanh(c * (x + 0.044715 * (x * x * x)))))


def _rms_norm(x, g):
    return x * lax.rsqrt(jnp.mean(x * x, axis=-1, keepdims=True) + EPS) * g


def _pre_norm(x_ref, g_ref, z_scr):
    x = x_ref[...]
    inv = lax.rsqrt(jnp.mean(x * x, axis=-1, keepdims=True) + EPS)
    for kb in range(D_MODEL // MXU_DIM):
        cs = slice(kb * MXU_DIM, (kb + 1) * MXU_DIM)
        z_scr[:, cs] = (x[:, cs] * inv * g_ref[:, cs]).astype(BF16)


def _to_step_order(xr, xr_scr, xp_scr, geo):
    R, L, P = geo.n_seg, geo.seg_len, geo.seg_pitch
    if R == SUBLANES:
        for s in range(R):
            for m in range(L // SUBLANES):
                for c in range(N_SLABS):
                    xp_scr[c, pl.ds(HIST * R + m * SUBLANES * R + s, SUBLANES, stride=R), :] = (
                        xr[s * L + m * SUBLANES:s * L + (m + 1) * SUBLANES, c * LANES:(c + 1) * LANES])
        return
    for s in range(R):
        for c in range(N_SLABS):
            xr_scr[c, s * P:s * P + L, :] = xr[s * L:(s + 1) * L, c * LANES:(c + 1) * LANES]
    for k in range(L):
        for g in range(R // SUBLANES):
            row = HIST * R + k * R + g * SUBLANES
            for c in range(N_SLABS):
                xp_scr[c, row:row + SUBLANES, :] = xr_scr[c, pl.ds(g * SUBLANES * P + k, SUBLANES, stride=P), :]


def _to_natural_order(hp_scr, dst_scr, geo):
    R, L, P = geo.n_seg, geo.seg_len, geo.step_pitch
    for s in range(R):
        for m in range(L // SUBLANES):
            row = s * L + m * SUBLANES
            for c in range(N_SLABS):
                dst_scr[row:row + SUBLANES, c * LANES:(c + 1) * LANES] = (
                    hp_scr[c, pl.ds(m * SUBLANES * P + s, SUBLANES, stride=P), :])


def _lru_coeffs(xp_scr, a_scr, b_scr, gate_scr, conv_w, conv_b, wg, b_r, b_i, lam, rows_per_step, reset_rows):
    R = rows_per_step
    for h in range(N_HEADS):
        ls = slice(h * HEAD_DIM, (h + 1) * HEAD_DIM)
        xc = conv_b[:, ls] + xp_scr[h, 0:TM, :] * conv_w[0:1, ls]
        for k in range(1, CONV_WIDTH):
            xc = xc + xp_scr[h, k * R:k * R + TM, :] * conv_w[k:k + 1, ls]
        b_scr[:, ls] = xc
    for h in range(N_HEADS):
        ls = slice(h * HEAD_DIM, (h + 1) * HEAD_DIM)
        gate_scr[h // 4][:, (h % 4) * 2 * HEAD_DIM:(h % 4 + 1) * 2 * HEAD_DIM] = jnp.dot(
            b_scr[:, ls].astype(BF16), wg[h], preferred_element_type=F32)
    for h in range(N_HEADS):
        ls = slice(h * HEAD_DIM, (h + 1) * HEAD_DIM)
        xc = b_scr[:, ls]
        g = gate_scr[h // 4][:, (h % 4) * 2 * HEAD_DIM:(h % 4 + 1) * 2 * HEAD_DIM]
        r = _sigmoid(g[:, :HEAD_DIM] + b_r[:, ls])
        i = _sigmoid(g[:, HEAD_DIM:] + b_i[:, ls])
        lam_h = lam[:, ls]
        softplus_neg = jnp.maximum(-lam_h, 0.0) + jnp.log1p(jnp.exp(-jnp.abs(lam_h)))
        log_a = r * (-LRU_C * softplus_neg)
        a = jnp.exp(log_a)
        mult = jnp.sqrt(-jnp.tanh(log_a) * (a * a + 1.0))
        ix = i * xc
        b = mult * ix
        a_scr[:, ls] = a
        b_scr[:, ls] = b
        if reset_rows is not None:
            a_scr[0:SUBLANES, ls] = jnp.where(reset_rows, 0.0, a[0:SUBLANES])
            b_scr[0:SUBLANES, ls] = jnp.where(reset_rows, ix[0:SUBLANES], b[0:SUBLANES])


def _sgu_head_fn(pre_scr, w_mix, bias_rows, cat, period=CHUNK):
    row = lax.broadcasted_iota(jnp.int32, (CHUNK, CHUNK), 0)
    col = lax.broadcasted_iota(jnp.int32, (CHUNK, CHUNK), 1)
    keep = row >= col
    if period < CHUNK:
        shift = period.bit_length() - 1
        keep = keep & ((row >> shift) == (col >> shift))
    def head(hd):
        ls = slice(hd * HEAD_DIM, (hd + 1) * HEAD_DIM)
        w_h = jnp.where(keep, jnp.tile(w_mix[hd], (CHUNK // period, 1)), 0.0).astype(BF16)
        bias = jnp.tile(bias_rows[0:period, ls], (CHUNK // period, 1))
        vs = slice(WIDTH + hd * HEAD_DIM, WIDTH + (hd + 1) * HEAD_DIM)
        blocks = [slice(ch * CHUNK, (ch + 1) * CHUNK) for ch in range(TM // CHUNK)]
        mixed = jnp.dot(w_h, jnp.concatenate([cat[rs, vs] for rs in blocks], axis=1), preferred_element_type=F32)
        for ch, rs in enumerate(blocks):
            s = mixed[:, ch * HEAD_DIM:(ch + 1) * HEAD_DIM] + bias
            cat[rs, vs] = (_gelu(pre_scr[U - 1, rs, ls]) * s * _silu(pre_scr[GS - 1, rs, ls])).astype(BF16)

    return head


def _load_weight(w_hbm, w_scr, stages, sem):
    n_stage = len(stages)
    n_row = w_hbm.shape[0] // TM
    n_chunks = w_scr.shape[0] * n_row
    assert n_chunks % n_stage == 0

    def aligned(v, m):
        return v if isinstance(v, int) else pl.multiple_of(v, m)

    def copy(k, slot):
        rows = pl.ds(aligned((k % n_row) * TM, TM), TM)
        cols = pl.ds(aligned((k // n_row) * WIDTH, WIDTH), WIDTH)
        return pltpu.make_async_copy(w_hbm.at[rows, cols], stages[slot], sem.at[slot])

    for k in range(n_stage):
        copy(k, k).start()

    def body(r, carry):
        for slot in range(n_stage):
            k = r * n_stage + slot
            copy(k, slot).wait()
            w_scr[k // n_row, pl.ds(aligned((k % n_row) * TM, TM), TM), :] = stages[slot][...].astype(BF16)

            @pl.when(k + n_stage < n_chunks)
            def _():
                copy(k + n_stage, slot).start()
        return carry

    lax.fori_loop(0, n_chunks // n_stage, body, 0)


def _layer_kernel(tiles_per_seq, n_tiles, n_dec, dec_t,
                  x_ref, xprev_ref, xs_hbm, cs_hbm, hs_hbm, w_in_hbm, w_out_hbm,
                  pre_g, post_g, conv_w, conv_b, wg, b_r, b_i, lam, sgu_g, w_s, bs_rows, ws_dec,
                  y_ref, conv_out, h_out, ys_hbm, convs_hbm, hs_out_hbm, vs_hbm,
                  w_in, w_out, z_scr, xr_scr, xp_scr, a_scr, b_scr, pre_scr, out_scr, cat_scr,
                  carry_h, carry_x, xr_tail, xs_buf, h0_buf, hn_buf, w_sem, in_sem, out_sem):
    hp_scr = xr_scr
    vs_buf = b_scr
    i = pl.program_id(0)
    t = i % tiles_per_seq
    cat_prev = cat_scr.at[(i + 1) % 2]
    cat_cur = cat_scr.at[i % 2]

    sq_sums = []

    def out_piece(c):
        piece = jnp.dot(cat_prev[...], w_out[c], preferred_element_type=F32)
        out_scr[:, c * WIDTH:(c + 1) * WIDTH] = piece
        sq_sums.append(jnp.sum(piece * piece, axis=-1, keepdims=True))

    def in_piece(slab):
        pre_scr[slab - 1] = jnp.dot(z_scr[...], w_in[slab], preferred_element_type=F32)

    def finish_prev():
        inv = lax.rsqrt(functools.reduce(lambda p, q: p + q, sq_sums) * (1.0 / D_MODEL) + EPS)
        sq_sums.clear()
        y_ref[...] = xprev_ref[...] + out_scr[...] * inv * post_g[...]

    @pl.when(i == 0)
    def _():
        halves = lambda ref: [ref.at[:, pl.ds(c * WIDTH, WIDTH)] for c in range(D_MODEL // WIDTH)]
        stages = [pre_scr.at[s] for s in range(pre_scr.shape[0])] + halves(out_scr) + halves(xs_buf)
        _load_weight(w_in_hbm, w_in, stages, w_sem)
        _load_weight(w_out_hbm, w_out, stages, w_sem)
        cat_scr[1] = jnp.zeros(cat_scr.shape[1:], BF16)

    @pl.when(t == 0)
    def _():
        carry_h[...] = jnp.zeros_like(carry_h)
        carry_x[...] = jnp.zeros_like(carry_x)

    @pl.when(i == n_tiles)
    def _():
        x_in, h_in, cs_in = _decode_copies_in(0, _decode_geometry(dec_t), xs_hbm, cs_hbm, hs_hbm, xs_buf, xp_scr, h0_buf,
                                              in_sem)
        for cp in [x_in, h_in] + cs_in:
            cp.start()
        for c in range(D_MODEL // WIDTH):
            out_piece(c)
        finish_prev()

    @pl.when(i > n_tiles)
    def _():
        _decode_tile(i - n_tiles - 1, n_dec, _decode_geometry(dec_t), xs_hbm, cs_hbm, hs_hbm, pre_g, post_g, conv_w, conv_b,
                     wg, b_r, b_i, lam, sgu_g, ws_dec, bs_rows, ys_hbm, convs_hbm, hs_out_hbm, vs_hbm, w_in, w_out, z_scr,
                     xr_scr, xp_scr, a_scr, b_scr, hp_scr, pre_scr, out_scr, cat_scr.at[0], xs_buf, vs_buf,
                     h0_buf, hn_buf, in_sem, out_sem)

    @pl.when(i < n_tiles)
    def _():
        geo = PROMPT
        R = geo.n_seg
        _pre_norm(x_ref, pre_g, z_scr)

        xr = jnp.dot(z_scr[...], w_in[XR], preferred_element_type=F32)
        xr_tail[...] = xr[TM - SUBLANES:TM, :]
        _to_step_order(xr, xr_scr, xp_scr, geo)
        sub = lax.broadcasted_iota(jnp.int32, (R, WIDTH), 0)
        for m in range(HIST):
            src = HIST * R + (geo.seg_len - HIST + m) * R
            rs = slice(m * R, (m + 1) * R)
            for c in range(N_SLABS):
                ls = slice(c * LANES, (c + 1) * LANES)
                rolled = pltpu.roll(xp_scr[c, src:src + R, :], 1, 0)
                xp_scr[c, rs, :] = jnp.where(sub[:, 0:LANES] == 0, carry_x[rs, ls], rolled)
                carry_x[rs, ls] = rolled

        reset_rows = (lax.broadcasted_iota(jnp.int32, (SUBLANES, HEAD_DIM), 0) == 0) & (t == 0)
        _lru_coeffs(xp_scr, a_scr, b_scr, (pre_scr.at[U - 1], pre_scr.at[GS - 1]), conv_w, conv_b, wg, b_r, b_i, lam,
                    R, reset_rows)

        in_piece(GR)
        hl = jnp.zeros((R, WIDTH), F32)
        pr = jnp.ones((R, WIDTH), F32)
        for k in range(geo.seg_len):
            a_k = a_scr[k * R:(k + 1) * R, :]
            hl = a_k * hl + b_scr[k * R:(k + 1) * R, :]
            pr = a_k * pr
        c_in = carry_h[0:1, :]
        h0 = jnp.zeros((R, WIDTH), F32)
        for j in range(R):
            h0 = jnp.where(sub == j, c_in, h0)
            c_in = hl[j:j + 1, :] + pr[j:j + 1, :] * c_in
        carry_h[0:1, :] = c_in
        h = h0
        for k in range(geo.seg_len):
            h = a_scr[k * R:(k + 1) * R, :] * h + b_scr[k * R:(k + 1) * R, :]
            for c in range(N_SLABS):
                hp_scr[c, pl.ds(k, R, stride=geo.seg_pitch), :] = h[:, c * LANES:(c + 1) * LANES]
        in_piece(V)
        for j in range(R):
            rs = slice(j * geo.seg_len, (j + 1) * geo.seg_len)
            for c in range(N_SLABS):
                ls = slice(c * LANES, (c + 1) * LANES)
                h_nat = hp_scr[c, j * geo.seg_pitch:j * geo.seg_pitch + geo.seg_len, :]
                cat_cur[rs, ls] = (h_nat * _silu(pre_scr[GR - 1, rs, ls])).astype(BF16)

        in_piece(U)
        in_piece(GS)
        cat_cur[:, WIDTH:2 * WIDTH] = _rms_norm(_gelu(pre_scr[V - 1]), sgu_g[...]).astype(BF16)
        sgu_head = _sgu_head_fn(pre_scr, w_s, bs_rows, cat_cur)
        for hd in range(N_HEADS):
            if hd % (N_HEADS // (D_MODEL // WIDTH)) == 0:
                out_piece(hd // (N_HEADS // (D_MODEL // WIDTH)))
            sgu_head(hd)
        finish_prev()

    @pl.when((i < n_tiles) & (t == tiles_per_seq - 1))
    def _():
        seq = i // tiles_per_seq
        for m in range(HIST):
            conv_out[m, pl.ds(seq, 1), :] = xr_tail[SUBLANES - HIST + m:SUBLANES - HIST + m + 1, :]
        h_out[pl.ds(seq, 1), :] = carry_h[0:1, :]


def _decode_copies_in(d, geo, xs_hbm, cs_hbm, hs_hbm, xs_buf, xp_scr, h0_buf, in_sem):
    R = geo.n_seg
    seqs = pl.ds(pl.multiple_of(d * R, R), R)
    x_in = pltpu.make_async_copy(xs_hbm.at[pl.ds(pl.multiple_of(d * TM, TM), TM), :], xs_buf, in_sem.at[0])
    h_in = pltpu.make_async_copy(hs_hbm.at[seqs, :], h0_buf, in_sem.at[1])
    cs_in = [pltpu.make_async_copy(cs_hbm.at[m, seqs, pl.ds(c * LANES, LANES)], xp_scr.at[c, pl.ds(m * R, R), :],
                                   in_sem.at[2 + m * N_SLABS + c])
             for m in range(HIST) for c in range(N_SLABS)]
    return x_in, h_in, cs_in


def _decode_tile(d, n_dec, geo, xs_hbm, cs_hbm, hs_hbm, pre_g, post_g, conv_w, conv_b, wg, b_r, b_i, lam, sgu_g,
                 w_mix, bias_rows, ys_hbm, convs_hbm, hs_out_hbm, vs_hbm, w_in, w_out, z_scr, xr_scr, xp_scr, a_scr, b_scr,
                 hp_scr, pre_scr, out_scr, cat, xs_buf, vs_buf, h0_buf, hn_buf, in_sem, out_sem):
    R, L = geo.n_seg, geo.seg_len
    tile_rows = lambda dd: pl.ds(pl.multiple_of(dd * TM, TM), TM)
    tile_seqs = lambda dd: pl.ds(pl.multiple_of(dd * R, R), R)

    def copies_out(dd):
        main = [pltpu.make_async_copy(out_scr, ys_hbm.at[tile_rows(dd), :], out_sem.at[0]),
                pltpu.make_async_copy(vs_buf, vs_hbm.at[tile_rows(dd), :], out_sem.at[1]),
                pltpu.make_async_copy(hn_buf, hs_out_hbm.at[tile_seqs(dd), :], out_sem.at[2])]
        conv = [pltpu.make_async_copy(xp_scr.at[c, pl.ds((L + m) * R, R), :],
                                      convs_hbm.at[m, tile_seqs(dd), pl.ds(c * LANES, LANES)],
                                      out_sem.at[3 + m * N_SLABS + c])
                for m in range(HIST) for c in range(N_SLABS)]
        return main + conv

    x_in, h_in, cs_in = _decode_copies_in(d, geo, xs_hbm, cs_hbm, hs_hbm, xs_buf, xp_scr, h0_buf, in_sem)
    x_next, h_next, cs_next = _decode_copies_in(d + 1, geo, xs_hbm, cs_hbm, hs_hbm, xs_buf, xp_scr, h0_buf, in_sem)
    has_next = d + 1 < n_dec
    for cp in [x_in, h_in] + cs_in:
        cp.wait()

    @pl.when(d > 0)
    def _():
        for cp in copies_out(d - 1):
            cp.wait()

    _pre_norm(xs_buf, pre_g, z_scr)

    xr = jnp.dot(z_scr[...], w_in[XR], preferred_element_type=F32)
    _to_step_order(xr, xr_scr, xp_scr, geo)
    _lru_coeffs(xp_scr, a_scr, b_scr, (pre_scr.at[U - 1], pre_scr.at[GS - 1]), conv_w, conv_b, wg, b_r, b_i, lam, R,
                None)

    @pl.when(has_next)
    def _():
        for cp in cs_next:
            cp.start()

    for slab in (GR, U, V, GS):
        pre_scr[slab - 1] = jnp.dot(z_scr[...], w_in[slab], preferred_element_type=F32)
    for c in range(N_SLABS):
        ls = slice(c * LANES, (c + 1) * LANES)
        h = h0_buf[:, ls]
        for s in range(L):
            h = a_scr[s * R:(s + 1) * R, ls] * h + b_scr[s * R:(s + 1) * R, ls]
            hp_scr[c, s * geo.step_pitch:s * geo.step_pitch + R, :] = h
        hn_buf[:, ls] = h

    @pl.when(has_next)
    def _():
        h_next.start()

    _to_natural_order(hp_scr, a_scr, geo)
    cat[:, 0:WIDTH] = (a_scr[...] * _silu(pre_scr[GR - 1])).astype(BF16)

    vs_buf[...] = _rms_norm(_gelu(pre_scr[V - 1]), sgu_g[...])
    cat[:, WIDTH:2 * WIDTH] = vs_buf[...].astype(BF16)
    sgu_head = _sgu_head_fn(pre_scr, w_mix, bias_rows, cat, period=L)
    for hd in range(N_HEADS):
        sgu_head(hd)

    for c in range(D_MODEL // WIDTH):
        out_scr[:, c * WIDTH:(c + 1) * WIDTH] = jnp.dot(cat[...], w_out[c], preferred_element_type=F32)
    out_scr[...] = xs_buf[...] + _rms_norm(out_scr[...], post_g[...])

    for cp in copies_out(d):
        cp.start()

    @pl.when(has_next)
    def _():
        x_next.start()

    @pl.when(d == n_dec - 1)
    def _():
        for cp in copies_out(d):
            cp.wait()


def _full(shape):
    return pl.BlockSpec(shape, lambda *_: (0,) * len(shape))


def _layer(x_prompt, x_sample, conv_state, h_state, pre_g, post_g, w_in, conv_w, conv_b, w_r, b_r, w_i, b_i, lam,
           sgu_g, w_s, b_s, w_out):
    n_seq, seq_len, d_model = x_prompt.shape
    dec_b, dec_t, _ = x_sample.shape
    assert d_model == D_MODEL and conv_w.shape == (CONV_WIDTH, WIDTH) and w_in.shape == (D_MODEL, 5 * WIDTH)
    assert seq_len % TM == 0 and dec_t * DEC_NB == TM and dec_b % DEC_NB == 0
    assert dec_t % SUBLANES == 0 and CHUNK % dec_t == 0 and HIST <= dec_t and dec_t & (dec_t - 1) == 0

    wg = jnp.concatenate([w_r, w_i], axis=-1).astype(BF16)
    row = lambda p: p.reshape(1, -1)
    bs_prompt = jnp.repeat(b_s.T, HEAD_DIM, axis=1)
    ws_dec = jnp.tile(w_s[:, :dec_t, :dec_t], (1, 1, CHUNK // dec_t))
    params = (row(pre_g), row(post_g), conv_w, row(conv_b), wg, row(b_r), row(b_i), row(lam), row(sgu_g), w_s,
              bs_prompt, ws_dec)

    tiles = seq_len // TM
    n_tiles = n_seq * tiles
    n_dec = dec_b // DEC_NB
    x_rows = x_prompt.reshape(n_seq * seq_len, d_model)
    xs_rows = x_sample.reshape(dec_b * dec_t, d_model)
    cs_steps = jnp.transpose(conv_state, (1, 0, 2))
    tile_of = lambda i: jnp.minimum(i, n_tiles - 1)
    prev_of = lambda i: jnp.clip(i - 1, 0, n_tiles - 1)
    any_spec = pl.BlockSpec(memory_space=pl.ANY)
    dec_geo = _decode_geometry(dec_t)
    relayout_rows = max(g.n_seg * g.seg_pitch for g in (PROMPT, dec_geo))
    assert relayout_rows >= max(g.seg_len * g.step_pitch for g in (PROMPT, dec_geo))
    y_rows, conv_p, h_p, ys_rows, conv_s, h_s, vs_rows = pl.pallas_call(
        functools.partial(_layer_kernel, tiles, n_tiles, n_dec, dec_t),
        grid=(n_tiles + 1 + n_dec,),
        in_specs=[pl.BlockSpec((TM, d_model), lambda i: (tile_of(i), 0)),
                  pl.BlockSpec((TM, d_model), lambda i: (prev_of(i), 0)),
                  any_spec, any_spec, any_spec, any_spec, any_spec] + [_full(p.shape) for p in params],
        out_specs=[pl.BlockSpec((TM, d_model), lambda i: (prev_of(i), 0)),
                   _full((HIST, n_seq, WIDTH)), _full((n_seq, WIDTH)),
                   any_spec, any_spec, any_spec, any_spec],
        out_shape=[jax.ShapeDtypeStruct(x_rows.shape, F32),
                   jax.ShapeDtypeStruct((HIST, n_seq, WIDTH), F32),
                   jax.ShapeDtypeStruct((n_seq, WIDTH), F32),
                   jax.ShapeDtypeStruct(xs_rows.shape, F32),
                   jax.ShapeDtypeStruct(cs_steps.shape, F32),
                   jax.ShapeDtypeStruct(h_state.shape, F32),
                   jax.ShapeDtypeStruct((dec_b * dec_t, WIDTH), F32)],
        scratch_shapes=[pltpu.VMEM((5, D_MODEL, WIDTH), BF16),
                        pltpu.VMEM((D_MODEL // WIDTH, D_MODEL, WIDTH), BF16),
                        pltpu.VMEM((TM, D_MODEL), BF16),
                        pltpu.VMEM((N_SLABS, relayout_rows, LANES), F32),
                        pltpu.VMEM((N_SLABS, TM + HIST * DEC_NB, LANES), F32),
                        pltpu.VMEM((TM, WIDTH), F32), pltpu.VMEM((TM, WIDTH), F32),
                        pltpu.VMEM((N_PRE, TM, WIDTH), F32),
                        pltpu.VMEM((TM, D_MODEL), F32),
                        pltpu.VMEM((2, TM, D_MODEL), BF16),
                        pltpu.VMEM((SUBLANES, WIDTH), F32),
                        pltpu.VMEM((HIST * SUBLANES, WIDTH), F32),
                        pltpu.VMEM((SUBLANES, WIDTH), F32),
                        pltpu.VMEM((TM, D_MODEL), F32),
                        pltpu.VMEM((DEC_NB, WIDTH), F32), pltpu.VMEM((DEC_NB, WIDTH), F32),
                        pltpu.SemaphoreType.DMA((N_STAGE,)),
                        pltpu.SemaphoreType.DMA((2 + HIST * N_SLABS,)),
                        pltpu.SemaphoreType.DMA((3 + HIST * N_SLABS,))],
        compiler_params=pltpu.CompilerParams(dimension_semantics=("arbitrary",),
                                             vmem_limit_bytes=VMEM_LIMIT_BYTES),
        name="hybrid_layer",
    )(x_rows, x_rows, xs_rows, cs_steps, h_state, w_in, w_out, *params)
    seq_major = lambda c: jnp.transpose(c, (1, 0, 2))
    return (y_rows.reshape(x_prompt.shape), ys_rows.reshape(x_sample.shape), seq_major(conv_p), h_p,
            seq_major(conv_s), h_s, vs_rows.reshape(dec_b, dec_t, WIDTH))


def kernel(x_prompt, x_sample, state_rglru_conv, state_rglru_h, pre_norm_g, post_norm_g, w_in, conv_w, conv_b,
           w_rgate, b_rgate, w_igate, b_igate, lru_lambda, sgu_norm_g, w_spatial, b_spatial, w_out):
    depth = w_in.shape[0]
    yp, ys = x_prompt, x_sample
    conv_p, h_p, conv_s, h_s, v_s = [], [], [], [], []
    for l in range(depth):
        yp, ys, cp, hp, cs, hs, vs = _layer(
            yp, ys, state_rglru_conv[l], state_rglru_h[l], pre_norm_g[l], post_norm_g[l], w_in[l], conv_w[l],
            conv_b[l], w_rgate[l], b_rgate[l], w_igate[l], b_igate[l], lru_lambda[l], sgu_norm_g[l], w_spatial[l],
            b_spatial[l], w_out[l])
        conv_p.append(cp)
        h_p.append(hp)
        conv_s.append(cs)
        h_s.append(hs)
        v_s.append(vs)
    return (yp, ys, jnp.stack(conv_p), jnp.stack(h_p), jnp.stack(conv_s), jnp.stack(h_s), jnp.stack(v_s))
```

```python
import collections
import functools
import math

import jax
import jax.numpy as jnp
from jax import lax
from jax.experimental import pallas as pl
from jax.experimental.pallas import tpu as pltpu

F32 = jnp.float32
BF16 = jnp.bfloat16

EPS = 1e-6
LRU_C = 8.0
CONV_WIDTH = 4
HIST = CONV_WIDTH - 1
N_HEADS = 8
HEAD_DIM = 128
CHUNK = 128
LANES = 128
SUBLANES = 8
N_SLABS = 8
MXU_DIM = 256

D_MODEL = 2048
WIDTH = 1024
TM = 256
XR, GR, U, V, GS = range(5)
N_PRE = 4
N_STAGE = N_PRE + 2 * (D_MODEL // WIDTH)
VMEM_LIMIT_BYTES = 62 * 1024 * 1024

Geometry = collections.namedtuple("Geometry", "n_seg seg_len seg_pitch step_pitch")
PROMPT = Geometry(n_seg=SUBLANES, seg_len=TM // SUBLANES, seg_pitch=TM // SUBLANES + SUBLANES, step_pitch=SUBLANES)
DEC_NB = 32


def _decode_geometry(n_steps):
    return Geometry(n_seg=TM // n_steps, seg_len=n_steps, seg_pitch=n_steps, step_pitch=TM // n_steps + SUBLANES)


def _sigmoid(x):
    return 0.5 * jnp.tanh(0.5 * x) + 0.5


def _silu(x):
    return x * _sigmoid(x)


def _gelu(x):
    c = math.sqrt(2.0 / math.pi)
    return x * (0.5 * (1.0 + jnp.tanh(c * (x + 0.044715 * (x * x * x)))))


def _rms_norm(x, g):
    return x * lax.rsqrt(jnp.mean(x * x, axis=-1, keepdims=True) + EPS) * g


def _pre_norm(x_ref, g_ref, z_scr):
    x = x_ref[...]
    inv = lax.rsqrt(jnp.mean(x * x, axis=-1, keepdims=True) + EPS)
    for kb in range(D_MODEL // MXU_DIM):
        cs = slice(kb * MXU_DIM, (kb + 1) * MXU_DIM)
        z_scr[:, cs] = (x[:, cs] * inv * g_ref[:, cs]).astype(BF16)


def _to_step_order(xr, xr_scr, xp_scr, geo):
    R, L, P = geo.n_seg, geo.seg_len, geo.seg_pitch
    if R == SUBLANES:
        for s in range(R):
            for m in range(L // SUBLANES):
                for c in range(N_SLABS):
                    xp_scr[c, pl.ds(HIST * R + m * SUBLANES * R + s, SUBLANES, stride=R), :] = (
                        xr[s * L + m * SUBLANES:s * L + (m + 1) * SUBLANES, c * LANES:(c + 1) * LANES])
        return
    for s in range(R):
        for c in range(N_SLABS):
            xr_scr[c, s * P:s * P + L, :] = xr[s * L:(s + 1) * L, c * LANES:(c + 1) * LANES]
    for k in range(L):
        for g in range(R // SUBLANES):
            row = HIST * R + k * R + g * SUBLANES
            for c in range(N_SLABS):
                xp_scr[c, row:row + SUBLANES, :] = xr_scr[c, pl.ds(g * SUBLANES * P + k, SUBLANES, stride=P), :]


def _to_natural_order(hp_scr, dst_scr, geo):
    R, L, P = geo.n_seg, geo.seg_len, geo.step_pitch
    for s in range(R):
        for m in range(L // SUBLANES):
            row = s * L + m * SUBLANES
            for c in range(N_SLABS):
                dst_scr[row:row + SUBLANES, c * LANES:(c + 1) * LANES] = (
                    hp_scr[c, pl.ds(m * SUBLANES * P + s, SUBLANES, stride=P), :])


def _lru_coeffs(xp_scr, a_scr, b_scr, gate_scr, conv_w, conv_b, wg, b_r, b_i, lam, rows_per_step, reset_rows):
    R = rows_per_step
    for h in range(N_HEADS):
        ls = slice(h * HEAD_DIM, (h + 1) * HEAD_DIM)
        xc = conv_b[:, ls] + xp_scr[h, 0:TM, :] * conv_w[0:1, ls]
        for k in range(1, CONV_WIDTH):
            xc = xc + xp_scr[h, k * R:k * R + TM, :] * conv_w[k:k + 1, ls]
        b_scr[:, ls] = xc
    for h in range(N_HEADS):
        ls = slice(h * HEAD_DIM, (h + 1) * HEAD_DIM)
        gate_scr[h // 4][:, (h % 4) * 2 * HEAD_DIM:(h % 4 + 1) * 2 * HEAD_DIM] = jnp.dot(
            b_scr[:, ls].astype(BF16), wg[h], preferred_element_type=F32)
    for h in range(N_HEADS):
        ls = slice(h * HEAD_DIM, (h + 1) * HEAD_DIM)
        xc = b_scr[:, ls]
        g = gate_scr[h // 4][:, (h % 4) * 2 * HEAD_DIM:(h % 4 + 1) * 2 * HEAD_DIM]
        r = _sigmoid(g[:, :HEAD_DIM] + b_r[:, ls])
        i = _sigmoid(g[:, HEAD_DIM:] + b_i[:, ls])
        lam_h = lam[:, ls]
        softplus_neg = jnp.maximum(-lam_h, 0.0) + jnp.log1p(jnp.exp(-jnp.abs(lam_h)))
        log_a = r * (-LRU_C * softplus_neg)
        a = jnp.exp(log_a)
        mult = jnp.sqrt(-jnp.tanh(log_a) * (a * a + 1.0))
        ix = i * xc
        b = mult * ix
        a_scr[:, ls] = a
        b_scr[:, ls] = b
        if reset_rows is not None:
            a_scr[0:SUBLANES, ls] = jnp.where(reset_rows, 0.0, a[0:SUBLANES])
            b_scr[0:SUBLANES, ls] = jnp.where(reset_rows, ix[0:SUBLANES], b[0:SUBLANES])


def _sgu_head_fn(pre_scr, w_mix, bias_rows, cat, period=CHUNK):
    row = lax.broadcasted_iota(jnp.int32, (CHUNK, CHUNK), 0)
    col = lax.broadcasted_iota(jnp.int32, (CHUNK, CHUNK), 1)
    keep = row >= col
    if period < CHUNK:
        shift = period.bit_length() - 1
        keep = keep & ((row >> shift) == (col >> shift))
    blocks = [slice(ch * CHUNK, (ch + 1) * CHUNK) for ch in range(TM // CHUNK)]

    def head(hd):
        ls = slice(hd * HEAD_DIM, (hd + 1) * HEAD_DIM)
        vs = slice(WIDTH + hd * HEAD_DIM, WIDTH + (hd + 1) * HEAD_DIM)
        w_h = jnp.where(keep, jnp.tile(w_mix[hd], (CHUNK // period, 1)), 0.0).astype(BF16)
        bias = jnp.tile(bias_rows[0:period, ls], (CHUNK // period, 1))
        mixed = jnp.dot(w_h, jnp.concatenate([cat[rs, vs] for rs in blocks], axis=1), preferred_element_type=F32)
        for ch, rs in enumerate(blocks):
            s = mixed[:, ch * HEAD_DIM:(ch + 1) * HEAD_DIM] + bias
            cat[rs, vs] = (_gelu(pre_scr[U - 1, rs, ls]) * s * _silu(pre_scr[GS - 1, rs, ls])).astype(BF16)

    return head


def _load_weight(w_hbm, w_scr, stages, sem):
    n_stage = len(stages)
    n_row = w_hbm.shape[0] // TM
    n_chunks = w_scr.shape[0] * n_row
    assert n_chunks % n_stage == 0

    def aligned(v, m):
        return v if isinstance(v, int) else pl.multiple_of(v, m)

    def copy(k, slot):
        rows = pl.ds(aligned((k % n_row) * TM, TM), TM)
        cols = pl.ds(aligned((k // n_row) * WIDTH, WIDTH), WIDTH)
        return pltpu.make_async_copy(w_hbm.at[rows, cols], stages[slot], sem.at[slot])

    for k in range(n_stage):
        copy(k, k).start()

    def body(r, carry):
        for slot in range(n_stage):
            k = r * n_stage + slot
            copy(k, slot).wait()
            w_scr[k // n_row, pl.ds(aligned((k % n_row) * TM, TM), TM), :] = stages[slot][...].astype(BF16)

            @pl.when(k + n_stage < n_chunks)
            def _():
                copy(k + n_stage, slot).start()
        return carry

    lax.fori_loop(0, n_chunks // n_stage, body, 0)


def _layer_kernel(tiles_per_seq, n_tiles, n_dec, dec_t,
                  x_ref, xprev_ref, xs_hbm, cs_hbm, hs_hbm, w_in_hbm, w_out_hbm,
                  pre_g, post_g, conv_w, conv_b, wg, b_r, b_i, lam, sgu_g, w_s, bs_rows, ws_dec,
                  y_ref, conv_out, h_out, ys_hbm, convs_hbm, hs_out_hbm, vs_hbm,
                  w_in, w_out, z_scr, xr_scr, xp_scr, a_scr, b_scr, pre_scr, out_scr, cat_scr,
                  carry_h, carry_x, xr_tail, xs_buf, h0_buf, hn_buf, w_sem, in_sem, out_sem):
    hp_scr = xr_scr
    vs_buf = b_scr
    i = pl.program_id(0)
    t = i % tiles_per_seq
    cat_prev = cat_scr.at[(i + 1) % 2]
    cat_cur = cat_scr.at[i % 2]

    sq_sums = []

    def out_piece(c):
        piece = jnp.dot(cat_prev[...], w_out[c], preferred_element_type=F32)
        out_scr[:, c * WIDTH:(c + 1) * WIDTH] = piece
        sq_sums.append(jnp.sum(piece * piece, axis=-1, keepdims=True))

    def in_piece(slab):
        pre_scr[slab - 1] = jnp.dot(z_scr[...], w_in[slab], preferred_element_type=F32)

    def finish_prev():
        inv = lax.rsqrt(functools.reduce(lambda p, q: p + q, sq_sums) * (1.0 / D_MODEL) + EPS)
        sq_sums.clear()
        y_ref[...] = xprev_ref[...] + out_scr[...] * inv * post_g[...]

    @pl.when(i == 0)
    def _():
        halves = lambda ref: [ref.at[:, pl.ds(c * WIDTH, WIDTH)] for c in range(D_MODEL // WIDTH)]
        stages = [pre_scr.at[s] for s in range(pre_scr.shape[0])] + halves(out_scr) + halves(xs_buf)
        _load_weight(w_in_hbm, w_in, stages, w_sem)
        _load_weight(w_out_hbm, w_out, stages, w_sem)
        cat_scr[1] = jnp.zeros(cat_scr.shape[1:], BF16)

    @pl.when(t == 0)
    def _():
        carry_h[...] = jnp.zeros_like(carry_h)
        carry_x[...] = jnp.zeros_like(carry_x)

    @pl.when(i == n_tiles)
    def _():
        x_in, h_in, cs_in = _decode_copies_in(0, _decode_geometry(dec_t), xs_hbm, cs_hbm, hs_hbm, xs_buf, xp_scr, h0_buf,
                                              in_sem)
        for cp in [x_in, h_in] + cs_in:
            cp.start()
        for c in range(D_MODEL // WIDTH):
            out_piece(c)
        finish_prev()

    @pl.when(i > n_tiles)
    def _():
        _decode_tile(i - n_tiles - 1, n_dec, _decode_geometry(dec_t), xs_hbm, cs_hbm, hs_hbm, pre_g, post_g, conv_w, conv_b,
                     wg, b_r, b_i, lam, sgu_g, ws_dec, bs_rows, ys_hbm, convs_hbm, hs_out_hbm, vs_hbm, w_in, w_out, z_scr,
                     xr_scr, xp_scr, a_scr, b_scr, hp_scr, pre_scr, out_scr, cat_scr.at[0], xs_buf, vs_buf,
                     h0_buf, hn_buf, in_sem, out_sem)

    @pl.when(i < n_tiles)
    def _():
        geo = PROMPT
        R = geo.n_seg
        _pre_norm(x_ref, pre_g, z_scr)

        xr = jnp.dot(z_scr[...], w_in[XR], preferred_element_type=F32)
        xr_tail[...] = xr[TM - SUBLANES:TM, :]
        _to_step_order(xr, xr_scr, xp_scr, geo)
        sub = lax.broadcasted_iota(jnp.int32, (R, WIDTH), 0)
        for m in range(HIST):
            src = HIST * R + (geo.seg_len - HIST + m) * R
            rs = slice(m * R, (m + 1) * R)
            for c in range(N_SLABS):
                ls = slice(c * LANES, (c + 1) * LANES)
                rolled = pltpu.roll(xp_scr[c, src:src + R, :], 1, 0)
                xp_scr[c, rs, :] = jnp.where(sub[:, 0:LANES] == 0, carry_x[rs, ls], rolled)
                carry_x[rs, ls] = rolled

        reset_rows = (lax.broadcasted_iota(jnp.int32, (SUBLANES, HEAD_DIM), 0) == 0) & (t == 0)
        _lru_coeffs(xp_scr, a_scr, b_scr, (pre_scr.at[U - 1], pre_scr.at[GS - 1]), conv_w, conv_b, wg, b_r, b_i, lam,
                    R, reset_rows)

        in_piece(GR)
        hl = jnp.zeros((R, WIDTH), F32)
        pr = jnp.ones((R, WIDTH), F32)
        for k in range(geo.seg_len):
            a_k = a_scr[k * R:(k + 1) * R, :]
            hl = a_k * hl + b_scr[k * R:(k + 1) * R, :]
            pr = a_k * pr
        c_in = carry_h[0:1, :]
        h0 = jnp.zeros((R, WIDTH), F32)
        for j in range(R):
            h0 = jnp.where(sub == j, c_in, h0)
            c_in = hl[j:j + 1, :] + pr[j:j + 1, :] * c_in
        carry_h[0:1, :] = c_in
        h = h0
        for k in range(geo.seg_len):
            h = a_scr[k * R:(k + 1) * R, :] * h + b_scr[k * R:(k + 1) * R, :]
            for c in range(N_SLABS):
                hp_scr[c, pl.ds(k, R, stride=geo.seg_pitch), :] = h[:, c * LANES:(c + 1) * LANES]
        in_piece(V)
        for j in range(R):
            rs = slice(j * geo.seg_len, (j + 1) * geo.seg_len)
            for c in range(N_SLABS):
                ls = slice(c * LANES, (c + 1) * LANES)
                h_nat = hp_scr[c, j * geo.seg_pitch:j * geo.seg_pitch + geo.seg_len, :]
                cat_cur[rs, ls] = (h_nat * _silu(pre_scr[GR - 1, rs, ls])).astype(BF16)

        in_piece(U)
        in_piece(GS)
        cat_cur[:, WIDTH:2 * WIDTH] = _rms_norm(_gelu(pre_scr[V - 1]), sgu_g[...]).astype(BF16)
        sgu_head = _sgu_head_fn(pre_scr, w_s, bs_rows, cat_cur)
        for hd in range(N_HEADS):
            if hd % (N_HEADS // (D_MODEL // WIDTH)) == 0:
                out_piece(hd // (N_HEADS // (D_MODEL // WIDTH)))
            sgu_head(hd)
        finish_prev()

    @pl.when((i < n_tiles) & (t == tiles_per_seq - 1))
    def _():
        seq = i // tiles_per_seq
        for m in range(HIST):
            conv_out[m, pl.ds(seq, 1), :] = xr_tail[SUBLANES - HIST + m:SUBLANES - HIST + m + 1, :]
        h_out[pl.ds(seq, 1), :] = carry_h[0:1, :]


def _decode_copies_in(d, geo, xs_hbm, cs_hbm, hs_hbm, xs_buf, xp_scr, h0_buf, in_sem):
    R = geo.n_seg
    seqs = pl.ds(pl.multiple_of(d * R, R), R)
    x_in = pltpu.make_async_copy(xs_hbm.at[pl.ds(pl.multiple_of(d * TM, TM), TM), :], xs_buf, in_sem.at[0])
    h_in = pltpu.make_async_copy(hs_hbm.at[seqs, :], h0_buf, in_sem.at[1])
    cs_in = [pltpu.make_async_copy(cs_hbm.at[m, seqs, pl.ds(c * LANES, LANES)], xp_scr.at[c, pl.ds(m * R, R), :],
                                   in_sem.at[2 + m * N_SLABS + c])
             for m in range(HIST) for c in range(N_SLABS)]
    return x_in, h_in, cs_in


def _decode_tile(d, n_dec, geo, xs_hbm, cs_hbm, hs_hbm, pre_g, post_g, conv_w, conv_b, wg, b_r, b_i, lam, sgu_g,
                 w_mix, bias_rows, ys_hbm, convs_hbm, hs_out_hbm, vs_hbm, w_in, w_out, z_scr, xr_scr, xp_scr, a_scr, b_scr,
                 hp_scr, pre_scr, out_scr, cat, xs_buf, vs_buf, h0_buf, hn_buf, in_sem, out_sem):
    R, L = geo.n_seg, geo.seg_len
    tile_rows = lambda dd: pl.ds(pl.multiple_of(dd * TM, TM), TM)
    tile_seqs = lambda dd: pl.ds(pl.multiple_of(dd * R, R), R)

    def copies_out(dd):
        main = [pltpu.make_async_copy(out_scr, ys_hbm.at[tile_rows(dd), :], out_sem.at[0]),
                pltpu.make_async_copy(vs_buf, vs_hbm.at[tile_rows(dd), :], out_sem.at[1]),
                pltpu.make_async_copy(hn_buf, hs_out_hbm.at[tile_seqs(dd), :], out_sem.at[2])]
        conv = [pltpu.make_async_copy(xp_scr.at[c, pl.ds((L + m) * R, R), :],
                                      convs_hbm.at[m, tile_seqs(dd), pl.ds(c * LANES, LANES)],
                                      out_sem.at[3 + m * N_SLABS + c])
                for m in range(HIST) for c in range(N_SLABS)]
        return main + conv

    x_in, h_in, cs_in = _decode_copies_in(d, geo, xs_hbm, cs_hbm, hs_hbm, xs_buf, xp_scr, h0_buf, in_sem)
    x_next, h_next, cs_next = _decode_copies_in(d + 1, geo, xs_hbm, cs_hbm, hs_hbm, xs_buf, xp_scr, h0_buf, in_sem)
    has_next = d + 1 < n_dec
    for cp in [x_in, h_in] + cs_in:
        cp.wait()

    @pl.when(d > 0)
    def _():
        for cp in copies_out(d - 1):
            cp.wait()

    _pre_norm(xs_buf, pre_g, z_scr)

    xr = jnp.dot(z_scr[...], w_in[XR], preferred_element_type=F32)
    _to_step_order(xr, xr_scr, xp_scr, geo)
    _lru_coeffs(xp_scr, a_scr, b_scr, (pre_scr.at[U - 1], pre_scr.at[GS - 1]), conv_w, conv_b, wg, b_r, b_i, lam, R,
                None)

    @pl.when(has_next)
    def _():
        for cp in cs_next:
            cp.start()

    for slab in (GR, U, V, GS):
        pre_scr[slab - 1] = jnp.dot(z_scr[...], w_in[slab], preferred_element_type=F32)
    for c in range(N_SLABS):
        ls = slice(c * LANES, (c + 1) * LANES)
        h = h0_buf[:, ls]
        for s in range(L):
            h = a_scr[s * R:(s + 1) * R, ls] * h + b_scr[s * R:(s + 1) * R, ls]
            hp_scr[c, s * geo.step_pitch:s * geo.step_pitch + R, :] = h
        hn_buf[:, ls] = h

    @pl.when(has_next)
    def _():
        h_next.start()

    _to_natural_order(hp_scr, a_scr, geo)
    cat[:, 0:WIDTH] = (a_scr[...] * _silu(pre_scr[GR - 1])).astype(BF16)

    vs_buf[...] = _rms_norm(_gelu(pre_scr[V - 1]), sgu_g[...])
    cat[:, WIDTH:2 * WIDTH] = vs_buf[...].astype(BF16)
    sgu_head = _sgu_head_fn(pre_scr, w_mix, bias_rows, cat, period=L)
    for hd in range(N_HEADS):
        sgu_head(hd)

    for c in range(D_MODEL // WIDTH):
        out_scr[:, c * WIDTH:(c + 1) * WIDTH] = jnp.dot(cat[...], w_out[c], preferred_element_type=F32)
    out_scr[...] = xs_buf[...] + _rms_norm(out_scr[...], post_g[...])

    for cp in copies_out(d):
        cp.start()

    @pl.when(has_next)
    def _():
        x_next.start()

    @pl.when(d == n_dec - 1)
    def _():
        for cp in copies_out(d):
            cp.wait()


def _full(shape):
    return pl.BlockSpec(shape, lambda *_: (0,) * len(shape))


def _layer(x_prompt, x_sample, conv_state, h_state, pre_g, post_g, w_in, conv_w, conv_b, w_r, b_r, w_i, b_i, lam,
           sgu_g, w_s, b_s, w_out):
    n_seq, seq_len, d_model = x_prompt.shape
    dec_b, dec_t, _ = x_sample.shape
    assert d_model == D_MODEL and conv_w.shape == (CONV_WIDTH, WIDTH) and w_in.shape == (D_MODEL, 5 * WIDTH)
    assert seq_len % TM == 0 and dec_t * DEC_NB == TM and dec_b % DEC_NB == 0
    assert dec_t % SUBLANES == 0 and CHUNK % dec_t == 0 and HIST <= dec_t and dec_t & (dec_t - 1) == 0

    wg = jnp.concatenate([w_r, w_i], axis=-1).astype(BF16)
    row = lambda p: p.reshape(1, -1)
    bs_prompt = jnp.repeat(b_s.T, HEAD_DIM, axis=1)
    ws_dec = jnp.tile(w_s[:, :dec_t, :dec_t], (1, 1, CHUNK // dec_t))
    params = (row(pre_g), row(post_g), conv_w, row(conv_b), wg, row(b_r), row(b_i), row(lam), row(sgu_g), w_s,
              bs_prompt, ws_dec)

    tiles = seq_len // TM
    n_tiles = n_seq * tiles
    n_dec = dec_b // DEC_NB
    x_rows = x_prompt.reshape(n_seq * seq_len, d_model)
    xs_rows = x_sample.reshape(dec_b * dec_t, d_model)
    cs_steps = jnp.transpose(conv_state, (1, 0, 2))
    tile_of = lambda i: jnp.minimum(i, n_tiles - 1)
    prev_of = lambda i: jnp.clip(i - 1, 0, n_tiles - 1)
    any_spec = pl.BlockSpec(memory_space=pl.ANY)
    dec_geo = _decode_geometry(dec_t)
    relayout_rows = max(g.n_seg * g.seg_pitch for g in (PROMPT, dec_geo))
    assert relayout_rows >= max(g.seg_len * g.step_pitch for g in (PROMPT, dec_geo))
    y_rows, conv_p, h_p, ys_rows, conv_s, h_s, vs_rows = pl.pallas_call(
        functools.partial(_layer_kernel, tiles, n_tiles, n_dec, dec_t),
        grid=(n_tiles + 1 + n_dec,),
        in_specs=[pl.BlockSpec((TM, d_model), lambda i: (tile_of(i), 0)),
                  pl.BlockSpec((TM, d_model), lambda i: (prev_of(i), 0)),
                  any_spec, any_spec, any_spec, any_spec, any_spec] + [_full(p.shape) for p in params],
        out_specs=[pl.BlockSpec((TM, d_model), lambda i: (prev_of(i), 0)),
                   _full((HIST, n_seq, WIDTH)), _full((n_seq, WIDTH)),
                   any_spec, any_spec, any_spec, any_spec],
        out_shape=[jax.ShapeDtypeStruct(x_rows.shape, F32),
                   jax.ShapeDtypeStruct((HIST, n_seq, WIDTH), F32),
                   jax.ShapeDtypeStruct((n_seq, WIDTH), F32),
                   jax.ShapeDtypeStruct(xs_rows.shape, F32),
                   jax.ShapeDtypeStruct(cs_steps.shape, F32),
                   jax.ShapeDtypeStruct(h_state.shape, F32),
                   jax.ShapeDtypeStruct((dec_b * dec_t, WIDTH), F32)],
        scratch_shapes=[pltpu.VMEM((5, D_MODEL, WIDTH), BF16),
                        pltpu.VMEM((D_MODEL // WIDTH, D_MODEL, WIDTH), BF16),
                        pltpu.VMEM((TM, D_MODEL), BF16),
                        pltpu.VMEM((N_SLABS, relayout_rows, LANES), F32),
                        pltpu.VMEM((N_SLABS, TM + HIST * DEC_NB, LANES), F32),
                        pltpu.VMEM((TM, WIDTH), F32), pltpu.VMEM((TM, WIDTH), F32),
                        pltpu.VMEM((N_PRE, TM, WIDTH), F32),
                        pltpu.VMEM((TM, D_MODEL), F32),
                        pltpu.VMEM((2, TM, D_MODEL), BF16),
                        pltpu.VMEM((SUBLANES, WIDTH), F32),
                        pltpu.VMEM((HIST * SUBLANES, WIDTH), F32),
                        pltpu.VMEM((SUBLANES, WIDTH), F32),
                        pltpu.VMEM((TM, D_MODEL), F32),
                        pltpu.VMEM((DEC_NB, WIDTH), F32), pltpu.VMEM((DEC_NB, WIDTH), F32),
                        pltpu.SemaphoreType.DMA((N_STAGE,)),
                        pltpu.SemaphoreType.DMA((2 + HIST * N_SLABS,)),
                        pltpu.SemaphoreType.DMA((3 + HIST * N_SLABS,))],
        compiler_params=pltpu.CompilerParams(dimension_semantics=("arbitrary",),
                                             vmem_limit_bytes=VMEM_LIMIT_BYTES),
        name="hybrid_layer",
    )(x_rows, x_rows, xs_rows, cs_steps, h_state, w_in, w_out, *params)
    seq_major = lambda c: jnp.transpose(c, (1, 0, 2))
    return (y_rows.reshape(x_prompt.shape), ys_rows.reshape(x_sample.shape), seq_major(conv_p), h_p,
            seq_major(conv_s), h_s, vs_rows.reshape(dec_b, dec_t, WIDTH))


def kernel(x_prompt, x_sample, state_rglru_conv, state_rglru_h, pre_norm_g, post_norm_g, w_in, conv_w, conv_b,
           w_rgate, b_rgate, w_igate, b_igate, lru_lambda, sgu_norm_g, w_spatial, b_spatial, w_out):
    depth = w_in.shape[0]
    yp, ys = x_prompt, x_sample
    conv_p, h_p, conv_s, h_s, v_s = [], [], [], [], []
    for l in range(depth):
        yp, ys, cp, hp, cs, hs, vs = _layer(
            yp, ys, state_rglru_conv[l], state_rglru_h[l], pre_norm_g[l], post_norm_g[l], w_in[l], conv_w[l],
            conv_b[l], w_rgate[l], b_rgate[l], w_igate[l], b_igate[l], lru_lambda[l], sgu_norm_g[l], w_spatial[l],
            b_spatial[l], w_out[l])
        conv_p.append(cp)
        h_p.append(hp)
        conv_s.append(cs)
        h_s.append(hs)
        v_s.append(vs)
    return (yp, ys, jnp.stack(conv_p), jnp.stack(h_p), jnp.stack(conv_s), jnp.stack(h_s), jnp.stack(v_s))
```

```python
import collections
import functools
import math

import jax
import jax.numpy as jnp
from jax import lax
from jax.experimental import pallas as pl
from jax.experimental.pallas import tpu as pltpu

F32 = jnp.float32
BF16 = jnp.bfloat16

EPS = 1e-6
LRU_C = 8.0
CONV_WIDTH = 4
HIST = CONV_WIDTH - 1
N_HEADS = 8
HEAD_DIM = 128
CHUNK = 128
LANES = 128
SUBLANES = 8
N_SLABS = 8
MXU_DIM = 256

D_MODEL = 2048
WIDTH = 1024
TM = 256
XR, GR, U, V, GS = range(5)
N_PRE = 4
N_STAGE = N_PRE + 2 * (D_MODEL // WIDTH)
VMEM_LIMIT_BYTES = 63 * 1024 * 1024

Geometry = collections.namedtuple("Geometry", "n_seg seg_len seg_pitch step_pitch")
PROMPT = Geometry(n_seg=SUBLANES, seg_len=TM // SUBLANES, seg_pitch=TM // SUBLANES + SUBLANES, step_pitch=SUBLANES)
DEC_NB = 32


def _decode_geometry(n_steps):
    return Geometry(n_seg=TM // n_steps, seg_len=n_steps, seg_pitch=n_steps, step_pitch=TM // n_steps + SUBLANES)


def _sigmoid(x):
    return 0.5 * jnp.tanh(0.5 * x) + 0.5


def _silu(x):
    return x * _sigmoid(x)


def _gelu(x):
    c = math.sqrt(2.0 / math.pi)
    return x * (0.5 * (1.0 + jnp.tanh(c * (x + 0.044715 * (x * x * x)))))


def _rms_norm(x, g):
    return x * lax.rsqrt(jnp.mean(x * x, axis=-1, keepdims=True) + EPS) * g


def _pre_norm(x_ref, g_ref, z_scr):
    x = x_ref[...]
    inv = lax.rsqrt(jnp.mean(x * x, axis=-1, keepdims=True) + EPS)
    for kb in range(D_MODEL // MXU_DIM):
        cs = slice(kb * MXU_DIM, (kb + 1) * MXU_DIM)
        z_scr[:, cs] = (x[:, cs] * inv * g_ref[:, cs]).astype(BF16)


def _to_step_order(xr, xr_scr, xp_scr, geo):
    R, L, P = geo.n_seg, geo.seg_len, geo.seg_pitch
    if R == SUBLANES:
        for s in range(R):
            for m in range(L // SUBLANES):
                for c in range(N_SLABS):
                    xp_scr[c, pl.ds(HIST * R + m * SUBLANES * R + s, SUBLANES, stride=R), :] = (
                        xr[s * L + m * SUBLANES:s * L + (m + 1) * SUBLANES, c * LANES:(c + 1) * LANES])
        return
    for s in range(R):
        for c in range(N_SLABS):
            xr_scr[c, s * P:s * P + L, :] = xr[s * L:(s + 1) * L, c * LANES:(c + 1) * LANES]
    for k in range(L):
        for g in range(R // SUBLANES):
            row = HIST * R + k * R + g * SUBLANES
            for c in range(N_SLABS):
                xp_scr[c, row:row + SUBLANES, :] = xr_scr[c, pl.ds(g * SUBLANES * P + k, SUBLANES, stride=P), :]


def _to_natural_order(hp_scr, dst_scr, geo):
    R, L, P = geo.n_seg, geo.seg_len, geo.step_pitch
    for s in range(R):
        for m in range(L // SUBLANES):
            row = s * L + m * SUBLANES
            for c in range(N_SLABS):
                dst_scr[row:row + SUBLANES, c * LANES:(c + 1) * LANES] = (
                    hp_scr[c, pl.ds(m * SUBLANES * P + s, SUBLANES, stride=P), :])


def _lru_coeffs(xp_scr, a_scr, b_scr, gate_scr, conv_w, conv_b, wg, b_r, b_i, lam, rows_per_step, reset_rows):
    R = rows_per_step
    for h in range(N_HEADS):
        ls = slice(h * HEAD_DIM, (h + 1) * HEAD_DIM)
        xc = conv_b[:, ls] + xp_scr[h, 0:TM, :] * conv_w[0:1, ls]
        for k in range(1, CONV_WIDTH):
            xc = xc + xp_scr[h, k * R:k * R + TM, :] * conv_w[k:k + 1, ls]
        b_scr[:, ls] = xc
    for h in range(N_HEADS):
        ls = slice(h * HEAD_DIM, (h + 1) * HEAD_DIM)
        gate_scr[h // 4][:, (h % 4) * 2 * HEAD_DIM:(h % 4 + 1) * 2 * HEAD_DIM] = jnp.dot(
            b_scr[:, ls].astype(BF16), wg[h], preferred_element_type=F32)
    for h in range(N_HEADS):
        ls = slice(h * HEAD_DIM, (h + 1) * HEAD_DIM)
        xc = b_scr[:, ls]
        g = gate_scr[h // 4][:, (h % 4) * 2 * HEAD_DIM:(h % 4 + 1) * 2 * HEAD_DIM]
        r = _sigmoid(g[:, :HEAD_DIM] + b_r[:, ls])
        i = _sigmoid(g[:, HEAD_DIM:] + b_i[:, ls])
        lam_h = lam[:, ls]
        softplus_neg = jnp.maximum(-lam_h, 0.0) + jnp.log1p(jnp.exp(-jnp.abs(lam_h)))
        log_a = r * (-LRU_C * softplus_neg)
        a = jnp.exp(log_a)
        mult = jnp.sqrt(-jnp.tanh(log_a) * (a * a + 1.0))
        ix = i * xc
        b = mult * ix
        a_scr[:, ls] = a
        b_scr[:, ls] = b
        if reset_rows is not None:
            a_scr[0:SUBLANES, ls] = jnp.where(reset_rows, 0.0, a[0:SUBLANES])
            b_scr[0:SUBLANES, ls] = jnp.where(reset_rows, ix[0:SUBLANES], b[0:SUBLANES])


def _sgu_head_fn(pre_scr, w_mix, bias_rows, cat, period=CHUNK):
    row = lax.broadcasted_iota(jnp.int32, (CHUNK, CHUNK), 0)
    col = lax.broadcasted_iota(jnp.int32, (CHUNK, CHUNK), 1)
    keep = row >= col
    if period < CHUNK:
        shift = period.bit_length() - 1
        keep = keep & ((row >> shift) == (col >> shift))
    blocks = [slice(ch * CHUNK, (ch + 1) * CHUNK) for ch in range(TM // CHUNK)]

    def head(hd):
        ls = slice(hd * HEAD_DIM, (hd + 1) * HEAD_DIM)
        vs = slice(WIDTH + hd * HEAD_DIM, WIDTH + (hd + 1) * HEAD_DIM)
        w_h = jnp.where(keep, jnp.tile(w_mix[hd], (CHUNK // period, 1)), 0.0).astype(BF16)
        bias = jnp.tile(bias_rows[0:period, ls], (CHUNK // period, 1))
        mixed = jnp.dot(w_h, jnp.concatenate([cat[rs, vs] for rs in blocks], axis=1), preferred_element_type=F32)
        for ch, rs in enumerate(blocks):
            s = mixed[:, ch * HEAD_DIM:(ch + 1) * HEAD_DIM] + bias
            cat[rs, vs] = (_gelu(pre_scr[U - 1, rs, ls]) * s * _silu(pre_scr[GS - 1, rs, ls])).astype(BF16)

    return head


def _load_weight(w_hbm, w_scr, stages, sem):
    n_stage = len(stages)
    n_row = w_hbm.shape[0] // TM
    n_chunks = w_scr.shape[0] * n_row
    assert n_chunks % n_stage == 0

    def aligned(v, m):
        return v if isinstance(v, int) else pl.multiple_of(v, m)

    def copy(k, slot):
        rows = pl.ds(aligned((k % n_row) * TM, TM), TM)
        cols = pl.ds(aligned((k // n_row) * WIDTH, WIDTH), WIDTH)
        return pltpu.make_async_copy(w_hbm.at[rows, cols], stages[slot], sem.at[slot])

    for k in range(n_stage):
        copy(k, k).start()

    def body(r, carry):
        for slot in range(n_stage):
            k = r * n_stage + slot
            copy(k, slot).wait()
            w_scr[k // n_row, pl.ds(aligned((k % n_row) * TM, TM), TM), :] = stages[slot][...].astype(BF16)

            @pl.when(k + n_stage < n_chunks)
            def _():
                copy(k + n_stage, slot).start()
        return carry

    lax.fori_loop(0, n_chunks // n_stage, body, 0)


def _layer_kernel(tiles_per_seq, n_tiles, n_dec, dec_t,
                  x_ref, xprev_ref, xs_hbm, cs_hbm, hs_hbm, w_in_hbm, w_out_hbm,
                  pre_g, post_g, conv_w, conv_b, wg, b_r, b_i, lam, sgu_g, w_s, bs_rows, ws_dec,
                  y_ref, conv_out, h_out, ys_hbm, convs_hbm, hs_out_hbm, vs_hbm,
                  w_in, w_out, z_scr, xr_nat, xr_scr, xp_scr, a_scr, b_scr, pre_scr, out_scr, cat_scr,
                  carry_h, carry_x, xr_tail, xs_buf, h0_buf, hn_buf, w_sem, in_sem, out_sem):
    hp_scr = xr_scr
    vs_buf = b_scr
    i = pl.program_id(0)
    t = i % tiles_per_seq
    cat_prev = cat_scr.at[(i + 1) % 2]
    cat_cur = cat_scr.at[i % 2]
    z_cur = z_scr.at[i % 2]
    z_next = z_scr.at[(i + 1) % 2]

    sq_sums = []

    def out_piece(c):
        piece = jnp.dot(cat_prev[...], w_out[c], preferred_element_type=F32)
        out_scr[:, c * WIDTH:(c + 1) * WIDTH] = piece
        sq_sums.append(jnp.sum(piece * piece, axis=-1, keepdims=True))

    def in_piece(slab):
        pre_scr[slab - 1] = jnp.dot(z_cur[...], w_in[slab], preferred_element_type=F32)

    def finish_prev():
        inv = lax.rsqrt(functools.reduce(lambda p, q: p + q, sq_sums) * (1.0 / D_MODEL) + EPS)
        sq_sums.clear()
        y_ref[...] = xprev_ref[...] + out_scr[...] * inv * post_g[...]

    @pl.when(i == 0)
    def _():
        halves = lambda ref: [ref.at[:, pl.ds(c * WIDTH, WIDTH)] for c in range(D_MODEL // WIDTH)]
        stages = [pre_scr.at[s] for s in range(pre_scr.shape[0])] + halves(out_scr) + halves(xs_buf)
        _load_weight(w_in_hbm, w_in, stages, w_sem)
        _load_weight(w_out_hbm, w_out, stages, w_sem)
        cat_scr[1] = jnp.zeros(cat_scr.shape[1:], BF16)
        _pre_norm(xprev_ref, pre_g, z_cur)
        xr_nat[...] = jnp.dot(z_cur[...], w_in[XR], preferred_element_type=F32)

    @pl.when(t == 0)
    def _():
        carry_h[...] = jnp.zeros_like(carry_h)
        carry_x[...] = jnp.zeros_like(carry_x)

    @pl.when(i == n_tiles)
    def _():
        x_in, h_in, cs_in = _decode_copies_in(0, _decode_geometry(dec_t), xs_hbm, cs_hbm, hs_hbm, xs_buf, xp_scr, h0_buf,
                                              in_sem)
        for cp in [x_in, h_in] + cs_in:
            cp.start()
        for c in range(D_MODEL // WIDTH):
            out_piece(c)
        finish_prev()

    @pl.when(i > n_tiles)
    def _():
        _decode_tile(i - n_tiles - 1, n_dec, _decode_geometry(dec_t), xs_hbm, cs_hbm, hs_hbm, pre_g, post_g, conv_w, conv_b,
                     wg, b_r, b_i, lam, sgu_g, ws_dec, bs_rows, ys_hbm, convs_hbm, hs_out_hbm, vs_hbm, w_in, w_out, z_scr.at[0],
                     xr_scr, xp_scr, a_scr, b_scr, hp_scr, pre_scr, out_scr, cat_scr.at[0], xs_buf, vs_buf,
                     h0_buf, hn_buf, in_sem, out_sem)

    @pl.when(i < n_tiles)
    def _():
        geo = PROMPT
        R = geo.n_seg

        xr = xr_nat[...]
        xr_tail[...] = xr[TM - SUBLANES:TM, :]
        _to_step_order(xr, xr_scr, xp_scr, geo)
        sub = lax.broadcasted_iota(jnp.int32, (R, WIDTH), 0)
        for m in range(HIST):
            src = HIST * R + (geo.seg_len - HIST + m) * R
            rs = slice(m * R, (m + 1) * R)
            for c in range(N_SLABS):
                ls = slice(c * LANES, (c + 1) * LANES)
                rolled = pltpu.roll(xp_scr[c, src:src + R, :], 1, 0)
                xp_scr[c, rs, :] = jnp.where(sub[:, 0:LANES] == 0, carry_x[rs, ls], rolled)
                carry_x[rs, ls] = rolled

        reset_rows = (lax.broadcasted_iota(jnp.int32, (SUBLANES, HEAD_DIM), 0) == 0) & (t == 0)
        _lru_coeffs(xp_scr, a_scr, b_scr, (pre_scr.at[U - 1], pre_scr.at[GS - 1]), conv_w, conv_b, wg, b_r, b_i, lam,
                    R, reset_rows)

        in_piece(GR)
        hl = jnp.zeros((R, WIDTH), F32)
        pr = jnp.ones((R, WIDTH), F32)
        for k in range(geo.seg_len):
            a_k = a_scr[k * R:(k + 1) * R, :]
            hl = a_k * hl + b_scr[k * R:(k + 1) * R, :]
            pr = a_k * pr
        c_in = carry_h[0:1, :]
        h0 = jnp.zeros((R, WIDTH), F32)
        for j in range(R):
            h0 = jnp.where(sub == j, c_in, h0)
            c_in = hl[j:j + 1, :] + pr[j:j + 1, :] * c_in
        carry_h[0:1, :] = c_in
        h = h0
        for k in range(geo.seg_len):
            h = a_scr[k * R:(k + 1) * R, :] * h + b_scr[k * R:(k + 1) * R, :]
            for c in range(N_SLABS):
                hp_scr[c, pl.ds(k, R, stride=geo.seg_pitch), :] = h[:, c * LANES:(c + 1) * LANES]
        in_piece(V)
        for j in range(R):
            rs = slice(j * geo.seg_len, (j + 1) * geo.seg_len)
            for c in range(N_SLABS):
                ls = slice(c * LANES, (c + 1) * LANES)
                h_nat = hp_scr[c, j * geo.seg_pitch:j * geo.seg_pitch + geo.seg_len, :]
                cat_cur[rs, ls] = (h_nat * _silu(pre_scr[GR - 1, rs, ls])).astype(BF16)

        in_piece(U)
        in_piece(GS)
        cat_cur[:, WIDTH:2 * WIDTH] = _rms_norm(_gelu(pre_scr[V - 1]), sgu_g[...]).astype(BF16)
        sgu_head = _sgu_head_fn(pre_scr, w_s, bs_rows, cat_cur)
        for hd in range(N_HEADS):
            if hd % (N_HEADS // (D_MODEL // WIDTH)) == 0:
                out_piece(hd // (N_HEADS // (D_MODEL // WIDTH)))
            sgu_head(hd)
        _pre_norm(x_ref, pre_g, z_next)
        xr_nat[...] = jnp.dot(z_next[...], w_in[XR], preferred_element_type=F32)
        finish_prev()

    @pl.when((i < n_tiles) & (t == tiles_per_seq - 1))
    def _():
        seq = i // tiles_per_seq
        for m in range(HIST):
            conv_out[m, pl.ds(seq, 1), :] = xr_tail[SUBLANES - HIST + m:SUBLANES - HIST + m + 1, :]
        h_out[pl.ds(seq, 1), :] = carry_h[0:1, :]


def _decode_copies_in(d, geo, xs_hbm, cs_hbm, hs_hbm, xs_buf, xp_scr, h0_buf, in_sem):
    R = geo.n_seg
    seqs = pl.ds(pl.multiple_of(d * R, R), R)
    x_in = pltpu.make_async_copy(xs_hbm.at[pl.ds(pl.multiple_of(d * TM, TM), TM), :], xs_buf, in_sem.at[0])
    h_in = pltpu.make_async_copy(hs_hbm.at[seqs, :], h0_buf, in_sem.at[1])
    cs_in = [pltpu.make_async_copy(cs_hbm.at[m, seqs, pl.ds(c * LANES, LANES)], xp_scr.at[c, pl.ds(m * R, R), :],
                                   in_sem.at[2 + m * N_SLABS + c])
             for m in range(HIST) for c in range(N_SLABS)]
    return x_in, h_in, cs_in


def _decode_tile(d, n_dec, geo, xs_hbm, cs_hbm, hs_hbm, pre_g, post_g, conv_w, conv_b, wg, b_r, b_i, lam, sgu_g,
                 w_mix, bias_rows, ys_hbm, convs_hbm, hs_out_hbm, vs_hbm, w_in, w_out, z_scr, xr_scr, xp_scr, a_scr, b_scr,
                 hp_scr, pre_scr, out_scr, cat, xs_buf, vs_buf, h0_buf, hn_buf, in_sem, out_sem):
    R, L = geo.n_seg, geo.seg_len
    tile_rows = lambda dd: pl.ds(pl.multiple_of(dd * TM, TM), TM)
    tile_seqs = lambda dd: pl.ds(pl.multiple_of(dd * R, R), R)

    def copies_out(dd):
        main = [pltpu.make_async_copy(out_scr, ys_hbm.at[tile_rows(dd), :], out_sem.at[0]),
                pltpu.make_async_copy(vs_buf, vs_hbm.at[tile_rows(dd), :], out_sem.at[1]),
                pltpu.make_async_copy(hn_buf, hs_out_hbm.at[tile_seqs(dd), :], out_sem.at[2])]
        conv = [pltpu.make_async_copy(xp_scr.at[c, pl.ds((L + m) * R, R), :],
                                      convs_hbm.at[m, tile_seqs(dd), pl.ds(c * LANES, LANES)],
                                      out_sem.at[3 + m * N_SLABS + c])
                for m in range(HIST) for c in range(N_SLABS)]
        return main + conv

    x_in, h_in, cs_in = _decode_copies_in(d, geo, xs_hbm, cs_hbm, hs_hbm, xs_buf, xp_scr, h0_buf, in_sem)
    x_next, h_next, cs_next = _decode_copies_in(d + 1, geo, xs_hbm, cs_hbm, hs_hbm, xs_buf, xp_scr, h0_buf, in_sem)
    has_next = d + 1 < n_dec
    for cp in [x_in, h_in] + cs_in:
        cp.wait()

    @pl.when(d > 0)
    def _():
        for cp in copies_out(d - 1):
            cp.wait()

    _pre_norm(xs_buf, pre_g, z_scr)

    xr = jnp.dot(z_scr[...], w_in[XR], preferred_element_type=F32)
    _to_step_order(xr, xr_scr, xp_scr, geo)
    _lru_coeffs(xp_scr, a_scr, b_scr, (pre_scr.at[U - 1], pre_scr.at[GS - 1]), conv_w, conv_b, wg, b_r, b_i, lam, R,
                None)

    @pl.when(has_next)
    def _():
        for cp in cs_next:
            cp.start()

    for slab in (GR, U, V, GS):
        pre_scr[slab - 1] = jnp.dot(z_scr[...], w_in[slab], preferred_element_type=F32)
    for c in range(N_SLABS):
        ls = slice(c * LANES, (c + 1) * LANES)
        h = h0_buf[:, ls]
        for s in range(L):
            h = a_scr[s * R:(s + 1) * R, ls] * h + b_scr[s * R:(s + 1) * R, ls]
            hp_scr[c, s * geo.step_pitch:s * geo.step_pitch + R, :] = h
        hn_buf[:, ls] = h

    @pl.when(has_next)
    def _():
        h_next.start()

    _to_natural_order(hp_scr, a_scr, geo)
    cat[:, 0:WIDTH] = (a_scr[...] * _silu(pre_scr[GR - 1])).astype(BF16)

    vs_buf[...] = _rms_norm(_gelu(pre_scr[V - 1]), sgu_g[...])
    cat[:, WIDTH:2 * WIDTH] = vs_buf[...].astype(BF16)
    sgu_head = _sgu_head_fn(pre_scr, w_mix, bias_rows, cat, period=L)
    for hd in range(N_HEADS):
        sgu_head(hd)

    for c in range(D_MODEL // WIDTH):
        out_scr[:, c * WIDTH:(c + 1) * WIDTH] = jnp.dot(cat[...], w_out[c], preferred_element_type=F32)
    out_scr[...] = xs_buf[...] + _rms_norm(out_scr[...], post_g[...])

    for cp in copies_out(d):
        cp.start()

    @pl.when(has_next)
    def _():
        x_next.start()

    @pl.when(d == n_dec - 1)
    def _():
        for cp in copies_out(d):
            cp.wait()


def _full(shape):
    return pl.BlockSpec(shape, lambda *_: (0,) * len(shape))


def _layer(x_prompt, x_sample, conv_state, h_state, pre_g, post_g, w_in, conv_w, conv_b, w_r, b_r, w_i, b_i, lam,
           sgu_g, w_s, b_s, w_out):
    n_seq, seq_len, d_model = x_prompt.shape
    dec_b, dec_t, _ = x_sample.shape
    assert d_model == D_MODEL and conv_w.shape == (CONV_WIDTH, WIDTH) and w_in.shape == (D_MODEL, 5 * WIDTH)
    assert seq_len % TM == 0 and dec_t * DEC_NB == TM and dec_b % DEC_NB == 0
    assert dec_t % SUBLANES == 0 and CHUNK % dec_t == 0 and HIST <= dec_t and dec_t & (dec_t - 1) == 0

    wg = jnp.concatenate([w_r, w_i], axis=-1).astype(BF16)
    row = lambda p: p.reshape(1, -1)
    bs_prompt = jnp.repeat(b_s.T, HEAD_DIM, axis=1)
    ws_dec = jnp.tile(w_s[:, :dec_t, :dec_t], (1, 1, CHUNK // dec_t))
    params = (row(pre_g), row(post_g), conv_w, row(conv_b), wg, row(b_r), row(b_i), row(lam), row(sgu_g), w_s,
              bs_prompt, ws_dec)

    tiles = seq_len // TM
    n_tiles = n_seq * tiles
    n_dec = dec_b // DEC_NB
    x_rows = x_prompt.reshape(n_seq * seq_len, d_model)
    xs_rows = x_sample.reshape(dec_b * dec_t, d_model)
    cs_steps = jnp.transpose(conv_state, (1, 0, 2))
    next_of = lambda i: jnp.minimum(i + 1, n_tiles - 1)
    prev_of = lambda i: jnp.clip(i - 1, 0, n_tiles - 1)
    any_spec = pl.BlockSpec(memory_space=pl.ANY)
    dec_geo = _decode_geometry(dec_t)
    relayout_rows = max(g.n_seg * g.seg_pitch for g in (PROMPT, dec_geo))
    assert relayout_rows >= max(g.seg_len * g.step_pitch for g in (PROMPT, dec_geo))
    y_rows, conv_p, h_p, ys_rows, conv_s, h_s, vs_rows = pl.pallas_call(
        functools.partial(_layer_kernel, tiles, n_tiles, n_dec, dec_t),
        grid=(n_tiles + 1 + n_dec,),
        in_specs=[pl.BlockSpec((TM, d_model), lambda i: (next_of(i), 0)),
                  pl.BlockSpec((TM, d_model), lambda i: (prev_of(i), 0)),
                  any_spec, any_spec, any_spec, any_spec, any_spec] + [_full(p.shape) for p in params],
        out_specs=[pl.BlockSpec((TM, d_model), lambda i: (prev_of(i), 0)),
                   _full((HIST, n_seq, WIDTH)), _full((n_seq, WIDTH)),
                   any_spec, any_spec, any_spec, any_spec],
        out_shape=[jax.ShapeDtypeStruct(x_rows.shape, F32),
                   jax.ShapeDtypeStruct((HIST, n_seq, WIDTH), F32),
                   jax.ShapeDtypeStruct((n_seq, WIDTH), F32),
                   jax.ShapeDtypeStruct(xs_rows.shape, F32),
                   jax.ShapeDtypeStruct(cs_steps.shape, F32),
                   jax.ShapeDtypeStruct(h_state.shape, F32),
                   jax.ShapeDtypeStruct((dec_b * dec_t, WIDTH), F32)],
        scratch_shapes=[pltpu.VMEM((5, D_MODEL, WIDTH), BF16),
                        pltpu.VMEM((D_MODEL // WIDTH, D_MODEL, WIDTH), BF16),
                        pltpu.VMEM((2, TM, D_MODEL), BF16),
                        pltpu.VMEM((TM, WIDTH), F32),
                        pltpu.VMEM((N_SLABS, relayout_rows, LANES), F32),
                        pltpu.VMEM((N_SLABS, TM + HIST * DEC_NB, LANES), F32),
                        pltpu.VMEM((TM, WIDTH), F32), pltpu.VMEM((TM, WIDTH), F32),
                        pltpu.VMEM((N_PRE, TM, WIDTH), F32),
                        pltpu.VMEM((TM, D_MODEL), F32),
                        pltpu.VMEM((2, TM, D_MODEL), BF16),
                        pltpu.VMEM((SUBLANES, WIDTH), F32),
                        pltpu.VMEM((HIST * SUBLANES, WIDTH), F32),
                        pltpu.VMEM((SUBLANES, WIDTH), F32),
                        pltpu.VMEM((TM, D_MODEL), F32),
                        pltpu.VMEM((DEC_NB, WIDTH), F32), pltpu.VMEM((DEC_NB, WIDTH), F32),
                        pltpu.SemaphoreType.DMA((N_STAGE,)),
                        pltpu.SemaphoreType.DMA((2 + HIST * N_SLABS,)),
                        pltpu.SemaphoreType.DMA((3 + HIST * N_SLABS,))],
        compiler_params=pltpu.CompilerParams(dimension_semantics=("arbitrary",),
                                             vmem_limit_bytes=VMEM_LIMIT_BYTES),
        name="hybrid_layer",
    )(x_rows, x_rows, xs_rows, cs_steps, h_state, w_in, w_out, *params)
    seq_major = lambda c: jnp.transpose(c, (1, 0, 2))
    return (y_rows.reshape(x_prompt.shape), ys_rows.reshape(x_sample.shape), seq_major(conv_p), h_p,
            seq_major(conv_s), h_s, vs_rows.reshape(dec_b, dec_t, WIDTH))


def kernel(x_prompt, x_sample, state_rglru_conv, state_rglru_h, pre_norm_g, post_norm_g, w_in, conv_w, conv_b,
           w_rgate, b_rgate, w_igate, b_igate, lru_lambda, sgu_norm_g, w_spatial, b_spatial, w_out):
    depth = w_in.shape[0]
    yp, ys = x_prompt, x_sample
    conv_p, h_p, conv_s, h_s, v_s = [], [], [], [], []
    for l in range(depth):
        yp, ys, cp, hp, cs, hs, vs = _layer(
            yp, ys, state_rglru_conv[l], state_rglru_h[l], pre_norm_g[l], post_norm_g[l], w_in[l], conv_w[l],
            conv_b[l], w_rgate[l], b_rgate[l], w_igate[l], b_igate[l], lru_lambda[l], sgu_norm_g[l], w_spatial[l],
            b_spatial[l], w_out[l])
        conv_p.append(cp)
        h_p.append(hp)
        conv_s.append(cs)
        h_s.append(hs)
        v_s.append(vs)
    return (yp, ys, jnp.stack(conv_p), jnp.stack(h_p), jnp.stack(conv_s), jnp.stack(h_s), jnp.stack(v_s))
```

```python
import collections
import functools
import math

import jax
import jax.numpy as jnp
from jax import lax
from jax.experimental import pallas as pl
from jax.experimental.pallas import tpu as pltpu

F32 = jnp.float32
BF16 = jnp.bfloat16

EPS = 1e-6
LRU_C = 8.0
CONV_WIDTH = 4
HIST = CONV_WIDTH - 1
N_HEADS = 8
HEAD_DIM = 128
CHUNK = 128
LANES = 128
SUBLANES = 8
N_SLABS = 8
MXU_DIM = 256

D_MODEL = 2048
WIDTH = 1024
TM = 256
XR, GR, U, V, GS = range(5)
N_PRE = 4
N_STAGE = N_PRE + 2 * (D_MODEL // WIDTH)
VMEM_LIMIT_BYTES = 63 * 1024 * 1024

Geometry = collections.namedtuple("Geometry", "n_seg seg_len seg_pitch step_pitch")
PROMPT = Geometry(n_seg=SUBLANES, seg_len=TM // SUBLANES, seg_pitch=TM // SUBLANES + SUBLANES, step_pitch=SUBLANES)
DEC_NB = 32


def _decode_geometry(n_steps):
    return Geometry(n_seg=TM // n_steps, seg_len=n_steps, seg_pitch=n_steps, step_pitch=TM // n_steps + SUBLANES)


def _sigmoid(x):
    return 0.5 * jnp.tanh(0.5 * x) + 0.5


def _silu(x):
    return x * _sigmoid(x)


def _gelu(x):
    c = math.sqrt(2.0 / math.pi)
    return x * (0.5 * (1.0 + jnp.tanh(c * (x + 0.044715 * (x * x * x)))))


def _rms_norm(x, g):
    return x * lax.rsqrt(jnp.mean(x * x, axis=-1, keepdims=True) + EPS) * g


def _pre_norm(x_ref, g_ref, z_scr):
    x = x_ref[...]
    inv = lax.rsqrt(jnp.mean(x * x, axis=-1, keepdims=True) + EPS)
    for kb in range(D_MODEL // MXU_DIM):
        cs = slice(kb * MXU_DIM, (kb + 1) * MXU_DIM)
        z_scr[:, cs] = (x[:, cs] * inv * g_ref[:, cs]).astype(BF16)


def _to_step_order(xr, xr_scr, xp_scr, geo):
    R, L, P = geo.n_seg, geo.seg_len, geo.seg_pitch
    if R == SUBLANES:
        for s in range(R):
            for m in range(L // SUBLANES):
                for c in range(N_SLABS):
                    xp_scr[c, pl.ds(HIST * R + m * SUBLANES * R + s, SUBLANES, stride=R), :] = (
                        xr[s * L + m * SUBLANES:s * L + (m + 1) * SUBLANES, c * LANES:(c + 1) * LANES])
        return
    for s in range(R):
        for c in range(N_SLABS):
            xr_scr[c, s * P:s * P + L, :] = xr[s * L:(s + 1) * L, c * LANES:(c + 1) * LANES]
    for k in range(L):
        for g in range(R // SUBLANES):
            row = HIST * R + k * R + g * SUBLANES
            for c in range(N_SLABS):
                xp_scr[c, row:row + SUBLANES, :] = xr_scr[c, pl.ds(g * SUBLANES * P + k, SUBLANES, stride=P), :]


def _to_natural_order(hp_scr, dst_scr, geo):
    R, L, P = geo.n_seg, geo.seg_len, geo.step_pitch
    for s in range(R):
        for m in range(L // SUBLANES):
            row = s * L + m * SUBLANES
            for c in range(N_SLABS):
                dst_scr[row:row + SUBLANES, c * LANES:(c + 1) * LANES] = (
                    hp_scr[c, pl.ds(m * SUBLANES * P + s, SUBLANES, stride=P), :])


def _lru_coeffs(xp_scr, a_scr, b_scr, gate_scr, conv_w, conv_b, wg, b_r, b_i, lam, rows_per_step, reset_rows):
    R = rows_per_step
    for h in range(N_HEADS):
        ls = slice(h * HEAD_DIM, (h + 1) * HEAD_DIM)
        xc = conv_b[:, ls] + xp_scr[h, 0:TM, :] * conv_w[0:1, ls]
        for k in range(1, CONV_WIDTH):
            xc = xc + xp_scr[h, k * R:k * R + TM, :] * conv_w[k:k + 1, ls]
        b_scr[:, ls] = xc
    for h in range(N_HEADS):
        ls = slice(h * HEAD_DIM, (h + 1) * HEAD_DIM)
        gate_scr[h // 4][:, (h % 4) * 2 * HEAD_DIM:(h % 4 + 1) * 2 * HEAD_DIM] = jnp.dot(
            b_scr[:, ls].astype(BF16), wg[h], preferred_element_type=F32)
    for h in range(N_HEADS):
        ls = slice(h * HEAD_DIM, (h + 1) * HEAD_DIM)
        xc = b_scr[:, ls]
        g = gate_scr[h // 4][:, (h % 4) * 2 * HEAD_DIM:(h % 4 + 1) * 2 * HEAD_DIM]
        r = _sigmoid(g[:, :HEAD_DIM] + b_r[:, ls])
        i = _sigmoid(g[:, HEAD_DIM:] + b_i[:, ls])
        lam_h = lam[:, ls]
        softplus_neg = jnp.maximum(-lam_h, 0.0) + jnp.log1p(jnp.exp(-jnp.abs(lam_h)))
        log_a = r * (-LRU_C * softplus_neg)
        a = jnp.exp(log_a)
        mult = jnp.sqrt(-jnp.tanh(log_a) * (a * a + 1.0))
        ix = i * xc
        b = mult * ix
        a_scr[:, ls] = a
        b_scr[:, ls] = b
        if reset_rows is not None:
            a_scr[0:SUBLANES, ls] = jnp.where(reset_rows, 0.0, a[0:SUBLANES])
            b_scr[0:SUBLANES, ls] = jnp.where(reset_rows, ix[0:SUBLANES], b[0:SUBLANES])


def _sgu_head_fn(pre_scr, w_mix, bias_rows, cat, period=CHUNK):
    row = lax.broadcasted_iota(jnp.int32, (CHUNK, CHUNK), 0)
    col = lax.broadcasted_iota(jnp.int32, (CHUNK, CHUNK), 1)
    keep = row >= col
    if period < CHUNK:
        shift = period.bit_length() - 1
        keep = keep & ((row >> shift) == (col >> shift))
    blocks = [slice(ch * CHUNK, (ch + 1) * CHUNK) for ch in range(TM // CHUNK)]

    def head(hd):
        ls = slice(hd * HEAD_DIM, (hd + 1) * HEAD_DIM)
        vs = slice(WIDTH + hd * HEAD_DIM, WIDTH + (hd + 1) * HEAD_DIM)
        w_blk = w_mix[hd, 0:period, :]
        sh = period
        if sh < CHUNK:
            w_blk = jnp.where(col[0:period, :] < period, w_blk, 0.0)
        while sh < CHUNK:
            w_blk = w_blk + pltpu.roll(w_blk, sh, 1)
            sh *= 2
        w_h = jnp.where(keep, jnp.tile(w_blk, (CHUNK // period, 1)), 0.0).astype(BF16)
        bias = jnp.tile(bias_rows[0:period, ls], (CHUNK // period, 1))
        mixed = jnp.dot(w_h, jnp.concatenate([cat[rs, vs] for rs in blocks], axis=1), preferred_element_type=F32)
        for ch, rs in enumerate(blocks):
            s = mixed[:, ch * HEAD_DIM:(ch + 1) * HEAD_DIM] + bias
            cat[rs, vs] = (_gelu(pre_scr[U - 1, rs, ls]) * s * _silu(pre_scr[GS - 1, rs, ls])).astype(BF16)

    return head


def _load_weight(w_hbm, w_scr, stages, sem):
    n_stage = len(stages)
    n_row = w_hbm.shape[0] // TM
    n_chunks = w_scr.shape[0] * n_row
    assert n_chunks % n_stage == 0

    def aligned(v, m):
        return v if isinstance(v, int) else pl.multiple_of(v, m)

    def copy(k, slot):
        rows = pl.ds(aligned((k % n_row) * TM, TM), TM)
        cols = pl.ds(aligned((k // n_row) * WIDTH, WIDTH), WIDTH)
        return pltpu.make_async_copy(w_hbm.at[rows, cols], stages[slot], sem.at[slot])

    for k in range(n_stage):
        copy(k, k).start()

    def body(r, carry):
        for slot in range(n_stage):
            k = r * n_stage + slot
            copy(k, slot).wait()
            w_scr[k // n_row, pl.ds(aligned((k % n_row) * TM, TM), TM), :] = stages[slot][...].astype(BF16)

            @pl.when(k + n_stage < n_chunks)
            def _():
                copy(k + n_stage, slot).start()
        return carry

    lax.fori_loop(0, n_chunks // n_stage, body, 0)


def _layer_kernel(tiles_per_seq, n_tiles, n_dec, dec_t,
                  x_ref, xprev_ref, xs_hbm, cs_hbm, hs_hbm, w_in_hbm, w_out_hbm, wr_hbm, wi_hbm,
                  pre_g, post_g, conv_w, conv_b, b_r, b_i, lam, sgu_g, w_s, bs_rows,
                  y_ref, conv_out, h_out, ys_hbm, convs_hbm, hs_out_hbm, vs_hbm,
                  w_in, w_out, wg, z_scr, xr_nat, xr_scr, xp_scr, a_scr, b_scr, pre_scr, out_scr, cat_scr,
                  carry_h, carry_x, xr_tail, xs_buf, h0_buf, hn_buf, w_sem, in_sem, out_sem):
    hp_scr = xr_scr
    vs_buf = b_scr
    i = pl.program_id(0)
    t = i % tiles_per_seq
    cat_prev = cat_scr.at[(i + 1) % 2]
    cat_cur = cat_scr.at[i % 2]
    z_cur = z_scr.at[i % 2]
    z_next = z_scr.at[(i + 1) % 2]

    sq_sums = []

    def out_piece(c):
        piece = jnp.dot(cat_prev[...], w_out[c], preferred_element_type=F32)
        out_scr[:, c * WIDTH:(c + 1) * WIDTH] = piece
        sq_sums.append(jnp.sum(piece * piece, axis=-1, keepdims=True))

    def in_piece(slab):
        pre_scr[slab - 1] = jnp.dot(z_cur[...], w_in[slab], preferred_element_type=F32)

    def finish_prev():
        inv = lax.rsqrt(functools.reduce(lambda p, q: p + q, sq_sums) * (1.0 / D_MODEL) + EPS)
        sq_sums.clear()
        y_ref[...] = xprev_ref[...] + out_scr[...] * inv * post_g[...]

    @pl.when(i == 0)
    def _():
        halves = lambda ref: [ref.at[:, pl.ds(c * WIDTH, WIDTH)] for c in range(D_MODEL // WIDTH)]
        stages = [pre_scr.at[s] for s in range(pre_scr.shape[0])] + halves(out_scr) + halves(xs_buf)
        gates_in = [pltpu.make_async_copy(src, xr_scr.at[:, pl.ds(k * HEAD_DIM, HEAD_DIM), :], in_sem.at[k])
                    for k, src in enumerate((wr_hbm, wi_hbm))]
        for cp in gates_in:
            cp.start()
        _load_weight(w_in_hbm, w_in, stages, w_sem)
        _load_weight(w_out_hbm, w_out, stages, w_sem)
        for k, cp in enumerate(gates_in):
            cp.wait()
            wg[:, :, k * HEAD_DIM:(k + 1) * HEAD_DIM] = xr_scr[:, k * HEAD_DIM:(k + 1) * HEAD_DIM, :].astype(BF16)
        cat_scr[1] = jnp.zeros(cat_scr.shape[1:], BF16)
        _pre_norm(xprev_ref, pre_g, z_cur)
        xr_nat[...] = jnp.dot(z_cur[...], w_in[XR], preferred_element_type=F32)

    @pl.when(t == 0)
    def _():
        carry_h[...] = jnp.zeros_like(carry_h)
        carry_x[...] = jnp.zeros_like(carry_x)

    @pl.when(i == n_tiles)
    def _():
        x_in, h_in, cs_in = _decode_copies_in(0, _decode_geometry(dec_t), xs_hbm, cs_hbm, hs_hbm, xs_buf, xp_scr, h0_buf,
                                              in_sem)
        for cp in [x_in, h_in] + cs_in:
            cp.start()
        for c in range(D_MODEL // WIDTH):
            out_piece(c)
        finish_prev()

    @pl.when(i > n_tiles)
    def _():
        _decode_tile(i - n_tiles - 1, n_dec, _decode_geometry(dec_t), xs_hbm, cs_hbm, hs_hbm, pre_g, post_g, conv_w, conv_b,
                     wg, b_r, b_i, lam, sgu_g, w_s, bs_rows, ys_hbm, convs_hbm, hs_out_hbm, vs_hbm, w_in, w_out, z_scr.at[0],
                     xr_scr, xp_scr, a_scr, b_scr, hp_scr, pre_scr, out_scr, cat_scr.at[0], xs_buf, vs_buf,
                     h0_buf, hn_buf, in_sem, out_sem)

    @pl.when(i < n_tiles)
    def _():
        geo = PROMPT
        R = geo.n_seg

        xr = xr_nat[...]
        xr_tail[...] = xr[TM - SUBLANES:TM, :]
        _to_step_order(xr, xr_scr, xp_scr, geo)
        sub = lax.broadcasted_iota(jnp.int32, (R, WIDTH), 0)
        for m in range(HIST):
            src = HIST * R + (geo.seg_len - HIST + m) * R
            rs = slice(m * R, (m + 1) * R)
            for c in range(N_SLABS):
                ls = slice(c * LANES, (c + 1) * LANES)
                rolled = pltpu.roll(xp_scr[c, src:src + R, :], 1, 0)
                xp_scr[c, rs, :] = jnp.where(sub[:, 0:LANES] == 0, carry_x[rs, ls], rolled)
                carry_x[rs, ls] = rolled

        reset_rows = (lax.broadcasted_iota(jnp.int32, (SUBLANES, HEAD_DIM), 0) == 0) & (t == 0)
        _lru_coeffs(xp_scr, a_scr, b_scr, (pre_scr.at[U - 1], pre_scr.at[GS - 1]), conv_w, conv_b, wg, b_r, b_i, lam,
                    R, reset_rows)

        in_piece(GR)
        hl = jnp.zeros((R, WIDTH), F32)
        pr = jnp.ones((R, WIDTH), F32)
        for k in range(geo.seg_len):
            a_k = a_scr[k * R:(k + 1) * R, :]
            hl = a_k * hl + b_scr[k * R:(k + 1) * R, :]
            pr = a_k * pr
        c_in = carry_h[0:1, :]
        h0 = jnp.zeros((R, WIDTH), F32)
        for j in range(R):
            h0 = jnp.where(sub == j, c_in, h0)
            c_in = hl[j:j + 1, :] + pr[j:j + 1, :] * c_in
        carry_h[0:1, :] = c_in
        h = h0
        for k in range(geo.seg_len):
            h = a_scr[k * R:(k + 1) * R, :] * h + b_scr[k * R:(k + 1) * R, :]
            for c in range(N_SLABS):
                hp_scr[c, pl.ds(k, R, stride=geo.seg_pitch), :] = h[:, c * LANES:(c + 1) * LANES]
        in_piece(V)
        for j in range(R):
            rs = slice(j * geo.seg_len, (j + 1) * geo.seg_len)
            for c in range(N_SLABS):
                ls = slice(c * LANES, (c + 1) * LANES)
                h_nat = hp_scr[c, j * geo.seg_pitch:j * geo.seg_pitch + geo.seg_len, :]
                cat_cur[rs, ls] = (h_nat * _silu(pre_scr[GR - 1, rs, ls])).astype(BF16)

        in_piece(U)
        in_piece(GS)
        cat_cur[:, WIDTH:2 * WIDTH] = _rms_norm(_gelu(pre_scr[V - 1]), sgu_g[...]).astype(BF16)
        sgu_head = _sgu_head_fn(pre_scr, w_s, bs_rows, cat_cur)
        for hd in range(N_HEADS):
            if hd % (N_HEADS // (D_MODEL // WIDTH)) == 0:
                out_piece(hd // (N_HEADS // (D_MODEL // WIDTH)))
            sgu_head(hd)
        _pre_norm(x_ref, pre_g, z_next)
        xr_nat[...] = jnp.dot(z_next[...], w_in[XR], preferred_element_type=F32)
        finish_prev()

    @pl.when((i < n_tiles) & (t == tiles_per_seq - 1))
    def _():
        seq = i // tiles_per_seq
        for m in range(HIST):
            conv_out[m, pl.ds(seq, 1), :] = xr_tail[SUBLANES - HIST + m:SUBLANES - HIST + m + 1, :]
        h_out[pl.ds(seq, 1), :] = carry_h[0:1, :]


def _decode_copies_in(d, geo, xs_hbm, cs_hbm, hs_hbm, xs_buf, xp_scr, h0_buf, in_sem):
    R = geo.n_seg
    seqs = pl.ds(pl.multiple_of(d * R, R), R)
    x_in = pltpu.make_async_copy(xs_hbm.at[pl.ds(pl.multiple_of(d * TM, TM), TM), :], xs_buf, in_sem.at[0])
    h_in = pltpu.make_async_copy(hs_hbm.at[seqs, :], h0_buf, in_sem.at[1])
    cs_in = [pltpu.make_async_copy(cs_hbm.at[m, seqs, pl.ds(c * LANES, LANES)], xp_scr.at[c, pl.ds(m * R, R), :],
                                   in_sem.at[2 + m * N_SLABS + c])
             for m in range(HIST) for c in range(N_SLABS)]
    return x_in, h_in, cs_in


def _decode_tile(d, n_dec, geo, xs_hbm, cs_hbm, hs_hbm, pre_g, post_g, conv_w, conv_b, wg, b_r, b_i, lam, sgu_g,
                 w_mix, bias_rows, ys_hbm, convs_hbm, hs_out_hbm, vs_hbm, w_in, w_out, z_scr, xr_scr, xp_scr, a_scr, b_scr,
                 hp_scr, pre_scr, out_scr, cat, xs_buf, vs_buf, h0_buf, hn_buf, in_sem, out_sem):
    R, L = geo.n_seg, geo.seg_len
    tile_rows = lambda dd: pl.ds(pl.multiple_of(dd * TM, TM), TM)
    tile_seqs = lambda dd: pl.ds(pl.multiple_of(dd * R, R), R)

    def copies_out(dd):
        main = [pltpu.make_async_copy(out_scr, ys_hbm.at[tile_rows(dd), :], out_sem.at[0]),
                pltpu.make_async_copy(vs_buf, vs_hbm.at[tile_rows(dd), :], out_sem.at[1]),
                pltpu.make_async_copy(hn_buf, hs_out_hbm.at[tile_seqs(dd), :], out_sem.at[2])]
        conv = [pltpu.make_async_copy(xp_scr.at[c, pl.ds((L + m) * R, R), :],
                                      convs_hbm.at[m, tile_seqs(dd), pl.ds(c * LANES, LANES)],
                                      out_sem.at[3 + m * N_SLABS + c])
                for m in range(HIST) for c in range(N_SLABS)]
        return main + conv

    x_in, h_in, cs_in = _decode_copies_in(d, geo, xs_hbm, cs_hbm, hs_hbm, xs_buf, xp_scr, h0_buf, in_sem)
    x_next, h_next, cs_next = _decode_copies_in(d + 1, geo, xs_hbm, cs_hbm, hs_hbm, xs_buf, xp_scr, h0_buf, in_sem)
    has_next = d + 1 < n_dec
    for cp in [x_in, h_in] + cs_in:
        cp.wait()

    @pl.when(d > 0)
    def _():
        for cp in copies_out(d - 1):
            cp.wait()

    _pre_norm(xs_buf, pre_g, z_scr)

    xr = jnp.dot(z_scr[...], w_in[XR], preferred_element_type=F32)
    _to_step_order(xr, xr_scr, xp_scr, geo)
    _lru_coeffs(xp_scr, a_scr, b_scr, (pre_scr.at[U - 1], pre_scr.at[GS - 1]), conv_w, conv_b, wg, b_r, b_i, lam, R,
                None)

    @pl.when(has_next)
    def _():
        for cp in cs_next:
            cp.start()

    for slab in (GR, U, V, GS):
        pre_scr[slab - 1] = jnp.dot(z_scr[...], w_in[slab], preferred_element_type=F32)
    for c in range(N_SLABS):
        ls = slice(c * LANES, (c + 1) * LANES)
        h = h0_buf[:, ls]
        for s in range(L):
            h = a_scr[s * R:(s + 1) * R, ls] * h + b_scr[s * R:(s + 1) * R, ls]
            hp_scr[c, s * geo.step_pitch:s * geo.step_pitch + R, :] = h
        hn_buf[:, ls] = h

    @pl.when(has_next)
    def _():
        h_next.start()

    _to_natural_order(hp_scr, a_scr, geo)
    cat[:, 0:WIDTH] = (a_scr[...] * _silu(pre_scr[GR - 1])).astype(BF16)

    vs_buf[...] = _rms_norm(_gelu(pre_scr[V - 1]), sgu_g[...])
    cat[:, WIDTH:2 * WIDTH] = vs_buf[...].astype(BF16)
    sgu_head = _sgu_head_fn(pre_scr, w_mix, bias_rows, cat, period=L)
    for hd in range(N_HEADS):
        sgu_head(hd)

    for c in range(D_MODEL // WIDTH):
        out_scr[:, c * WIDTH:(c + 1) * WIDTH] = jnp.dot(cat[...], w_out[c], preferred_element_type=F32)
    out_scr[...] = xs_buf[...] + _rms_norm(out_scr[...], post_g[...])

    for cp in copies_out(d):
        cp.start()

    @pl.when(has_next)
    def _():
        x_next.start()

    @pl.when(d == n_dec - 1)
    def _():
        for cp in copies_out(d):
            cp.wait()


def _full(shape):
    return pl.BlockSpec(shape, lambda *_: (0,) * len(shape))


def _layer(x_prompt, x_sample, conv_state, h_state, pre_g, post_g, w_in, conv_w, conv_b, w_r, b_r, w_i, b_i, lam,
           sgu_g, w_s, b_s, w_out):
    n_seq, seq_len, d_model = x_prompt.shape
    dec_b, dec_t, _ = x_sample.shape
    assert d_model == D_MODEL and conv_w.shape == (CONV_WIDTH, WIDTH) and w_in.shape == (D_MODEL, 5 * WIDTH)
    assert seq_len % TM == 0 and dec_t * DEC_NB == TM and dec_b % DEC_NB == 0
    assert dec_t % SUBLANES == 0 and CHUNK % dec_t == 0 and HIST <= dec_t and dec_t & (dec_t - 1) == 0

    assert w_r.shape == w_i.shape == (N_SLABS, HEAD_DIM, HEAD_DIM)
    row = lambda p: p.reshape(1, -1)
    bs_prompt = jnp.repeat(b_s.T, HEAD_DIM, axis=1)
    params = (row(pre_g), row(post_g), conv_w, row(conv_b), row(b_r), row(b_i), row(lam), row(sgu_g), w_s, bs_prompt)

    tiles = seq_len // TM
    n_tiles = n_seq * tiles
    n_dec = dec_b // DEC_NB
    x_rows = x_prompt.reshape(n_seq * seq_len, d_model)
    xs_rows = x_sample.reshape(dec_b * dec_t, d_model)
    cs_steps = jnp.transpose(conv_state, (1, 0, 2))
    next_of = lambda i: jnp.minimum(i + 1, n_tiles - 1)
    prev_of = lambda i: jnp.clip(i - 1, 0, n_tiles - 1)
    any_spec = pl.BlockSpec(memory_space=pl.ANY)
    dec_geo = _decode_geometry(dec_t)
    relayout_rows = max(g.n_seg * g.seg_pitch for g in (PROMPT, dec_geo))
    assert relayout_rows >= max(g.seg_len * g.step_pitch for g in (PROMPT, dec_geo))
    y_rows, conv_p, h_p, ys_rows, conv_s, h_s, vs_rows = pl.pallas_call(
        functools.partial(_layer_kernel, tiles, n_tiles, n_dec, dec_t),
        grid=(n_tiles + 1 + n_dec,),
        in_specs=[pl.BlockSpec((TM, d_model), lambda i: (next_of(i), 0)),
                  pl.BlockSpec((TM, d_model), lambda i: (prev_of(i), 0)),
                  any_spec, any_spec, any_spec, any_spec, any_spec, any_spec, any_spec]
                 + [_full(p.shape) for p in params],
        out_specs=[pl.BlockSpec((TM, d_model), lambda i: (prev_of(i), 0)),
                   _full((HIST, n_seq, WIDTH)), _full((n_seq, WIDTH)),
                   any_spec, any_spec, any_spec, any_spec],
        out_shape=[jax.ShapeDtypeStruct(x_rows.shape, F32),
                   jax.ShapeDtypeStruct((HIST, n_seq, WIDTH), F32),
                   jax.ShapeDtypeStruct((n_seq, WIDTH), F32),
                   jax.ShapeDtypeStruct(xs_rows.shape, F32),
                   jax.ShapeDtypeStruct(cs_steps.shape, F32),
                   jax.ShapeDtypeStruct(h_state.shape, F32),
                   jax.ShapeDtypeStruct((dec_b * dec_t, WIDTH), F32)],
        scratch_shapes=[pltpu.VMEM((5, D_MODEL, WIDTH), BF16),
                        pltpu.VMEM((D_MODEL // WIDTH, D_MODEL, WIDTH), BF16),
                        pltpu.VMEM((N_HEADS, HEAD_DIM, 2 * HEAD_DIM), BF16),
                        pltpu.VMEM((2, TM, D_MODEL), BF16),
                        pltpu.VMEM((TM, WIDTH), F32),
                        pltpu.VMEM((N_SLABS, relayout_rows, LANES), F32),
                        pltpu.VMEM((N_SLABS, TM + HIST * DEC_NB, LANES), F32),
                        pltpu.VMEM((TM, WIDTH), F32), pltpu.VMEM((TM, WIDTH), F32),
                        pltpu.VMEM((N_PRE, TM, WIDTH), F32),
                        pltpu.VMEM((TM, D_MODEL), F32),
                        pltpu.VMEM((2, TM, D_MODEL), BF16),
                        pltpu.VMEM((SUBLANES, WIDTH), F32),
                        pltpu.VMEM((HIST * SUBLANES, WIDTH), F32),
                        pltpu.VMEM((SUBLANES, WIDTH), F32),
                        pltpu.VMEM((TM, D_MODEL), F32),
                        pltpu.VMEM((DEC_NB, WIDTH), F32), pltpu.VMEM((DEC_NB, WIDTH), F32),
                        pltpu.SemaphoreType.DMA((N_STAGE,)),
                        pltpu.SemaphoreType.DMA((2 + HIST * N_SLABS,)),
                        pltpu.SemaphoreType.DMA((3 + HIST * N_SLABS,))],
        compiler_params=pltpu.CompilerParams(dimension_semantics=("arbitrary",),
                                             vmem_limit_bytes=VMEM_LIMIT_BYTES),
        name="hybrid_layer",
    )(x_rows, x_rows, xs_rows, cs_steps, h_state, w_in, w_out, w_r, w_i, *params)
    seq_major = lambda c: jnp.transpose(c, (1, 0, 2))
    return (y_rows.reshape(x_prompt.shape), ys_rows.reshape(x_sample.shape), seq_major(conv_p), h_p,
            seq_major(conv_s), h_s, vs_rows.reshape(dec_b, dec_t, WIDTH))


def kernel(x_prompt, x_sample, state_rglru_conv, state_rglru_h, pre_norm_g, post_norm_g, w_in, conv_w, conv_b,
           w_rgate, b_rgate, w_igate, b_igate, lru_lambda, sgu_norm_g, w_spatial, b_spatial, w_out):
    depth = w_in.shape[0]
    yp, ys = x_prompt, x_sample
    conv_p, h_p, conv_s, h_s, v_s = [], [], [], [], []
    for l in range(depth):
        yp, ys, cp, hp, cs, hs, vs = _layer(
            yp, ys, state_rglru_conv[l], state_rglru_h[l], pre_norm_g[l], post_norm_g[l], w_in[l], conv_w[l],
            conv_b[l], w_rgate[l], b_rgate[l], w_igate[l], b_igate[l], lru_lambda[l], sgu_norm_g[l], w_spatial[l],
            b_spatial[l], w_out[l])
        conv_p.append(cp)
        h_p.append(hp)
        conv_s.append(cs)
        h_s.append(hs)
        v_s.append(vs)
    return (yp, ys, jnp.stack(conv_p), jnp.stack(h_p), jnp.stack(conv_s), jnp.stack(h_s), jnp.stack(v_s))
```

```python
import collections
import functools
import math

import jax
import jax.numpy as jnp
from jax import lax
from jax.experimental import pallas as pl
from jax.experimental.pallas import tpu as pltpu

F32 = jnp.float32
BF16 = jnp.bfloat16

EPS = 1e-6
LRU_C = 8.0
CONV_WIDTH = 4
HIST = CONV_WIDTH - 1
N_HEADS = 8
HEAD_DIM = 128
CHUNK = 128
LANES = 128
SUBLANES = 8
N_SLABS = 8
MXU_DIM = 256

D_MODEL = 2048
WIDTH = 1024
TM = 256
XR, GR, U, V, GS = range(5)
N_PRE = 4
N_STAGE = N_PRE + 2 * (D_MODEL // WIDTH)
VMEM_LIMIT_BYTES = 63 * 1024 * 1024

Geometry = collections.namedtuple("Geometry", "n_seg seg_len seg_pitch step_pitch")
PROMPT = Geometry(n_seg=SUBLANES, seg_len=TM // SUBLANES, seg_pitch=TM // SUBLANES + SUBLANES, step_pitch=SUBLANES)
DEC_NB = 32


def _decode_geometry(n_steps):
    return Geometry(n_seg=TM // n_steps, seg_len=n_steps, seg_pitch=n_steps, step_pitch=TM // n_steps + SUBLANES)


def _sigmoid(x):
    return 0.5 * jnp.tanh(0.5 * x) + 0.5


def _silu(x):
    return x * _sigmoid(x)


def _gelu(x):
    c = math.sqrt(2.0 / math.pi)
    return x * (0.5 * (1.0 + jnp.tanh(c * (x + 0.044715 * (x * x * x)))))


def _rms_norm(x, g):
    return x * lax.rsqrt(jnp.mean(x * x, axis=-1, keepdims=True) + EPS) * g


def _pre_norm(x_ref, g_ref, z_scr):
    x = x_ref[...]
    inv = lax.rsqrt(jnp.mean(x * x, axis=-1, keepdims=True) + EPS)
    for kb in range(D_MODEL // MXU_DIM):
        cs = slice(kb * MXU_DIM, (kb + 1) * MXU_DIM)
        z_scr[:, cs] = (x[:, cs] * inv * g_ref[:, cs]).astype(BF16)


def _to_step_order(xr, xr_scr, xp_scr, geo):
    R, L, P = geo.n_seg, geo.seg_len, geo.seg_pitch
    if R == SUBLANES:
        for s in range(R):
            for m in range(L // SUBLANES):
                for c in range(N_SLABS):
                    xp_scr[c, pl.ds(HIST * R + m * SUBLANES * R + s, SUBLANES, stride=R), :] = (
                        xr[s * L + m * SUBLANES:s * L + (m + 1) * SUBLANES, c * LANES:(c + 1) * LANES])
        return
    for s in range(R):
        for c in range(N_SLABS):
            xr_scr[c, s * P:s * P + L, :] = xr[s * L:(s + 1) * L, c * LANES:(c + 1) * LANES]
    for k in range(L):
        for g in range(R // SUBLANES):
            row = HIST * R + k * R + g * SUBLANES
            for c in range(N_SLABS):
                xp_scr[c, row:row + SUBLANES, :] = xr_scr[c, pl.ds(g * SUBLANES * P + k, SUBLANES, stride=P), :]


def _to_natural_order(hp_scr, dst_scr, geo):
    R, L, P = geo.n_seg, geo.seg_len, geo.step_pitch
    for s in range(R):
        for m in range(L // SUBLANES):
            row = s * L + m * SUBLANES
            for c in range(N_SLABS):
                dst_scr[row:row + SUBLANES, c * LANES:(c + 1) * LANES] = (
                    hp_scr[c, pl.ds(m * SUBLANES * P + s, SUBLANES, stride=P), :])


def _lru_coeffs(xp_scr, a_scr, b_scr, gate_scr, conv_w, conv_b, wg, b_r, b_i, lam, rows_per_step, reset_rows):
    R = rows_per_step
    for h in range(N_HEADS):
        ls = slice(h * HEAD_DIM, (h + 1) * HEAD_DIM)
        xc = conv_b[:, ls] + xp_scr[h, 0:TM, :] * conv_w[0:1, ls]
        for k in range(1, CONV_WIDTH):
            xc = xc + xp_scr[h, k * R:k * R + TM, :] * conv_w[k:k + 1, ls]
        b_scr[:, ls] = xc
    for h in range(N_HEADS):
        ls = slice(h * HEAD_DIM, (h + 1) * HEAD_DIM)
        gate_scr[h // 4][:, (h % 4) * 2 * HEAD_DIM:(h % 4 + 1) * 2 * HEAD_DIM] = jnp.dot(
            b_scr[:, ls].astype(BF16), wg[h], preferred_element_type=F32)
    for h in range(N_HEADS):
        ls = slice(h * HEAD_DIM, (h + 1) * HEAD_DIM)
        xc = b_scr[:, ls]
        g = gate_scr[h // 4][:, (h % 4) * 2 * HEAD_DIM:(h % 4 + 1) * 2 * HEAD_DIM]
        r = _sigmoid(g[:, :HEAD_DIM] + b_r[:, ls])
        i = _sigmoid(g[:, HEAD_DIM:] + b_i[:, ls])
        lam_h = lam[:, ls]
        softplus_neg = jnp.maximum(-lam_h, 0.0) + jnp.log1p(jnp.exp(-jnp.abs(lam_h)))
        log_a = r * (-LRU_C * softplus_neg)
        a = jnp.exp(log_a)
        mult = jnp.sqrt(-jnp.tanh(log_a) * (a * a + 1.0))
        ix = i * xc
        b = mult * ix
        a_scr[:, ls] = a
        b_scr[:, ls] = b
        if reset_rows is not None:
            a_scr[0:SUBLANES, ls] = jnp.where(reset_rows, 0.0, a[0:SUBLANES])
            b_scr[0:SUBLANES, ls] = jnp.where(reset_rows, ix[0:SUBLANES], b[0:SUBLANES])


def _sgu_head_fn(pre_scr, w_mix, bias_rows, cat, period=CHUNK):
    row = lax.broadcasted_iota(jnp.int32, (CHUNK, CHUNK), 0)
    col = lax.broadcasted_iota(jnp.int32, (CHUNK, CHUNK), 1)
    keep = row >= col
    if period < CHUNK:
        shift = period.bit_length() - 1
        keep = keep & ((row >> shift) == (col >> shift))
    blocks = [slice(ch * CHUNK, (ch + 1) * CHUNK) for ch in range(TM // CHUNK)]

    def head(hd):
        ls = slice(hd * HEAD_DIM, (hd + 1) * HEAD_DIM)
        vs = slice(WIDTH + hd * HEAD_DIM, WIDTH + (hd + 1) * HEAD_DIM)
        w_blk = w_mix[hd, 0:period, :]
        sh = period
        if sh < CHUNK:
            w_blk = jnp.where(col[0:period, :] < period, w_blk, 0.0)
        while sh < CHUNK:
            w_blk = w_blk + pltpu.roll(w_blk, sh, 1)
            sh *= 2
        w_h = jnp.where(keep, jnp.tile(w_blk, (CHUNK // period, 1)), 0.0).astype(BF16)
        bias = jnp.tile(bias_rows[0:period, ls], (CHUNK // period, 1))
        mixed = jnp.dot(w_h, jnp.concatenate([cat[rs, vs] for rs in blocks], axis=1), preferred_element_type=F32)
        for ch, rs in enumerate(blocks):
            s = mixed[:, ch * HEAD_DIM:(ch + 1) * HEAD_DIM] + bias
            cat[rs, vs] = (_gelu(pre_scr[U - 1, rs, ls]) * s * _silu(pre_scr[GS - 1, rs, ls])).astype(BF16)

    return head


def _load_weight(w_hbm, w_scr, stages, sem):
    n_stage = len(stages)
    n_row = w_hbm.shape[0] // TM
    n_chunks = w_scr.shape[0] * n_row
    assert n_chunks % n_stage == 0

    def aligned(v, m):
        return v if isinstance(v, int) else pl.multiple_of(v, m)

    def copy(k, slot):
        rows = pl.ds(aligned((k % n_row) * TM, TM), TM)
        cols = pl.ds(aligned((k // n_row) * WIDTH, WIDTH), WIDTH)
        return pltpu.make_async_copy(w_hbm.at[rows, cols], stages[slot], sem.at[slot])

    for k in range(n_stage):
        copy(k, k).start()

    def body(r, carry):
        for slot in range(n_stage):
            k = r * n_stage + slot
            copy(k, slot).wait()
            w_scr[k // n_row, pl.ds(aligned((k % n_row) * TM, TM), TM), :] = stages[slot][...].astype(BF16)

            @pl.when(k + n_stage < n_chunks)
            def _():
                copy(k + n_stage, slot).start()
        return carry

    lax.fori_loop(0, n_chunks // n_stage, body, 0)


def _layer_kernel(tiles_per_seq, n_tiles, n_dec, dec_t,
                  x_ref, xprev_ref, xs_hbm, cs_hbm, hs_hbm, w_in_hbm, w_out_hbm, wr_hbm, wi_hbm,
                  pre_g, post_g, conv_w, conv_b, b_r, b_i, lam, sgu_g, w_s, bs_rows,
                  y_ref, conv_out, h_out, ys_hbm, convs_hbm, hs_out_hbm, vs_hbm,
                  w_in, w_out, wg, z_scr, xr_scr, xp_scr, a_scr, b_scr, pre_scr, out_scr, cat_scr,
                  carry_h, carry_x, xr_tail, xs_buf, h0_buf, hn_buf, w_sem, in_sem, out_sem):
    hp_scr = xr_scr
    vs_buf = b_scr
    xr_nat = xs_buf.at[0, :, pl.ds(0, WIDTH)]
    i = pl.program_id(0)
    t = i % tiles_per_seq
    cat_prev = cat_scr.at[(i + 1) % 2]
    cat_cur = cat_scr.at[i % 2]
    z_cur = z_scr.at[i % 2]
    z_next = z_scr.at[(i + 1) % 2]

    sq_sums = []

    def out_piece(c):
        piece = jnp.dot(cat_prev[...], w_out[c], preferred_element_type=F32)
        out_scr[:, c * WIDTH:(c + 1) * WIDTH] = piece
        sq_sums.append(jnp.sum(piece * piece, axis=-1, keepdims=True))

    def in_piece(slab):
        pre_scr[slab - 1] = jnp.dot(z_cur[...], w_in[slab], preferred_element_type=F32)

    def finish_prev():
        inv = lax.rsqrt(functools.reduce(lambda p, q: p + q, sq_sums) * (1.0 / D_MODEL) + EPS)
        sq_sums.clear()
        y_ref[...] = xprev_ref[...] + out_scr[...] * inv * post_g[...]

    @pl.when(i == 0)
    def _():
        halves = lambda ref: [ref.at[:, pl.ds(c * WIDTH, WIDTH)] for c in range(D_MODEL // WIDTH)]
        stages = [pre_scr.at[s] for s in range(pre_scr.shape[0])] + halves(out_scr) + halves(xs_buf.at[0])
        gates_in = [pltpu.make_async_copy(src, xr_scr.at[:, pl.ds(k * HEAD_DIM, HEAD_DIM), :], in_sem.at[k])
                    for k, src in enumerate((wr_hbm, wi_hbm))]
        for cp in gates_in:
            cp.start()
        _load_weight(w_in_hbm, w_in, stages, w_sem)
        _load_weight(w_out_hbm, w_out, stages, w_sem)
        for k, cp in enumerate(gates_in):
            cp.wait()
            wg[:, :, k * HEAD_DIM:(k + 1) * HEAD_DIM] = xr_scr[:, k * HEAD_DIM:(k + 1) * HEAD_DIM, :].astype(BF16)
        cat_scr[1] = jnp.zeros(cat_scr.shape[1:], BF16)
        _pre_norm(xprev_ref, pre_g, z_cur)
        xr_nat[...] = jnp.dot(z_cur[...], w_in[XR], preferred_element_type=F32)

    @pl.when(t == 0)
    def _():
        carry_h[...] = jnp.zeros_like(carry_h)
        carry_x[...] = jnp.zeros_like(carry_x)

    @pl.when(i == n_tiles)
    def _():
        x_in, h_in, cs_in = _decode_copies_in(0, _decode_geometry(dec_t), xs_hbm, cs_hbm, hs_hbm, xs_buf, xp_scr, h0_buf,
                                              in_sem)
        for cp in [x_in, h_in] + cs_in:
            cp.start()
        for c in range(D_MODEL // WIDTH):
            out_piece(c)
        finish_prev()

    @pl.when(i > n_tiles)
    def _():
        _decode_tile(i - n_tiles - 1, n_dec, _decode_geometry(dec_t), xs_hbm, cs_hbm, hs_hbm, pre_g, post_g, conv_w, conv_b,
                     wg, b_r, b_i, lam, sgu_g, w_s, bs_rows, ys_hbm, convs_hbm, hs_out_hbm, vs_hbm, w_in, w_out, z_scr.at[0],
                     xr_scr, xp_scr, a_scr, b_scr, hp_scr, pre_scr, out_scr, cat_scr.at[0], xs_buf, vs_buf,
                     h0_buf, hn_buf, in_sem, out_sem)

    @pl.when(i < n_tiles)
    def _():
        geo = PROMPT
        R = geo.n_seg

        xr = xr_nat[...]
        xr_tail[...] = xr[TM - SUBLANES:TM, :]
        _to_step_order(xr, xr_scr, xp_scr, geo)
        sub = lax.broadcasted_iota(jnp.int32, (R, WIDTH), 0)
        for m in range(HIST):
            src = HIST * R + (geo.seg_len - HIST + m) * R
            rs = slice(m * R, (m + 1) * R)
            for c in range(N_SLABS):
                ls = slice(c * LANES, (c + 1) * LANES)
                rolled = pltpu.roll(xp_scr[c, src:src + R, :], 1, 0)
                xp_scr[c, rs, :] = jnp.where(sub[:, 0:LANES] == 0, carry_x[rs, ls], rolled)
                carry_x[rs, ls] = rolled

        reset_rows = (lax.broadcasted_iota(jnp.int32, (SUBLANES, HEAD_DIM), 0) == 0) & (t == 0)
        _lru_coeffs(xp_scr, a_scr, b_scr, (pre_scr.at[U - 1], pre_scr.at[GS - 1]), conv_w, conv_b, wg, b_r, b_i, lam,
                    R, reset_rows)

        in_piece(GR)
        hl = jnp.zeros((R, WIDTH), F32)
        pr = jnp.ones((R, WIDTH), F32)
        for k in range(geo.seg_len):
            a_k = a_scr[k * R:(k + 1) * R, :]
            hl = a_k * hl + b_scr[k * R:(k + 1) * R, :]
            pr = a_k * pr
        c_in = carry_h[0:1, :]
        h0 = jnp.zeros((R, WIDTH), F32)
        for j in range(R):
            h0 = jnp.where(sub == j, c_in, h0)
            c_in = hl[j:j + 1, :] + pr[j:j + 1, :] * c_in
        carry_h[0:1, :] = c_in
        h = h0
        for k in range(geo.seg_len):
            h = a_scr[k * R:(k + 1) * R, :] * h + b_scr[k * R:(k + 1) * R, :]
            for c in range(N_SLABS):
                hp_scr[c, pl.ds(k, R, stride=geo.seg_pitch), :] = h[:, c * LANES:(c + 1) * LANES]
        in_piece(V)
        for j in range(R):
            rs = slice(j * geo.seg_len, (j + 1) * geo.seg_len)
            for c in range(N_SLABS):
                ls = slice(c * LANES, (c + 1) * LANES)
                h_nat = hp_scr[c, j * geo.seg_pitch:j * geo.seg_pitch + geo.seg_len, :]
                cat_cur[rs, ls] = (h_nat * _silu(pre_scr[GR - 1, rs, ls])).astype(BF16)

        in_piece(U)
        in_piece(GS)
        cat_cur[:, WIDTH:2 * WIDTH] = _rms_norm(_gelu(pre_scr[V - 1]), sgu_g[...]).astype(BF16)
        sgu_head = _sgu_head_fn(pre_scr, w_s, bs_rows, cat_cur)
        for hd in range(N_HEADS):
            if hd % (N_HEADS // (D_MODEL // WIDTH)) == 0:
                out_piece(hd // (N_HEADS // (D_MODEL // WIDTH)))
            sgu_head(hd)
        _pre_norm(x_ref, pre_g, z_next)
        xr_nat[...] = jnp.dot(z_next[...], w_in[XR], preferred_element_type=F32)
        finish_prev()

    @pl.when((i < n_tiles) & (t == tiles_per_seq - 1))
    def _():
        seq = i // tiles_per_seq
        for m in range(HIST):
            conv_out[m, pl.ds(seq, 1), :] = xr_tail[SUBLANES - HIST + m:SUBLANES - HIST + m + 1, :]
        h_out[pl.ds(seq, 1), :] = carry_h[0:1, :]


def _decode_copies_in(d, geo, xs_hbm, cs_hbm, hs_hbm, xs_buf, xp_scr, h0_buf, in_sem):
    R = geo.n_seg
    seqs = pl.ds(pl.multiple_of(d * R, R), R)
    x_in = pltpu.make_async_copy(xs_hbm.at[pl.ds(pl.multiple_of(d * TM, TM), TM), :], xs_buf.at[d % 2], in_sem.at[0])
    h_in = pltpu.make_async_copy(hs_hbm.at[seqs, :], h0_buf, in_sem.at[1])
    cs_in = [pltpu.make_async_copy(cs_hbm.at[m, seqs, pl.ds(c * LANES, LANES)], xp_scr.at[c, pl.ds(m * R, R), :],
                                   in_sem.at[2 + m * N_SLABS + c])
             for m in range(HIST) for c in range(N_SLABS)]
    return x_in, h_in, cs_in


def _decode_tile(d, n_dec, geo, xs_hbm, cs_hbm, hs_hbm, pre_g, post_g, conv_w, conv_b, wg, b_r, b_i, lam, sgu_g,
                 w_mix, bias_rows, ys_hbm, convs_hbm, hs_out_hbm, vs_hbm, w_in, w_out, z_scr, xr_scr, xp_scr, a_scr, b_scr,
                 hp_scr, pre_scr, out_scr, cat, xs_buf, vs_buf, h0_buf, hn_buf, in_sem, out_sem):
    R, L = geo.n_seg, geo.seg_len
    tile_rows = lambda dd: pl.ds(pl.multiple_of(dd * TM, TM), TM)
    tile_seqs = lambda dd: pl.ds(pl.multiple_of(dd * R, R), R)

    def copies_out(dd):
        main = [pltpu.make_async_copy(out_scr, ys_hbm.at[tile_rows(dd), :], out_sem.at[0]),
                pltpu.make_async_copy(vs_buf, vs_hbm.at[tile_rows(dd), :], out_sem.at[1]),
                pltpu.make_async_copy(hn_buf, hs_out_hbm.at[tile_seqs(dd), :], out_sem.at[2])]
        conv = [pltpu.make_async_copy(xp_scr.at[c, pl.ds((L + m) * R, R), :],
                                      convs_hbm.at[m, tile_seqs(dd), pl.ds(c * LANES, LANES)],
                                      out_sem.at[3 + m * N_SLABS + c])
                for m in range(HIST) for c in range(N_SLABS)]
        return main + conv

    x_in, h_in, cs_in = _decode_copies_in(d, geo, xs_hbm, cs_hbm, hs_hbm, xs_buf, xp_scr, h0_buf, in_sem)
    x_next, h_next, cs_next = _decode_copies_in(d + 1, geo, xs_hbm, cs_hbm, hs_hbm, xs_buf, xp_scr, h0_buf, in_sem)
    has_next = d + 1 < n_dec
    xs_cur = xs_buf.at[d % 2]
    for cp in [x_in, h_in] + cs_in:
        cp.wait()

    @pl.when(has_next)
    def _():
        x_next.start()

    @pl.when(d > 0)
    def _():
        for cp in copies_out(d - 1)[1:]:
            cp.wait()

    _pre_norm(xs_cur, pre_g, z_scr)

    xr = jnp.dot(z_scr[...], w_in[XR], preferred_element_type=F32)
    _to_step_order(xr, xr_scr, xp_scr, geo)
    _lru_coeffs(xp_scr, a_scr, b_scr, (pre_scr.at[U - 1], pre_scr.at[GS - 1]), conv_w, conv_b, wg, b_r, b_i, lam, R,
                None)

    def in_piece(slab):
        pre_scr[slab - 1] = jnp.dot(z_scr[...], w_in[slab], preferred_element_type=F32)

    in_piece(GR)
    for c in range(N_SLABS):
        ls = slice(c * LANES, (c + 1) * LANES)
        h = h0_buf[:, ls]
        for s in range(L):
            h = a_scr[s * R:(s + 1) * R, ls] * h + b_scr[s * R:(s + 1) * R, ls]
            hp_scr[c, s * geo.step_pitch:s * geo.step_pitch + R, :] = h
        hn_buf[:, ls] = h
    in_piece(V)
    _to_natural_order(hp_scr, a_scr, geo)
    cat[:, 0:WIDTH] = (a_scr[...] * _silu(pre_scr[GR - 1])).astype(BF16)

    in_piece(U)
    in_piece(GS)
    vs_buf[...] = _rms_norm(_gelu(pre_scr[V - 1]), sgu_g[...])
    cat[:, WIDTH:2 * WIDTH] = vs_buf[...].astype(BF16)
    sgu_head = _sgu_head_fn(pre_scr, w_mix, bias_rows, cat, period=L)
    for hd in range(N_HEADS):
        sgu_head(hd)

    @pl.when(d > 0)
    def _():
        copies_out(d - 1)[0].wait()

    @pl.when(has_next)
    def _():
        for cp in [h_next] + cs_next:
            cp.start()

    for c in range(D_MODEL // WIDTH):
        out_scr[:, c * WIDTH:(c + 1) * WIDTH] = jnp.dot(cat[...], w_out[c], preferred_element_type=F32)
    out_scr[...] = xs_cur[...] + _rms_norm(out_scr[...], post_g[...])

    for cp in copies_out(d):
        cp.start()

    @pl.when(d == n_dec - 1)
    def _():
        for cp in copies_out(d):
            cp.wait()


def _full(shape):
    return pl.BlockSpec(shape, lambda *_: (0,) * len(shape))


def _layer(x_prompt, x_sample, conv_state, h_state, pre_g, post_g, w_in, conv_w, conv_b, w_r, b_r, w_i, b_i, lam,
           sgu_g, w_s, b_s, w_out):
    n_seq, seq_len, d_model = x_prompt.shape
    dec_b, dec_t, _ = x_sample.shape
    assert d_model == D_MODEL and conv_w.shape == (CONV_WIDTH, WIDTH) and w_in.shape == (D_MODEL, 5 * WIDTH)
    assert seq_len % TM == 0 and dec_t * DEC_NB == TM and dec_b % DEC_NB == 0
    assert dec_t % SUBLANES == 0 and CHUNK % dec_t == 0 and HIST <= dec_t and dec_t & (dec_t - 1) == 0

    assert w_r.shape == w_i.shape == (N_SLABS, HEAD_DIM, HEAD_DIM)
    row = lambda p: p.reshape(1, -1)
    bs_prompt = jnp.repeat(b_s.T, HEAD_DIM, axis=1)
    params = (row(pre_g), row(post_g), conv_w, row(conv_b), row(b_r), row(b_i), row(lam), row(sgu_g), w_s, bs_prompt)

    tiles = seq_len // TM
    n_tiles = n_seq * tiles
    n_dec = dec_b // DEC_NB
    x_rows = x_prompt.reshape(n_seq * seq_len, d_model)
    xs_rows = x_sample.reshape(dec_b * dec_t, d_model)
    cs_steps = jnp.transpose(conv_state, (1, 0, 2))
    next_of = lambda i: jnp.minimum(i + 1, n_tiles - 1)
    prev_of = lambda i: jnp.clip(i - 1, 0, n_tiles - 1)
    any_spec = pl.BlockSpec(memory_space=pl.ANY)
    dec_geo = _decode_geometry(dec_t)
    relayout_rows = max(g.n_seg * g.seg_pitch for g in (PROMPT, dec_geo))
    assert relayout_rows >= max(g.seg_len * g.step_pitch for g in (PROMPT, dec_geo))
    y_rows, conv_p, h_p, ys_rows, conv_s, h_s, vs_rows = pl.pallas_call(
        functools.partial(_layer_kernel, tiles, n_tiles, n_dec, dec_t),
        grid=(n_tiles + 1 + n_dec,),
        in_specs=[pl.BlockSpec((TM, d_model), lambda i: (next_of(i), 0)),
                  pl.BlockSpec((TM, d_model), lambda i: (prev_of(i), 0)),
                  any_spec, any_spec, any_spec, any_spec, any_spec, any_spec, any_spec]
                 + [_full(p.shape) for p in params],
        out_specs=[pl.BlockSpec((TM, d_model), lambda i: (prev_of(i), 0)),
                   _full((HIST, n_seq, WIDTH)), _full((n_seq, WIDTH)),
                   any_spec, any_spec, any_spec, any_spec],
        out_shape=[jax.ShapeDtypeStruct(x_rows.shape, F32),
                   jax.ShapeDtypeStruct((HIST, n_seq, WIDTH), F32),
                   jax.ShapeDtypeStruct((n_seq, WIDTH), F32),
                   jax.ShapeDtypeStruct(xs_rows.shape, F32),
                   jax.ShapeDtypeStruct(cs_steps.shape, F32),
                   jax.ShapeDtypeStruct(h_state.shape, F32),
                   jax.ShapeDtypeStruct((dec_b * dec_t, WIDTH), F32)],
        scratch_shapes=[pltpu.VMEM((5, D_MODEL, WIDTH), BF16),
                        pltpu.VMEM((D_MODEL // WIDTH, D_MODEL, WIDTH), BF16),
                        pltpu.VMEM((N_HEADS, HEAD_DIM, 2 * HEAD_DIM), BF16),
                        pltpu.VMEM((2, TM, D_MODEL), BF16),
                        pltpu.VMEM((N_SLABS, relayout_rows, LANES), F32),
                        pltpu.VMEM((N_SLABS, TM + HIST * DEC_NB, LANES), F32),
                        pltpu.VMEM((TM, WIDTH), F32), pltpu.VMEM((TM, WIDTH), F32),
                        pltpu.VMEM((N_PRE, TM, WIDTH), F32),
                        pltpu.VMEM((TM, D_MODEL), F32),
                        pltpu.VMEM((2, TM, D_MODEL), BF16),
                        pltpu.VMEM((SUBLANES, WIDTH), F32),
                        pltpu.VMEM((HIST * SUBLANES, WIDTH), F32),
                        pltpu.VMEM((SUBLANES, WIDTH), F32),
                        pltpu.VMEM((2, TM, D_MODEL), F32),
                        pltpu.VMEM((DEC_NB, WIDTH), F32), pltpu.VMEM((DEC_NB, WIDTH), F32),
                        pltpu.SemaphoreType.DMA((N_STAGE,)),
                        pltpu.SemaphoreType.DMA((2 + HIST * N_SLABS,)),
                        pltpu.SemaphoreType.DMA((3 + HIST * N_SLABS,))],
        compiler_params=pltpu.CompilerParams(dimension_semantics=("arbitrary",),
                                             vmem_limit_bytes=VMEM_LIMIT_BYTES),
        name="hybrid_layer",
    )(x_rows, x_rows, xs_rows, cs_steps, h_state, w_in, w_out, w_r, w_i, *params)
    seq_major = lambda c: jnp.transpose(c, (1, 0, 2))
    return (y_rows.reshape(x_prompt.shape), ys_rows.reshape(x_sample.shape), seq_major(conv_p), h_p,
            seq_major(conv_s), h_s, vs_rows.reshape(dec_b, dec_t, WIDTH))


def kernel(x_prompt, x_sample, state_rglru_conv, state_rglru_h, pre_norm_g, post_norm_g, w_in, conv_w, conv_b,
           w_rgate, b_rgate, w_igate, b_igate, lru_lambda, sgu_norm_g, w_spatial, b_spatial, w_out):
    depth = w_in.shape[0]
    yp, ys = x_prompt, x_sample
    conv_p, h_p, conv_s, h_s, v_s = [], [], [], [], []
    for l in range(depth):
        yp, ys, cp, hp, cs, hs, vs = _layer(
            yp, ys, state_rglru_conv[l], state_rglru_h[l], pre_norm_g[l], post_norm_g[l], w_in[l], conv_w[l],
            conv_b[l], w_rgate[l], b_rgate[l], w_igate[l], b_igate[l], lru_lambda[l], sgu_norm_g[l], w_spatial[l],
            b_spatial[l], w_out[l])
        conv_p.append(cp)
        h_p.append(hp)
        conv_s.append(cs)
        h_s.append(hs)
        v_s.append(vs)
    return (yp, ys, jnp.stack(conv_p), jnp.stack(h_p), jnp.stack(conv_s), jnp.stack(h_s), jnp.stack(v_s))
```

```python
import collections
import functools
import math

import jax
import jax.numpy as jnp
from jax import lax
from jax.experimental import pallas as pl
from jax.experimental.pallas import tpu as pltpu

F32 = jnp.float32
BF16 = jnp.bfloat16

EPS = 1e-6
LRU_C = 8.0
CONV_WIDTH = 4
HIST = CONV_WIDTH - 1
N_HEADS = 8
HEAD_DIM = 128
CHUNK = 128
LANES = 128
SUBLANES = 8
N_SLABS = 8
MXU_DIM = 256

D_MODEL = 2048
WIDTH = 1024
TM = 256
XR, GR, U, V, GS = range(5)
N_PRE = 4
N_STAGE = N_PRE + 2 * (D_MODEL // WIDTH)
VMEM_LIMIT_BYTES = 63 * 1024 * 1024

Geometry = collections.namedtuple("Geometry", "n_seg seg_len seg_pitch step_pitch")
PROMPT = Geometry(n_seg=SUBLANES, seg_len=TM // SUBLANES, seg_pitch=TM // SUBLANES + SUBLANES, step_pitch=SUBLANES)
DEC_NB = 32


def _decode_geometry(n_steps):
    return Geometry(n_seg=TM // n_steps, seg_len=n_steps, seg_pitch=n_steps, step_pitch=TM // n_steps + SUBLANES)


def _sigmoid(x):
    return 0.5 * jnp.tanh(0.5 * x) + 0.5


def _silu(x):
    return x * _sigmoid(x)


def _gelu(x):
    c = math.sqrt(2.0 / math.pi)
    return x * (0.5 * (1.0 + jnp.tanh(c * (x + 0.044715 * (x * x * x)))))


def _rms_norm(x, g):
    return x * lax.rsqrt(jnp.mean(x * x, axis=-1, keepdims=True) + EPS) * g


def _pre_norm(x_ref, g_ref, z_scr):
    x = x_ref[...]
    inv = lax.rsqrt(jnp.mean(x * x, axis=-1, keepdims=True) + EPS)
    for kb in range(D_MODEL // MXU_DIM):
        cs = slice(kb * MXU_DIM, (kb + 1) * MXU_DIM)
        z_scr[:, cs] = (x[:, cs] * inv * g_ref[:, cs]).astype(BF16)


def _to_step_order(xr, xr_scr, xp_scr, geo):
    R, L, P = geo.n_seg, geo.seg_len, geo.seg_pitch
    if R == SUBLANES:
        for s in range(R):
            for m in range(L // SUBLANES):
                for c in range(N_SLABS):
                    xp_scr[c, pl.ds(HIST * R + m * SUBLANES * R + s, SUBLANES, stride=R), :] = (
                        xr[s * L + m * SUBLANES:s * L + (m + 1) * SUBLANES, c * LANES:(c + 1) * LANES])
        return
    for s in range(R):
        for c in range(N_SLABS):
            xr_scr[c, s * P:s * P + L, :] = xr[s * L:(s + 1) * L, c * LANES:(c + 1) * LANES]
    for k in range(L):
        for g in range(R // SUBLANES):
            row = HIST * R + k * R + g * SUBLANES
            for c in range(N_SLABS):
                xp_scr[c, row:row + SUBLANES, :] = xr_scr[c, pl.ds(g * SUBLANES * P + k, SUBLANES, stride=P), :]


def _to_natural_order(hp_scr, dst_scr, geo):
    R, L, P = geo.n_seg, geo.seg_len, geo.step_pitch
    for s in range(R):
        for m in range(L // SUBLANES):
            row = s * L + m * SUBLANES
            for c in range(N_SLABS):
                dst_scr[row:row + SUBLANES, c * LANES:(c + 1) * LANES] = (
                    hp_scr[c, pl.ds(m * SUBLANES * P + s, SUBLANES, stride=P), :])


def _lru_coeffs(xp_scr, a_scr, b_scr, gate_scr, conv_w, conv_b, wg, b_r, b_i, lam, rows_per_step, reset_rows):
    R = rows_per_step
    for h in range(N_HEADS):
        ls = slice(h * HEAD_DIM, (h + 1) * HEAD_DIM)
        xc = conv_b[:, ls] + xp_scr[h, 0:TM, :] * conv_w[0:1, ls]
        for k in range(1, CONV_WIDTH):
            xc = xc + xp_scr[h, k * R:k * R + TM, :] * conv_w[k:k + 1, ls]
        b_scr[:, ls] = xc
    for h in range(N_HEADS):
        ls = slice(h * HEAD_DIM, (h + 1) * HEAD_DIM)
        gate_scr[h // 4][:, (h % 4) * 2 * HEAD_DIM:(h % 4 + 1) * 2 * HEAD_DIM] = jnp.dot(
            b_scr[:, ls].astype(BF16), wg[h], preferred_element_type=F32)
    for h in range(N_HEADS):
        ls = slice(h * HEAD_DIM, (h + 1) * HEAD_DIM)
        xc = b_scr[:, ls]
        g = gate_scr[h // 4][:, (h % 4) * 2 * HEAD_DIM:(h % 4 + 1) * 2 * HEAD_DIM]
        r = _sigmoid(g[:, :HEAD_DIM] + b_r[:, ls])
        i = _sigmoid(g[:, HEAD_DIM:] + b_i[:, ls])
        lam_h = lam[:, ls]
        softplus_neg = jnp.maximum(-lam_h, 0.0) + jnp.log1p(jnp.exp(-jnp.abs(lam_h)))
        log_a = r * (-LRU_C * softplus_neg)
        a = jnp.exp(log_a)
        mult = jnp.sqrt(-jnp.tanh(log_a) * (a * a + 1.0))
        ix = i * xc
        b = mult * ix
        a_scr[:, ls] = a
        b_scr[:, ls] = b
        if reset_rows is not None:
            a_scr[0:SUBLANES, ls] = jnp.where(reset_rows, 0.0, a[0:SUBLANES])
            b_scr[0:SUBLANES, ls] = jnp.where(reset_rows, ix[0:SUBLANES], b[0:SUBLANES])


def _sgu_head_fn(pre_scr, w_mix, bias_rows, cat, period=CHUNK):
    row = lax.broadcasted_iota(jnp.int32, (CHUNK, CHUNK), 0)
    col = lax.broadcasted_iota(jnp.int32, (CHUNK, CHUNK), 1)
    keep = row >= col
    if period < CHUNK:
        shift = period.bit_length() - 1
        keep = keep & ((row >> shift) == (col >> shift))
    blocks = [slice(ch * CHUNK, (ch + 1) * CHUNK) for ch in range(TM // CHUNK)]

    def head(hd):
        ls = slice(hd * HEAD_DIM, (hd + 1) * HEAD_DIM)
        vs = slice(WIDTH + hd * HEAD_DIM, WIDTH + (hd + 1) * HEAD_DIM)
        w_blk = w_mix[hd, 0:period, :]
        sh = period
        if sh < CHUNK:
            w_blk = jnp.where(col[0:period, :] < period, w_blk, 0.0)
        while sh < CHUNK:
            w_blk = w_blk + pltpu.roll(w_blk, sh, 1)
            sh *= 2
        w_h = jnp.where(keep, jnp.tile(w_blk, (CHUNK // period, 1)), 0.0).astype(BF16)
        bias = jnp.tile(bias_rows[0:period, ls], (CHUNK // period, 1))
        mixed = jnp.dot(w_h, jnp.concatenate([cat[rs, vs] for rs in blocks], axis=1), preferred_element_type=F32)
        for ch, rs in enumerate(blocks):
            s = mixed[:, ch * HEAD_DIM:(ch + 1) * HEAD_DIM] + bias
            cat[rs, vs] = (_gelu(pre_scr[U - 1, rs, ls]) * s * _silu(pre_scr[GS - 1, rs, ls])).astype(BF16)

    return head


def _load_weight(w_hbm, w_scr, stages, sem):
    n_stage = len(stages)
    n_row = w_hbm.shape[0] // TM
    n_chunks = w_scr.shape[0] * n_row
    assert n_chunks % n_stage == 0

    def aligned(v, m):
        return v if isinstance(v, int) else pl.multiple_of(v, m)

    def copy(k, slot):
        rows = pl.ds(aligned((k % n_row) * TM, TM), TM)
        cols = pl.ds(aligned((k // n_row) * WIDTH, WIDTH), WIDTH)
        return pltpu.make_async_copy(w_hbm.at[rows, cols], stages[slot], sem.at[slot])

    for k in range(n_stage):
        copy(k, k).start()

    def body(r, carry):
        for slot in range(n_stage):
            k = r * n_stage + slot
            copy(k, slot).wait()
            w_scr[k // n_row, pl.ds(aligned((k % n_row) * TM, TM), TM), :] = stages[slot][...].astype(BF16)

            @pl.when(k + n_stage < n_chunks)
            def _():
                copy(k + n_stage, slot).start()
        return carry

    lax.fori_loop(0, n_chunks // n_stage, body, 0)


def _layer_kernel(tiles_per_seq, n_tiles, n_dec, dec_t,
                  x_ref, xprev_ref, xs_hbm, cs_hbm, hs_hbm, w_in_hbm, w_out_hbm, wr_hbm, wi_hbm,
                  pre_g, post_g, conv_w, conv_b, b_r, b_i, lam, sgu_g, w_s, bs_rows,
                  y_ref, conv_out, h_out, ys_hbm, convs_hbm, hs_out_hbm, vs_hbm,
                  w_in, w_out, wg, z_scr, xr_scr, xp_scr, a_scr, b_scr, pre_scr, out_scr, cat_scr,
                  carry_h, carry_x, xr_tail, xs_buf, h0_buf, hn_buf, w_sem, in_sem, out_sem):
    hp_scr = xr_scr
    vs_buf = b_scr
    xr_nat = xs_buf.at[0, :, pl.ds(0, WIDTH)]
    i = pl.program_id(0)
    t = i % tiles_per_seq
    cat_prev = cat_scr.at[(i + 1) % 2]
    cat_cur = cat_scr.at[i % 2]
    z_cur = z_scr.at[i % 2]
    z_next = z_scr.at[(i + 1) % 2]

    sq_sums = []

    def out_piece(c):
        piece = jnp.dot(cat_prev[...], w_out[c], preferred_element_type=F32)
        out_scr[:, c * WIDTH:(c + 1) * WIDTH] = piece
        sq_sums.append(jnp.sum(piece * piece, axis=-1, keepdims=True))

    def in_piece(slab):
        pre_scr[slab - 1] = jnp.dot(z_cur[...], w_in[slab], preferred_element_type=F32)

    def finish_prev():
        inv = lax.rsqrt(functools.reduce(lambda p, q: p + q, sq_sums) * (1.0 / D_MODEL) + EPS)
        sq_sums.clear()
        y_ref[...] = xprev_ref[...] + out_scr[...] * inv * post_g[...]

    @pl.when(i == 0)
    def _():
        halves = lambda ref: [ref.at[:, pl.ds(c * WIDTH, WIDTH)] for c in range(D_MODEL // WIDTH)]
        stages = [pre_scr.at[s] for s in range(pre_scr.shape[0])] + halves(out_scr) + halves(xs_buf.at[0])
        gates_in = [pltpu.make_async_copy(src, xr_scr.at[:, pl.ds(k * HEAD_DIM, HEAD_DIM), :], in_sem.at[k])
                    for k, src in enumerate((wr_hbm, wi_hbm))]
        for cp in gates_in:
            cp.start()
        _load_weight(w_in_hbm, w_in, stages, w_sem)
        _load_weight(w_out_hbm, w_out, stages, w_sem)
        for k, cp in enumerate(gates_in):
            cp.wait()
            wg[:, :, k * HEAD_DIM:(k + 1) * HEAD_DIM] = xr_scr[:, k * HEAD_DIM:(k + 1) * HEAD_DIM, :].astype(BF16)
        cat_scr[1] = jnp.zeros(cat_scr.shape[1:], BF16)
        _pre_norm(xprev_ref, pre_g, z_cur)
        xr_nat[...] = jnp.dot(z_cur[...], w_in[XR], preferred_element_type=F32)

    @pl.when(t == 0)
    def _():
        carry_h[...] = jnp.zeros_like(carry_h)
        carry_x[...] = jnp.zeros_like(carry_x)

    @pl.when(i == n_tiles)
    def _():
        x_in, h_in, cs_in = _decode_copies_in(0, _decode_geometry(dec_t), xs_hbm, cs_hbm, hs_hbm, xs_buf, xp_scr, h0_buf,
                                              in_sem)
        for cp in [x_in, h_in] + cs_in:
            cp.start()
        for c in range(D_MODEL // WIDTH):
            out_piece(c)
        finish_prev()

    @pl.when(i > n_tiles)
    def _():
        _decode_tile(i - n_tiles - 1, n_dec, _decode_geometry(dec_t), xs_hbm, cs_hbm, hs_hbm, pre_g, post_g, conv_w, conv_b,
                     wg, b_r, b_i, lam, sgu_g, w_s, bs_rows, ys_hbm, convs_hbm, hs_out_hbm, vs_hbm, w_in, w_out, z_scr.at[0],
                     xr_scr, xp_scr, a_scr, b_scr, hp_scr, pre_scr, out_scr, cat_scr.at[0], xs_buf, vs_buf,
                     h0_buf, hn_buf, in_sem, out_sem)

    @pl.when(i < n_tiles)
    def _():
        geo = PROMPT
        R = geo.n_seg

        xr = xr_nat[...]
        xr_tail[...] = xr[TM - SUBLANES:TM, :]
        _to_step_order(xr, xr_scr, xp_scr, geo)
        sub = lax.broadcasted_iota(jnp.int32, (R, WIDTH), 0)
        for m in range(HIST):
            src = HIST * R + (geo.seg_len - HIST + m) * R
            rs = slice(m * R, (m + 1) * R)
            for c in range(N_SLABS):
                ls = slice(c * LANES, (c + 1) * LANES)
                rolled = pltpu.roll(xp_scr[c, src:src + R, :], 1, 0)
                xp_scr[c, rs, :] = jnp.where(sub[:, 0:LANES] == 0, carry_x[rs, ls], rolled)
                carry_x[rs, ls] = rolled

        reset_rows = (lax.broadcasted_iota(jnp.int32, (SUBLANES, HEAD_DIM), 0) == 0) & (t == 0)
        _lru_coeffs(xp_scr, a_scr, b_scr, (pre_scr.at[U - 1], pre_scr.at[GS - 1]), conv_w, conv_b, wg, b_r, b_i, lam,
                    R, reset_rows)

        in_piece(GR)
        hl = jnp.zeros((R, WIDTH), F32)
        pr = jnp.ones((R, WIDTH), F32)
        for k in range(geo.seg_len):
            a_k = a_scr[k * R:(k + 1) * R, :]
            hl = a_k * hl + b_scr[k * R:(k + 1) * R, :]
            pr = a_k * pr
        c_in = carry_h[0:1, :]
        h0 = jnp.zeros((R, WIDTH), F32)
        for j in range(R):
            h0 = jnp.where(sub == j, c_in, h0)
            c_in = hl[j:j + 1, :] + pr[j:j + 1, :] * c_in
        carry_h[0:1, :] = c_in
        h = h0
        for k in range(geo.seg_len):
            h = a_scr[k * R:(k + 1) * R, :] * h + b_scr[k * R:(k + 1) * R, :]
            for c in range(N_SLABS):
                hp_scr[c, pl.ds(k, R, stride=geo.seg_pitch), :] = h[:, c * LANES:(c + 1) * LANES]
        in_piece(V)
        for j in range(R):
            rs = slice(j * geo.seg_len, (j + 1) * geo.seg_len)
            for c in range(N_SLABS):
                ls = slice(c * LANES, (c + 1) * LANES)
                h_nat = hp_scr[c, j * geo.seg_pitch:j * geo.seg_pitch + geo.seg_len, :]
                cat_cur[rs, ls] = (h_nat * _silu(pre_scr[GR - 1, rs, ls])).astype(BF16)

        in_piece(U)
        in_piece(GS)
        cat_cur[:, WIDTH:2 * WIDTH] = _rms_norm(_gelu(pre_scr[V - 1]), sgu_g[...]).astype(BF16)
        sgu_head = _sgu_head_fn(pre_scr, w_s, bs_rows, cat_cur)
        for hd in range(N_HEADS):
            if hd % (N_HEADS // (D_MODEL // WIDTH)) == 0:
                out_piece(hd // (N_HEADS // (D_MODEL // WIDTH)))
            sgu_head(hd)
        _pre_norm(x_ref, pre_g, z_next)
        xr_nat[...] = jnp.dot(z_next[...], w_in[XR], preferred_element_type=F32)
        finish_prev()

    @pl.when((i < n_tiles) & (t == tiles_per_seq - 1))
    def _():
        seq = i // tiles_per_seq
        for m in range(HIST):
            conv_out[m, pl.ds(seq, 1), :] = xr_tail[SUBLANES - HIST + m:SUBLANES - HIST + m + 1, :]
        h_out[pl.ds(seq, 1), :] = carry_h[0:1, :]


def _decode_copies_in(d, geo, xs_hbm, cs_hbm, hs_hbm, xs_buf, xp_scr, h0_buf, in_sem):
    R = geo.n_seg
    seqs = pl.ds(pl.multiple_of(d * R, R), R)
    x_in = pltpu.make_async_copy(xs_hbm.at[pl.ds(pl.multiple_of(d * TM, TM), TM), :], xs_buf.at[d % 2], in_sem.at[0])
    h_in = pltpu.make_async_copy(hs_hbm.at[seqs, :], h0_buf, in_sem.at[1])
    cs_in = [pltpu.make_async_copy(cs_hbm.at[m, seqs, pl.ds(c * LANES, LANES)], xp_scr.at[c, pl.ds(m * R, R), :],
                                   in_sem.at[2 + m * N_SLABS + c])
             for m in range(HIST) for c in range(N_SLABS)]
    return x_in, h_in, cs_in


def _decode_tile(d, n_dec, geo, xs_hbm, cs_hbm, hs_hbm, pre_g, post_g, conv_w, conv_b, wg, b_r, b_i, lam, sgu_g,
                 w_mix, bias_rows, ys_hbm, convs_hbm, hs_out_hbm, vs_hbm, w_in, w_out, z_scr, xr_scr, xp_scr, a_scr, b_scr,
                 hp_scr, pre_scr, out_scr, cat, xs_buf, vs_buf, h0_buf, hn_buf, in_sem, out_sem):
    R, L = geo.n_seg, geo.seg_len
    tile_rows = lambda dd: pl.ds(pl.multiple_of(dd * TM, TM), TM)
    tile_seqs = lambda dd: pl.ds(pl.multiple_of(dd * R, R), R)

    def copies_out(dd):
        main = [pltpu.make_async_copy(out_scr, ys_hbm.at[tile_rows(dd), :], out_sem.at[0]),
                pltpu.make_async_copy(vs_buf, vs_hbm.at[tile_rows(dd), :], out_sem.at[1]),
                pltpu.make_async_copy(hn_buf, hs_out_hbm.at[tile_seqs(dd), :], out_sem.at[2])]
        conv = [pltpu.make_async_copy(xp_scr.at[c, pl.ds((L + m) * R, R), :],
                                      convs_hbm.at[m, tile_seqs(dd), pl.ds(c * LANES, LANES)],
                                      out_sem.at[3 + m * N_SLABS + c])
                for m in range(HIST) for c in range(N_SLABS)]
        return main + conv

    x_in, h_in, cs_in = _decode_copies_in(d, geo, xs_hbm, cs_hbm, hs_hbm, xs_buf, xp_scr, h0_buf, in_sem)
    x_next, h_next, cs_next = _decode_copies_in(d + 1, geo, xs_hbm, cs_hbm, hs_hbm, xs_buf, xp_scr, h0_buf, in_sem)
    has_next = d + 1 < n_dec
    xs_cur = xs_buf.at[d % 2]
    for cp in [x_in, h_in] + cs_in:
        cp.wait()

    @pl.when(has_next)
    def _():
        x_next.start()

    @pl.when(d > 0)
    def _():
        for cp in copies_out(d - 1)[1:]:
            cp.wait()

    _pre_norm(xs_cur, pre_g, z_scr)

    xr = jnp.dot(z_scr[...], w_in[XR], preferred_element_type=F32)
    _to_step_order(xr, xr_scr, xp_scr, geo)
    _lru_coeffs(xp_scr, a_scr, b_scr, (pre_scr.at[U - 1], pre_scr.at[GS - 1]), conv_w, conv_b, wg, b_r, b_i, lam, R,
                None)

    def in_piece(slab):
        pre_scr[slab - 1] = jnp.dot(z_scr[...], w_in[slab], preferred_element_type=F32)

    in_piece(GR)
    for c in range(N_SLABS):
        ls = slice(c * LANES, (c + 1) * LANES)
        h = h0_buf[:, ls]
        for s in range(L):
            h = a_scr[s * R:(s + 1) * R, ls] * h + b_scr[s * R:(s + 1) * R, ls]
            hp_scr[c, s * geo.step_pitch:s * geo.step_pitch + R, :] = h
        hn_buf[:, ls] = h
    in_piece(V)
    _to_natural_order(hp_scr, a_scr, geo)
    cat[:, 0:WIDTH] = (a_scr[...] * _silu(pre_scr[GR - 1])).astype(BF16)

    in_piece(U)
    in_piece(GS)
    vs_buf[...] = _rms_norm(_gelu(pre_scr[V - 1]), sgu_g[...])
    cat[:, WIDTH:2 * WIDTH] = vs_buf[...].astype(BF16)
    sgu_head = _sgu_head_fn(pre_scr, w_mix, bias_rows, cat, period=L)
    for hd in range(N_HEADS):
        sgu_head(hd)
    for cp in copies_out(d)[1:]:
        cp.start()

    @pl.when(d > 0)
    def _():
        copies_out(d - 1)[0].wait()

    @pl.when(has_next)
    def _():
        for cp in [h_next] + cs_next:
            cp.start()

    for c in range(D_MODEL // WIDTH):
        out_scr[:, c * WIDTH:(c + 1) * WIDTH] = jnp.dot(cat[...], w_out[c], preferred_element_type=F32)
    out_scr[...] = xs_cur[...] + _rms_norm(out_scr[...], post_g[...])

    copies_out(d)[0].start()

    @pl.when(d == n_dec - 1)
    def _():
        for cp in copies_out(d):
            cp.wait()


def _full(shape):
    return pl.BlockSpec(shape, lambda *_: (0,) * len(shape))


def _layer(x_prompt, x_sample, conv_state, h_state, pre_g, post_g, w_in, conv_w, conv_b, w_r, b_r, w_i, b_i, lam,
           sgu_g, w_s, b_s, w_out):
    n_seq, seq_len, d_model = x_prompt.shape
    dec_b, dec_t, _ = x_sample.shape
    assert d_model == D_MODEL and conv_w.shape == (CONV_WIDTH, WIDTH) and w_in.shape == (D_MODEL, 5 * WIDTH)
    assert seq_len % TM == 0 and dec_t * DEC_NB == TM and dec_b % DEC_NB == 0
    assert dec_t % SUBLANES == 0 and CHUNK % dec_t == 0 and HIST <= dec_t and dec_t & (dec_t - 1) == 0

    assert w_r.shape == w_i.shape == (N_SLABS, HEAD_DIM, HEAD_DIM)
    row = lambda p: p.reshape(1, -1)
    bs_prompt = jnp.repeat(b_s.T, HEAD_DIM, axis=1)
    params = (row(pre_g), row(post_g), conv_w, row(conv_b), row(b_r), row(b_i), row(lam), row(sgu_g), w_s, bs_prompt)

    tiles = seq_len // TM
    n_tiles = n_seq * tiles
    n_dec = dec_b // DEC_NB
    x_rows = x_prompt.reshape(n_seq * seq_len, d_model)
    xs_rows = x_sample.reshape(dec_b * dec_t, d_model)
    cs_steps = jnp.transpose(conv_state, (1, 0, 2))
    next_of = lambda i: jnp.minimum(i + 1, n_tiles - 1)
    prev_of = lambda i: jnp.clip(i - 1, 0, n_tiles - 1)
    any_spec = pl.BlockSpec(memory_space=pl.ANY)
    dec_geo = _decode_geometry(dec_t)
    relayout_rows = max(g.n_seg * g.seg_pitch for g in (PROMPT, dec_geo))
    assert relayout_rows >= max(g.seg_len * g.step_pitch for g in (PROMPT, dec_geo))
    y_rows, conv_p, h_p, ys_rows, conv_s, h_s, vs_rows = pl.pallas_call(
        functools.partial(_layer_kernel, tiles, n_tiles, n_dec, dec_t),
        grid=(n_tiles + 1 + n_dec,),
        in_specs=[pl.BlockSpec((TM, d_model), lambda i: (next_of(i), 0)),
                  pl.BlockSpec((TM, d_model), lambda i: (prev_of(i), 0)),
                  any_spec, any_spec, any_spec, any_spec, any_spec, any_spec, any_spec]
                 + [_full(p.shape) for p in params],
        out_specs=[pl.BlockSpec((TM, d_model), lambda i: (prev_of(i), 0)),
                   _full((HIST, n_seq, WIDTH)), _full((n_seq, WIDTH)),
                   any_spec, any_spec, any_spec, any_spec],
        out_shape=[jax.ShapeDtypeStruct(x_rows.shape, F32),
                   jax.ShapeDtypeStruct((HIST, n_seq, WIDTH), F32),
                   jax.ShapeDtypeStruct((n_seq, WIDTH), F32),
                   jax.ShapeDtypeStruct(xs_rows.shape, F32),
                   jax.ShapeDtypeStruct(cs_steps.shape, F32),
                   jax.ShapeDtypeStruct(h_state.shape, F32),
                   jax.ShapeDtypeStruct((dec_b * dec_t, WIDTH), F32)],
        scratch_shapes=[pltpu.VMEM((5, D_MODEL, WIDTH), BF16),
                        pltpu.VMEM((D_MODEL // WIDTH, D_MODEL, WIDTH), BF16),
                        pltpu.VMEM((N_HEADS, HEAD_DIM, 2 * HEAD_DIM), BF16),
                        pltpu.VMEM((2, TM, D_MODEL), BF16),
                        pltpu.VMEM((N_SLABS, relayout_rows, LANES), F32),
                        pltpu.VMEM((N_SLABS, TM + HIST * DEC_NB, LANES), F32),
                        pltpu.VMEM((TM, WIDTH), F32), pltpu.VMEM((TM, WIDTH), F32),
                        pltpu.VMEM((N_PRE, TM, WIDTH), F32),
                        pltpu.VMEM((TM, D_MODEL), F32),
                        pltpu.VMEM((2, TM, D_MODEL), BF16),
                        pltpu.VMEM((SUBLANES, WIDTH), F32),
                        pltpu.VMEM((HIST * SUBLANES, WIDTH), F32),
                        pltpu.VMEM((SUBLANES, WIDTH), F32),
                        pltpu.VMEM((2, TM, D_MODEL), F32),
                        pltpu.VMEM((DEC_NB, WIDTH), F32), pltpu.VMEM((DEC_NB, WIDTH), F32),
                        pltpu.SemaphoreType.DMA((N_STAGE,)),
                        pltpu.SemaphoreType.DMA((2 + HIST * N_SLABS,)),
                        pltpu.SemaphoreType.DMA((3 + HIST * N_SLABS,))],
        compiler_params=pltpu.CompilerParams(dimension_semantics=("arbitrary",),
                                             vmem_limit_bytes=VMEM_LIMIT_BYTES),
        name="hybrid_layer",
    )(x_rows, x_rows, xs_rows, cs_steps, h_state, w_in, w_out, w_r, w_i, *params)
    seq_major = lambda c: jnp.transpose(c, (1, 0, 2))
    return (y_rows.reshape(x_prompt.shape), ys_rows.reshape(x_sample.shape), seq_major(conv_p), h_p,
            seq_major(conv_s), h_s, vs_rows.reshape(dec_b, dec_t, WIDTH))


def kernel(x_prompt, x_sample, state_rglru_conv, state_rglru_h, pre_norm_g, post_norm_g, w_in, conv_w, conv_b,
           w_rgate, b_rgate, w_igate, b_igate, lru_lambda, sgu_norm_g, w_spatial, b_spatial, w_out):
    depth = w_in.shape[0]
    yp, ys = x_prompt, x_sample
    conv_p, h_p, conv_s, h_s, v_s = [], [], [], [], []
    for l in range(depth):
        yp, ys, cp, hp, cs, hs, vs = _layer(
            yp, ys, state_rglru_conv[l], state_rglru_h[l], pre_norm_g[l], post_norm_g[l], w_in[l], conv_w[l],
            conv_b[l], w_rgate[l], b_rgate[l], w_igate[l], b_igate[l], lru_lambda[l], sgu_norm_g[l], w_spatial[l],
            b_spatial[l], w_out[l])
        conv_p.append(cp)
        h_p.append(hp)
        conv_s.append(cs)
        h_s.append(hs)
        v_s.append(vs)
    return (yp, ys, jnp.stack(conv_p), jnp.stack(h_p), jnp.stack(conv_s), jnp.stack(h_s), jnp.stack(v_s))
```

```python
import collections
import functools
import math

import jax
import jax.numpy as jnp
from jax import lax
from jax.experimental import pallas as pl
from jax.experimental.pallas import tpu as pltpu

F32 = jnp.float32
BF16 = jnp.bfloat16

EPS = 1e-6
LRU_C = 8.0
CONV_WIDTH = 4
HIST = CONV_WIDTH - 1
N_HEADS = 8
HEAD_DIM = 128
CHUNK = 128
LANES = 128
SUBLANES = 8
N_SLABS = 8
MXU_DIM = 256

D_MODEL = 2048
WIDTH = 1024
TM = 256
XR, GR, U, V, GS = range(5)
N_PRE = 4
N_STAGE = N_PRE + 2 * (D_MODEL // WIDTH)
VMEM_LIMIT_BYTES = 63 * 1024 * 1024

Geometry = collections.namedtuple("Geometry", "n_seg seg_len seg_pitch step_pitch")
PROMPT = Geometry(n_seg=SUBLANES, seg_len=TM // SUBLANES, seg_pitch=TM // SUBLANES + SUBLANES, step_pitch=SUBLANES)
DEC_NB = 32


def _decode_geometry(n_steps):
    return Geometry(n_seg=TM // n_steps, seg_len=n_steps, seg_pitch=n_steps, step_pitch=TM // n_steps + SUBLANES)


def _sigmoid(x):
    return 0.5 * jnp.tanh(0.5 * x) + 0.5


def _silu(x):
    return x * _sigmoid(x)


def _gelu(x):
    c = math.sqrt(2.0 / math.pi)
    return x * (0.5 * (1.0 + jnp.tanh(c * (x + 0.044715 * (x * x * x)))))


def _rms_norm(x, g):
    return x * lax.rsqrt(jnp.mean(x * x, axis=-1, keepdims=True) + EPS) * g


def _pre_norm(x_ref, g_ref, z_scr):
    x = x_ref[...]
    inv = lax.rsqrt(jnp.mean(x * x, axis=-1, keepdims=True) + EPS)
    for kb in range(D_MODEL // MXU_DIM):
        cs = slice(kb * MXU_DIM, (kb + 1) * MXU_DIM)
        z_scr[:, cs] = (x[:, cs] * inv * g_ref[:, cs]).astype(BF16)


def _to_step_order(xr, xr_scr, xp_scr, geo):
    R, L, P = geo.n_seg, geo.seg_len, geo.seg_pitch
    if R == SUBLANES:
        for s in range(R):
            for m in range(L // SUBLANES):
                for c in range(N_SLABS):
                    xp_scr[c, pl.ds(HIST * R + m * SUBLANES * R + s, SUBLANES, stride=R), :] = (
                        xr[s * L + m * SUBLANES:s * L + (m + 1) * SUBLANES, c * LANES:(c + 1) * LANES])
        return
    for s in range(R):
        for c in range(N_SLABS):
            xr_scr[c, s * P:s * P + L, :] = xr[s * L:(s + 1) * L, c * LANES:(c + 1) * LANES]
    for k in range(L):
        for g in range(R // SUBLANES):
            row = HIST * R + k * R + g * SUBLANES
            for c in range(N_SLABS):
                xp_scr[c, row:row + SUBLANES, :] = xr_scr[c, pl.ds(g * SUBLANES * P + k, SUBLANES, stride=P), :]


def _to_natural_order(hp_scr, dst_scr, geo):
    R, L, P = geo.n_seg, geo.seg_len, geo.step_pitch
    for s in range(R):
        for m in range(L // SUBLANES):
            row = s * L + m * SUBLANES
            for c in range(N_SLABS):
                dst_scr[row:row + SUBLANES, c * LANES:(c + 1) * LANES] = (
                    hp_scr[c, pl.ds(m * SUBLANES * P + s, SUBLANES, stride=P), :])


def _lru_coeffs(xp_scr, a_scr, b_scr, gate_scr, conv_w, conv_b, wg, b_r, b_i, lam, rows_per_step, reset_rows):
    R = rows_per_step
    for h in range(N_HEADS):
        ls = slice(h * HEAD_DIM, (h + 1) * HEAD_DIM)
        xc = conv_b[:, ls] + xp_scr[h, 0:TM, :] * conv_w[0:1, ls]
        for k in range(1, CONV_WIDTH):
            xc = xc + xp_scr[h, k * R:k * R + TM, :] * conv_w[k:k + 1, ls]
        b_scr[:, ls] = xc
    for h in range(N_HEADS):
        ls = slice(h * HEAD_DIM, (h + 1) * HEAD_DIM)
        gate_scr[h // 4][:, (h % 4) * 2 * HEAD_DIM:(h % 4 + 1) * 2 * HEAD_DIM] = jnp.dot(
            b_scr[:, ls].astype(BF16), wg[h], preferred_element_type=F32)
    for h in range(N_HEADS):
        ls = slice(h * HEAD_DIM, (h + 1) * HEAD_DIM)
        xc = b_scr[:, ls]
        g = gate_scr[h // 4][:, (h % 4) * 2 * HEAD_DIM:(h % 4 + 1) * 2 * HEAD_DIM]
        r = _sigmoid(g[:, :HEAD_DIM] + b_r[:, ls])
        i = _sigmoid(g[:, HEAD_DIM:] + b_i[:, ls])
        lam_h = lam[:, ls]
        softplus_neg = jnp.maximum(-lam_h, 0.0) + jnp.log1p(jnp.exp(-jnp.abs(lam_h)))
        log_a = r * (-LRU_C * softplus_neg)
        a = jnp.exp(log_a)
        mult = jnp.sqrt(-jnp.tanh(log_a) * (a * a + 1.0))
        ix = i * xc
        b = mult * ix
        a_scr[:, ls] = a
        b_scr[:, ls] = b
        if reset_rows is not None:
            a_scr[0:SUBLANES, ls] = jnp.where(reset_rows, 0.0, a[0:SUBLANES])
            b_scr[0:SUBLANES, ls] = jnp.where(reset_rows, ix[0:SUBLANES], b[0:SUBLANES])


def _sgu_head_fn(pre_scr, w_mix, bias_rows, cat, period=CHUNK):
    row = lax.broadcasted_iota(jnp.int32, (CHUNK, CHUNK), 0)
    col = lax.broadcasted_iota(jnp.int32, (CHUNK, CHUNK), 1)
    keep = row >= col
    if period < CHUNK:
        shift = period.bit_length() - 1
        keep = keep & ((row >> shift) == (col >> shift))
    blocks = [slice(ch * CHUNK, (ch + 1) * CHUNK) for ch in range(TM // CHUNK)]

    def head(hd):
        ls = slice(hd * HEAD_DIM, (hd + 1) * HEAD_DIM)
        vs = slice(WIDTH + hd * HEAD_DIM, WIDTH + (hd + 1) * HEAD_DIM)
        w_blk = w_mix[hd, 0:period, :]
        sh = period
        if sh < CHUNK:
            w_blk = jnp.where(col[0:period, :] < period, w_blk, 0.0)
        while sh < CHUNK:
            w_blk = w_blk + pltpu.roll(w_blk, sh, 1)
            sh *= 2
        w_h = jnp.where(keep, jnp.tile(w_blk, (CHUNK // period, 1)), 0.0).astype(BF16)
        bias = jnp.tile(bias_rows[0:period, ls], (CHUNK // period, 1))
        mixed = jnp.dot(w_h, jnp.concatenate([cat[rs, vs] for rs in blocks], axis=1), preferred_element_type=F32)
        for ch, rs in enumerate(blocks):
            s = mixed[:, ch * HEAD_DIM:(ch + 1) * HEAD_DIM] + bias
            cat[rs, vs] = (_gelu(pre_scr[U - 1, rs, ls]) * s * _silu(pre_scr[GS - 1, rs, ls])).astype(BF16)

    return head


def _load_weight(w_hbm, w_scr, stages, sem):
    n_stage = len(stages)
    n_row = w_hbm.shape[0] // TM
    n_chunks = w_scr.shape[0] * n_row
    assert n_chunks % n_stage == 0

    def aligned(v, m):
        return v if isinstance(v, int) else pl.multiple_of(v, m)

    def copy(k, slot):
        rows = pl.ds(aligned((k % n_row) * TM, TM), TM)
        cols = pl.ds(aligned((k // n_row) * WIDTH, WIDTH), WIDTH)
        return pltpu.make_async_copy(w_hbm.at[rows, cols], stages[slot], sem.at[slot])

    for k in range(n_stage):
        copy(k, k).start()

    def body(r, carry):
        for slot in range(n_stage):
            k = r * n_stage + slot
            copy(k, slot).wait()
            w_scr[k // n_row, pl.ds(aligned((k % n_row) * TM, TM), TM), :] = stages[slot][...].astype(BF16)

            @pl.when(k + n_stage < n_chunks)
            def _():
                copy(k + n_stage, slot).start()
        return carry

    lax.fori_loop(0, n_chunks // n_stage, body, 0)


def _layer_kernel(tiles_per_seq, n_tiles, n_dec, dec_t,
                  x_ref, xprev_ref, xs_hbm, cs_hbm, hs_hbm, w_in_hbm, w_out_hbm, wr_hbm, wi_hbm,
                  pre_g, post_g, conv_w, conv_b, b_r, b_i, lam, sgu_g, w_s, bs_rows,
                  y_ref, conv_out, h_out, ys_hbm, convs_hbm, hs_out_hbm, vs_hbm,
                  w_in, w_out, wg, z_scr, xr_scr, xp_scr, a_scr, b_scr, pre_scr, out_scr, cat_scr,
                  carry_h, carry_x, xr_tail, xs_buf, h0_buf, hn_buf, w_sem, in_sem, out_sem):
    hp_scr = xr_scr
    vs_buf = b_scr
    xr_nat = xs_buf.at[0, :, pl.ds(0, WIDTH)]
    i = pl.program_id(0)
    t = i % tiles_per_seq
    cat_prev = cat_scr.at[(i + 1) % 2]
    cat_cur = cat_scr.at[i % 2]
    z_cur = z_scr.at[i % 2]
    z_next = z_scr.at[(i + 1) % 2]

    sq_sums = []

    def out_piece(c):
        piece = jnp.dot(cat_prev[...], w_out[c], preferred_element_type=F32)
        out_scr[:, c * WIDTH:(c + 1) * WIDTH] = piece
        sq_sums.append(jnp.sum(piece * piece, axis=-1, keepdims=True))

    def in_piece(slab):
        pre_scr[slab - 1] = jnp.dot(z_cur[...], w_in[slab], preferred_element_type=F32)

    def finish_prev():
        inv = lax.rsqrt(functools.reduce(lambda p, q: p + q, sq_sums) * (1.0 / D_MODEL) + EPS)
        sq_sums.clear()
        y_ref[...] = xprev_ref[...] + out_scr[...] * inv * post_g[...]

    @pl.when(i == 0)
    def _():
        halves = lambda ref: [ref.at[:, pl.ds(c * WIDTH, WIDTH)] for c in range(D_MODEL // WIDTH)]
        stages = [pre_scr.at[s] for s in range(pre_scr.shape[0])] + halves(out_scr) + halves(xs_buf.at[0])
        gates_in = [pltpu.make_async_copy(src, xr_scr.at[:, pl.ds(k * HEAD_DIM, HEAD_DIM), :], in_sem.at[k])
                    for k, src in enumerate((wr_hbm, wi_hbm))]
        for cp in gates_in:
            cp.start()
        _load_weight(w_in_hbm, w_in, stages, w_sem)
        _load_weight(w_out_hbm, w_out, stages, w_sem)
        for k, cp in enumerate(gates_in):
            cp.wait()
            wg[:, :, k * HEAD_DIM:(k + 1) * HEAD_DIM] = xr_scr[:, k * HEAD_DIM:(k + 1) * HEAD_DIM, :].astype(BF16)
        cat_scr[1] = jnp.zeros(cat_scr.shape[1:], BF16)
        _pre_norm(xprev_ref, pre_g, z_cur)
        xr_nat[...] = jnp.dot(z_cur[...], w_in[XR], preferred_element_type=F32)

    @pl.when(t == 0)
    def _():
        carry_h[...] = jnp.zeros_like(carry_h)
        carry_x[...] = jnp.zeros_like(carry_x)

    @pl.when(i == n_tiles)
    def _():
        x_in, h_in, cs_in = _decode_copies_in(0, _decode_geometry(dec_t), xs_hbm, cs_hbm, hs_hbm, xs_buf, xp_scr, h0_buf,
                                              in_sem)
        for k, cp in enumerate([x_in, h_in] + cs_in):
            cp.start(priority=k % 2)
        for c in range(D_MODEL // WIDTH):
            out_piece(c)
        finish_prev()

    @pl.when(i > n_tiles)
    def _():
        _decode_tile(i - n_tiles - 1, n_dec, _decode_geometry(dec_t), xs_hbm, cs_hbm, hs_hbm, pre_g, post_g, conv_w, conv_b,
                     wg, b_r, b_i, lam, sgu_g, w_s, bs_rows, ys_hbm, convs_hbm, hs_out_hbm, vs_hbm, w_in, w_out, z_scr.at[0],
                     xr_scr, xp_scr, a_scr, b_scr, hp_scr, pre_scr, out_scr, cat_scr.at[0], xs_buf, vs_buf,
                     h0_buf, hn_buf, in_sem, out_sem)

    @pl.when(i < n_tiles)
    def _():
        geo = PROMPT
        R = geo.n_seg

        xr = xr_nat[...]
        xr_tail[...] = xr[TM - SUBLANES:TM, :]
        _to_step_order(xr, xr_scr, xp_scr, geo)
        sub = lax.broadcasted_iota(jnp.int32, (R, WIDTH), 0)
        for m in range(HIST):
            src = HIST * R + (geo.seg_len - HIST + m) * R
            rs = slice(m * R, (m + 1) * R)
            for c in range(N_SLABS):
                ls = slice(c * LANES, (c + 1) * LANES)
                rolled = pltpu.roll(xp_scr[c, src:src + R, :], 1, 0)
                xp_scr[c, rs, :] = jnp.where(sub[:, 0:LANES] == 0, carry_x[rs, ls], rolled)
                carry_x[rs, ls] = rolled

        reset_rows = (lax.broadcasted_iota(jnp.int32, (SUBLANES, HEAD_DIM), 0) == 0) & (t == 0)
        _lru_coeffs(xp_scr, a_scr, b_scr, (pre_scr.at[U - 1], pre_scr.at[GS - 1]), conv_w, conv_b, wg, b_r, b_i, lam,
                    R, reset_rows)

        in_piece(GR)
        hl = jnp.zeros((R, WIDTH), F32)
        pr = jnp.ones((R, WIDTH), F32)
        for k in range(geo.seg_len):
            a_k = a_scr[k * R:(k + 1) * R, :]
            hl = a_k * hl + b_scr[k * R:(k + 1) * R, :]
            pr = a_k * pr
        c_in = carry_h[0:1, :]
        h0 = jnp.zeros((R, WIDTH), F32)
        for j in range(R):
            h0 = jnp.where(sub == j, c_in, h0)
            c_in = hl[j:j + 1, :] + pr[j:j + 1, :] * c_in
        carry_h[0:1, :] = c_in
        h = h0
        for k in range(geo.seg_len):
            h = a_scr[k * R:(k + 1) * R, :] * h + b_scr[k * R:(k + 1) * R, :]
            for c in range(N_SLABS):
                hp_scr[c, pl.ds(k, R, stride=geo.seg_pitch), :] = h[:, c * LANES:(c + 1) * LANES]
        in_piece(V)
        for j in range(R):
            rs = slice(j * geo.seg_len, (j + 1) * geo.seg_len)
            for c in range(N_SLABS):
                ls = slice(c * LANES, (c + 1) * LANES)
                h_nat = hp_scr[c, j * geo.seg_pitch:j * geo.seg_pitch + geo.seg_len, :]
                cat_cur[rs, ls] = (h_nat * _silu(pre_scr[GR - 1, rs, ls])).astype(BF16)

        in_piece(U)
        in_piece(GS)
        cat_cur[:, WIDTH:2 * WIDTH] = _rms_norm(_gelu(pre_scr[V - 1]), sgu_g[...]).astype(BF16)
        sgu_head = _sgu_head_fn(pre_scr, w_s, bs_rows, cat_cur)
        for hd in range(N_HEADS):
            if hd % (N_HEADS // (D_MODEL // WIDTH)) == 0:
                out_piece(hd // (N_HEADS // (D_MODEL // WIDTH)))
            sgu_head(hd)
        _pre_norm(x_ref, pre_g, z_next)
        xr_nat[...] = jnp.dot(z_next[...], w_in[XR], preferred_element_type=F32)
        finish_prev()

    @pl.when((i < n_tiles) & (t == tiles_per_seq - 1))
    def _():
        seq = i // tiles_per_seq
        for m in range(HIST):
            conv_out[m, pl.ds(seq, 1), :] = xr_tail[SUBLANES - HIST + m:SUBLANES - HIST + m + 1, :]
        h_out[pl.ds(seq, 1), :] = carry_h[0:1, :]


def _decode_copies_in(d, geo, xs_hbm, cs_hbm, hs_hbm, xs_buf, xp_scr, h0_buf, in_sem):
    R = geo.n_seg
    seqs = pl.ds(pl.multiple_of(d * R, R), R)
    x_in = pltpu.make_async_copy(xs_hbm.at[pl.ds(pl.multiple_of(d * TM, TM), TM), :], xs_buf.at[d % 2], in_sem.at[0])
    h_in = pltpu.make_async_copy(hs_hbm.at[seqs, :], h0_buf, in_sem.at[1])
    cs_in = [pltpu.make_async_copy(cs_hbm.at[m, seqs, pl.ds(c * LANES, LANES)], xp_scr.at[c, pl.ds(m * R, R), :],
                                   in_sem.at[2 + m * N_SLABS + c])
             for m in range(HIST) for c in range(N_SLABS)]
    return x_in, h_in, cs_in


def _decode_tile(d, n_dec, geo, xs_hbm, cs_hbm, hs_hbm, pre_g, post_g, conv_w, conv_b, wg, b_r, b_i, lam, sgu_g,
                 w_mix, bias_rows, ys_hbm, convs_hbm, hs_out_hbm, vs_hbm, w_in, w_out, z_scr, xr_scr, xp_scr, a_scr, b_scr,
                 hp_scr, pre_scr, out_scr, cat, xs_buf, vs_buf, h0_buf, hn_buf, in_sem, out_sem):
    R, L = geo.n_seg, geo.seg_len
    tile_rows = lambda dd: pl.ds(pl.multiple_of(dd * TM, TM), TM)
    tile_seqs = lambda dd: pl.ds(pl.multiple_of(dd * R, R), R)

    def copies_out(dd):
        main = [pltpu.make_async_copy(out_scr, ys_hbm.at[tile_rows(dd), :], out_sem.at[0]),
                pltpu.make_async_copy(vs_buf, vs_hbm.at[tile_rows(dd), :], out_sem.at[1]),
                pltpu.make_async_copy(hn_buf, hs_out_hbm.at[tile_seqs(dd), :], out_sem.at[2])]
        conv = [pltpu.make_async_copy(xp_scr.at[c, pl.ds((L + m) * R, R), :],
                                      convs_hbm.at[m, tile_seqs(dd), pl.ds(c * LANES, LANES)],
                                      out_sem.at[3 + m * N_SLABS + c])
                for m in range(HIST) for c in range(N_SLABS)]
        return main + conv

    x_in, h_in, cs_in = _decode_copies_in(d, geo, xs_hbm, cs_hbm, hs_hbm, xs_buf, xp_scr, h0_buf, in_sem)
    x_next, h_next, cs_next = _decode_copies_in(d + 1, geo, xs_hbm, cs_hbm, hs_hbm, xs_buf, xp_scr, h0_buf, in_sem)
    has_next = d + 1 < n_dec
    xs_cur = xs_buf.at[d % 2]
    for cp in [x_in, h_in] + cs_in:
        cp.wait()

    @pl.when(has_next)
    def _():
        x_next.start()

    @pl.when(d > 0)
    def _():
        for cp in copies_out(d - 1)[1:]:
            cp.wait()

    _pre_norm(xs_cur, pre_g, z_scr)

    xr = jnp.dot(z_scr[...], w_in[XR], preferred_element_type=F32)
    _to_step_order(xr, xr_scr, xp_scr, geo)
    _lru_coeffs(xp_scr, a_scr, b_scr, (pre_scr.at[U - 1], pre_scr.at[GS - 1]), conv_w, conv_b, wg, b_r, b_i, lam, R,
                None)

    def in_piece(slab):
        pre_scr[slab - 1] = jnp.dot(z_scr[...], w_in[slab], preferred_element_type=F32)

    in_piece(GR)
    for c in range(N_SLABS):
        ls = slice(c * LANES, (c + 1) * LANES)
        h = h0_buf[:, ls]
        for s in range(L):
            h = a_scr[s * R:(s + 1) * R, ls] * h + b_scr[s * R:(s + 1) * R, ls]
            hp_scr[c, s * geo.step_pitch:s * geo.step_pitch + R, :] = h
        hn_buf[:, ls] = h
    in_piece(V)
    _to_natural_order(hp_scr, a_scr, geo)
    cat[:, 0:WIDTH] = (a_scr[...] * _silu(pre_scr[GR - 1])).astype(BF16)

    in_piece(U)
    in_piece(GS)
    vs_buf[...] = _rms_norm(_gelu(pre_scr[V - 1]), sgu_g[...])
    cat[:, WIDTH:2 * WIDTH] = vs_buf[...].astype(BF16)
    sgu_head = _sgu_head_fn(pre_scr, w_mix, bias_rows, cat, period=L)
    for hd in range(N_HEADS):
        sgu_head(hd)
    for k, cp in enumerate(copies_out(d)[1:]):
        cp.start(priority=k % 2)

    @pl.when(d > 0)
    def _():
        copies_out(d - 1)[0].wait()

    @pl.when(has_next)
    def _():
        for k, cp in enumerate([h_next] + cs_next):
            cp.start(priority=k % 2)

    for c in range(D_MODEL // WIDTH):
        out_scr[:, c * WIDTH:(c + 1) * WIDTH] = jnp.dot(cat[...], w_out[c], preferred_element_type=F32)
    out_scr[...] = xs_cur[...] + _rms_norm(out_scr[...], post_g[...])

    copies_out(d)[0].start()

    @pl.when(d == n_dec - 1)
    def _():
        for cp in copies_out(d):
            cp.wait()


def _full(shape):
    return pl.BlockSpec(shape, lambda *_: (0,) * len(shape))


def _layer(x_prompt, x_sample, conv_state, h_state, pre_g, post_g, w_in, conv_w, conv_b, w_r, b_r, w_i, b_i, lam,
           sgu_g, w_s, b_s, w_out):
    n_seq, seq_len, d_model = x_prompt.shape
    dec_b, dec_t, _ = x_sample.shape
    assert d_model == D_MODEL and conv_w.shape == (CONV_WIDTH, WIDTH) and w_in.shape == (D_MODEL, 5 * WIDTH)
    assert seq_len % TM == 0 and dec_t * DEC_NB == TM and dec_b % DEC_NB == 0
    assert dec_t % SUBLANES == 0 and CHUNK % dec_t == 0 and HIST <= dec_t and dec_t & (dec_t - 1) == 0

    assert w_r.shape == w_i.shape == (N_SLABS, HEAD_DIM, HEAD_DIM)
    row = lambda p: p.reshape(1, -1)
    bs_prompt = jnp.repeat(b_s.T, HEAD_DIM, axis=1)
    params = (row(pre_g), row(post_g), conv_w, row(conv_b), row(b_r), row(b_i), row(lam), row(sgu_g), w_s, bs_prompt)

    tiles = seq_len // TM
    n_tiles = n_seq * tiles
    n_dec = dec_b // DEC_NB
    x_rows = x_prompt.reshape(n_seq * seq_len, d_model)
    xs_rows = x_sample.reshape(dec_b * dec_t, d_model)
    cs_steps = jnp.transpose(conv_state, (1, 0, 2))
    next_of = lambda i: jnp.minimum(i + 1, n_tiles - 1)
    prev_of = lambda i: jnp.clip(i - 1, 0, n_tiles - 1)
    any_spec = pl.BlockSpec(memory_space=pl.ANY)
    dec_geo = _decode_geometry(dec_t)
    relayout_rows = max(g.n_seg * g.seg_pitch for g in (PROMPT, dec_geo))
    assert relayout_rows >= max(g.seg_len * g.step_pitch for g in (PROMPT, dec_geo))
    y_rows, conv_p, h_p, ys_rows, conv_s, h_s, vs_rows = pl.pallas_call(
        functools.partial(_layer_kernel, tiles, n_tiles, n_dec, dec_t),
        grid=(n_tiles + 1 + n_dec,),
        in_specs=[pl.BlockSpec((TM, d_model), lambda i: (next_of(i), 0)),
                  pl.BlockSpec((TM, d_model), lambda i: (prev_of(i), 0)),
                  any_spec, any_spec, any_spec, any_spec, any_spec, any_spec, any_spec]
                 + [_full(p.shape) for p in params],
        out_specs=[pl.BlockSpec((TM, d_model), lambda i: (prev_of(i), 0)),
                   _full((HIST, n_seq, WIDTH)), _full((n_seq, WIDTH)),
                   any_spec, any_spec, any_spec, any_spec],
        out_shape=[jax.ShapeDtypeStruct(x_rows.shape, F32),
                   jax.ShapeDtypeStruct((HIST, n_seq, WIDTH), F32),
                   jax.ShapeDtypeStruct((n_seq, WIDTH), F32),
                   jax.ShapeDtypeStruct(xs_rows.shape, F32),
                   jax.ShapeDtypeStruct(cs_steps.shape, F32),
                   jax.ShapeDtypeStruct(h_state.shape, F32),
                   jax.ShapeDtypeStruct((dec_b * dec_t, WIDTH), F32)],
        scratch_shapes=[pltpu.VMEM((5, D_MODEL, WIDTH), BF16),
                        pltpu.VMEM((D_MODEL // WIDTH, D_MODEL, WIDTH), BF16),
                        pltpu.VMEM((N_HEADS, HEAD_DIM, 2 * HEAD_DIM), BF16),
                        pltpu.VMEM((2, TM, D_MODEL), BF16),
                        pltpu.VMEM((N_SLABS, relayout_rows, LANES), F32),
                        pltpu.VMEM((N_SLABS, TM + HIST * DEC_NB, LANES), F32),
                        pltpu.VMEM((TM, WIDTH), F32), pltpu.VMEM((TM, WIDTH), F32),
                        pltpu.VMEM((N_PRE, TM, WIDTH), F32),
                        pltpu.VMEM((TM, D_MODEL), F32),
                        pltpu.VMEM((2, TM, D_MODEL), BF16),
                        pltpu.VMEM((SUBLANES, WIDTH), F32),
                        pltpu.VMEM((HIST * SUBLANES, WIDTH), F32),
                        pltpu.VMEM((SUBLANES, WIDTH), F32),
                        pltpu.VMEM((2, TM, D_MODEL), F32),
                        pltpu.VMEM((DEC_NB, WIDTH), F32), pltpu.VMEM((DEC_NB, WIDTH), F32),
                        pltpu.SemaphoreType.DMA((N_STAGE,)),
                        pltpu.SemaphoreType.DMA((2 + HIST * N_SLABS,)),
                        pltpu.SemaphoreType.DMA((3 + HIST * N_SLABS,))],
        compiler_params=pltpu.CompilerParams(dimension_semantics=("arbitrary",),
                                             vmem_limit_bytes=VMEM_LIMIT_BYTES),
        name="hybrid_layer",
    )(x_rows, x_rows, xs_rows, cs_steps, h_state, w_in, w_out, w_r, w_i, *params)
    seq_major = lambda c: jnp.transpose(c, (1, 0, 2))
    return (y_rows.reshape(x_prompt.shape), ys_rows.reshape(x_sample.shape), seq_major(conv_p), h_p,
            seq_major(conv_s), h_s, vs_rows.reshape(dec_b, dec_t, WIDTH))


def kernel(x_prompt, x_sample, state_rglru_conv, state_rglru_h, pre_norm_g, post_norm_g, w_in, conv_w, conv_b,
           w_rgate, b_rgate, w_igate, b_igate, lru_lambda, sgu_norm_g, w_spatial, b_spatial, w_out):
    depth = w_in.shape[0]
    yp, ys = x_prompt, x_sample
    conv_p, h_p, conv_s, h_s, v_s = [], [], [], [], []
    for l in range(depth):
        yp, ys, cp, hp, cs, hs, vs = _layer(
            yp, ys, state_rglru_conv[l], state_rglru_h[l], pre_norm_g[l], post_norm_g[l], w_in[l], conv_w[l],
            conv_b[l], w_rgate[l], b_rgate[l], w_igate[l], b_igate[l], lru_lambda[l], sgu_norm_g[l], w_spatial[l],
            b_spatial[l], w_out[l])
        conv_p.append(cp)
        h_p.append(hp)
        conv_s.append(cs)
        h_s.append(hs)
        v_s.append(vs)
    return (yp, ys, jnp.stack(conv_p), jnp.stack(h_p), jnp.stack(conv_s), jnp.stack(h_s), jnp.stack(v_s))
```

```python
import collections
import functools
import math

import jax
import jax.numpy as jnp
from jax import lax
from jax.experimental import pallas as pl
from jax.experimental.pallas import tpu as pltpu

F32 = jnp.float32
BF16 = jnp.bfloat16

EPS = 1e-6
LRU_C = 8.0
CONV_WIDTH = 4
HIST = CONV_WIDTH - 1
N_HEADS = 8
HEAD_DIM = 128
CHUNK = 128
LANES = 128
SUBLANES = 8
N_SLABS = 8
MXU_DIM = 256

D_MODEL = 2048
WIDTH = 1024
TM = 256
XR, GR, U, V, GS = range(5)
N_PRE = 4
N_STAGE = N_PRE + 2 * (D_MODEL // WIDTH)
VMEM_LIMIT_BYTES = 63 * 1024 * 1024

Geometry = collections.namedtuple("Geometry", "n_seg seg_len seg_pitch step_pitch")
PROMPT = Geometry(n_seg=SUBLANES, seg_len=TM // SUBLANES, seg_pitch=TM // SUBLANES + SUBLANES, step_pitch=SUBLANES)
DEC_NB = 32


def _decode_geometry(n_steps):
    return Geometry(n_seg=TM // n_steps, seg_len=n_steps, seg_pitch=n_steps, step_pitch=TM // n_steps + SUBLANES)


def _sigmoid(x):
    return 0.5 * jnp.tanh(0.5 * x) + 0.5


def _silu(x):
    return x * _sigmoid(x)


def _gelu(x):
    c = math.sqrt(2.0 / math.pi)
    return x * (0.5 * (1.0 + jnp.tanh(c * (x + 0.044715 * (x * x * x)))))


def _rms_norm(x, g):
    return x * lax.rsqrt(jnp.mean(x * x, axis=-1, keepdims=True) + EPS) * g


def _pre_norm(x_ref, g_ref, z_scr):
    x = x_ref[...]
    inv = lax.rsqrt(jnp.mean(x * x, axis=-1, keepdims=True) + EPS)
    for kb in range(D_MODEL // MXU_DIM):
        cs = slice(kb * MXU_DIM, (kb + 1) * MXU_DIM)
        z_scr[:, cs] = (x[:, cs] * inv * g_ref[:, cs]).astype(BF16)


def _to_step_order(xr, xr_scr, xp_scr, geo):
    R, L, P = geo.n_seg, geo.seg_len, geo.seg_pitch
    if R == SUBLANES:
        for s in range(R):
            for m in range(L // SUBLANES):
                for c in range(N_SLABS):
                    xp_scr[c, pl.ds(HIST * R + m * SUBLANES * R + s, SUBLANES, stride=R), :] = (
                        xr[s * L + m * SUBLANES:s * L + (m + 1) * SUBLANES, c * LANES:(c + 1) * LANES])
        return
    for s in range(R):
        for c in range(N_SLABS):
            xr_scr[c, s * P:s * P + L, :] = xr[s * L:(s + 1) * L, c * LANES:(c + 1) * LANES]
    for k in range(L):
        for g in range(R // SUBLANES):
            row = HIST * R + k * R + g * SUBLANES
            for c in range(N_SLABS):
                xp_scr[c, row:row + SUBLANES, :] = xr_scr[c, pl.ds(g * SUBLANES * P + k, SUBLANES, stride=P), :]


def _to_natural_order(hp_scr, dst_scr, geo):
    R, L, P = geo.n_seg, geo.seg_len, geo.step_pitch
    for s in range(R):
        for m in range(L // SUBLANES):
            row = s * L + m * SUBLANES
            for c in range(N_SLABS):
                dst_scr[row:row + SUBLANES, c * LANES:(c + 1) * LANES] = (
                    hp_scr[c, pl.ds(m * SUBLANES * P + s, SUBLANES, stride=P), :])


def _lru_coeffs(xp_scr, a_scr, b_scr, gate_scr, conv_w, conv_b, wg, b_r, b_i, lam, rows_per_step, reset_rows):
    R = rows_per_step
    for h in range(N_HEADS):
        ls = slice(h * HEAD_DIM, (h + 1) * HEAD_DIM)
        xc = conv_b[:, ls] + xp_scr[h, 0:TM, :] * conv_w[0:1, ls]
        for k in range(1, CONV_WIDTH):
            xc = xc + xp_scr[h, k * R:k * R + TM, :] * conv_w[k:k + 1, ls]
        b_scr[:, ls] = xc
    for h in range(N_HEADS):
        ls = slice(h * HEAD_DIM, (h + 1) * HEAD_DIM)
        gate_scr[h // 4][:, (h % 4) * 2 * HEAD_DIM:(h % 4 + 1) * 2 * HEAD_DIM] = jnp.dot(
            b_scr[:, ls].astype(BF16), wg[h], preferred_element_type=F32)
    for h in range(N_HEADS):
        ls = slice(h * HEAD_DIM, (h + 1) * HEAD_DIM)
        xc = b_scr[:, ls]
        g = gate_scr[h // 4][:, (h % 4) * 2 * HEAD_DIM:(h % 4 + 1) * 2 * HEAD_DIM]
        r = _sigmoid(g[:, :HEAD_DIM] + b_r[:, ls])
        i = _sigmoid(g[:, HEAD_DIM:] + b_i[:, ls])
        lam_h = lam[:, ls]
        softplus_neg = jnp.maximum(-lam_h, 0.0) + jnp.log1p(jnp.exp(-jnp.abs(lam_h)))
        log_a = r * (-LRU_C * softplus_neg)
        a = jnp.exp(log_a)
        mult = jnp.sqrt(-jnp.tanh(log_a) * (a * a + 1.0))
        ix = i * xc
        b = mult * ix
        a_scr[:, ls] = a
        b_scr[:, ls] = b
        if reset_rows is not None:
            a_scr[0:SUBLANES, ls] = jnp.where(reset_rows, 0.0, a[0:SUBLANES])
            b_scr[0:SUBLANES, ls] = jnp.where(reset_rows, ix[0:SUBLANES], b[0:SUBLANES])


def _sgu_head_fn(pre_scr, w_mix, bias_rows, cat, period=CHUNK):
    row = lax.broadcasted_iota(jnp.int32, (CHUNK, CHUNK), 0)
    col = lax.broadcasted_iota(jnp.int32, (CHUNK, CHUNK), 1)
    keep = row >= col
    if period < CHUNK:
        shift = period.bit_length() - 1
        keep = keep & ((row >> shift) == (col >> shift))
    blocks = [slice(ch * CHUNK, (ch + 1) * CHUNK) for ch in range(TM // CHUNK)]

    def head(hd):
        ls = slice(hd * HEAD_DIM, (hd + 1) * HEAD_DIM)
        vs = slice(WIDTH + hd * HEAD_DIM, WIDTH + (hd + 1) * HEAD_DIM)
        w_blk = w_mix[hd, 0:period, :]
        sh = period
        if sh < CHUNK:
            w_blk = jnp.where(col[0:period, :] < period, w_blk, 0.0)
        while sh < CHUNK:
            w_blk = w_blk + pltpu.roll(w_blk, sh, 1)
            sh *= 2
        w_h = jnp.where(keep, jnp.tile(w_blk, (CHUNK // period, 1)), 0.0).astype(BF16)
        bias = jnp.tile(bias_rows[0:period, ls], (CHUNK // period, 1))
        mixed = jnp.dot(w_h, jnp.concatenate([cat[rs, vs] for rs in blocks], axis=1), preferred_element_type=F32)
        for ch, rs in enumerate(blocks):
            s = mixed[:, ch * HEAD_DIM:(ch + 1) * HEAD_DIM] + bias
            cat[rs, vs] = (_gelu(pre_scr[U - 1, rs, ls]) * s * _silu(pre_scr[GS - 1, rs, ls])).astype(BF16)

    return head


def _load_weight(w_hbm, w_scr, stages, sem):
    n_stage = len(stages)
    n_row = w_hbm.shape[0] // TM
    n_chunks = w_scr.shape[0] * n_row
    assert n_chunks % n_stage == 0

    def aligned(v, m):
        return v if isinstance(v, int) else pl.multiple_of(v, m)

    def copy(k, slot):
        rows = pl.ds(aligned((k % n_row) * TM, TM), TM)
        cols = pl.ds(aligned((k // n_row) * WIDTH, WIDTH), WIDTH)
        return pltpu.make_async_copy(w_hbm.at[rows, cols], stages[slot], sem.at[slot])

    for k in range(n_stage):
        copy(k, k).start()

    def body(r, carry):
        for slot in range(n_stage):
            k = r * n_stage + slot
            copy(k, slot).wait()
            w_scr[k // n_row, pl.ds(aligned((k % n_row) * TM, TM), TM), :] = stages[slot][...].astype(BF16)

            @pl.when(k + n_stage < n_chunks)
            def _():
                copy(k + n_stage, slot).start()
        return carry

    lax.fori_loop(0, n_chunks // n_stage, body, 0)


def _layer_kernel(tiles_per_seq, n_tiles, n_dec, dec_t,
                  x_ref, xprev_ref, xs_hbm, cs_hbm, hs_hbm, w_in_hbm, w_out_hbm, wr_hbm, wi_hbm,
                  pre_g, post_g, conv_w, conv_b, b_r, b_i, lam, sgu_g, w_s, bs_rows,
                  y_ref, conv_out, h_out, ys_hbm, convs_hbm, hs_out_hbm, vs_hbm,
                  w_in, w_out, wg, z_scr, xr_scr, xp_scr, a_scr, b_scr, pre_scr, out_scr, cat_scr,
                  carry_h, carry_x, xr_tail, xs_buf, h0_buf, hn_buf, w_sem, in_sem, out_sem):
    hp_scr = xr_scr
    vs_buf = b_scr
    xr_nat = xs_buf.at[0, :, pl.ds(0, WIDTH)]
    i = pl.program_id(0)
    t = i % tiles_per_seq
    cat_prev = cat_scr.at[(i + 1) % 2]
    cat_cur = cat_scr.at[i % 2]
    z_cur = z_scr.at[i % 2]
    z_next = z_scr.at[(i + 1) % 2]

    sq_sums = []

    def out_piece(c):
        piece = jnp.dot(cat_prev[...], w_out[c], preferred_element_type=F32)
        out_scr[:, c * WIDTH:(c + 1) * WIDTH] = piece
        sq_sums.append(jnp.sum(piece * piece, axis=-1, keepdims=True))

    def in_piece(slab):
        pre_scr[slab - 1] = jnp.dot(z_cur[...], w_in[slab], preferred_element_type=F32)

    def finish_prev():
        inv = lax.rsqrt(functools.reduce(lambda p, q: p + q, sq_sums) * (1.0 / D_MODEL) + EPS)
        sq_sums.clear()
        y_ref[...] = xprev_ref[...] + out_scr[...] * inv * post_g[...]

    @pl.when(i == 0)
    def _():
        halves = lambda ref: [ref.at[:, pl.ds(c * WIDTH, WIDTH)] for c in range(D_MODEL // WIDTH)]
        stages = [pre_scr.at[s] for s in range(pre_scr.shape[0])] + halves(out_scr) + halves(xs_buf.at[0])
        gates_in = [pltpu.make_async_copy(src, xr_scr.at[:, pl.ds(k * HEAD_DIM, HEAD_DIM), :], in_sem.at[k])
                    for k, src in enumerate((wr_hbm, wi_hbm))]
        for cp in gates_in:
            cp.start()
        _load_weight(w_in_hbm, w_in, stages, w_sem)
        _load_weight(w_out_hbm, w_out, stages, w_sem)
        for k, cp in enumerate(gates_in):
            cp.wait()
            wg[:, :, k * HEAD_DIM:(k + 1) * HEAD_DIM] = xr_scr[:, k * HEAD_DIM:(k + 1) * HEAD_DIM, :].astype(BF16)
        cat_scr[1] = jnp.zeros(cat_scr.shape[1:], BF16)
        _pre_norm(xprev_ref, pre_g, z_cur)
        xr_nat[...] = jnp.dot(z_cur[...], w_in[XR], preferred_element_type=F32)

    @pl.when(t == 0)
    def _():
        carry_h[...] = jnp.zeros_like(carry_h)
        carry_x[...] = jnp.zeros_like(carry_x)

    @pl.when(i == n_tiles)
    def _():
        x_in, h_in, cs_in = _decode_copies_in(0, _decode_geometry(dec_t), xs_hbm, cs_hbm, hs_hbm, xs_buf, xp_scr, h0_buf,
                                              in_sem)
        for k, cp in enumerate([x_in, h_in] + cs_in):
            cp.start(priority=k % 2)
        for c in range(D_MODEL // WIDTH):
            out_piece(c)
        finish_prev()

    @pl.when(i > n_tiles)
    def _():
        _decode_tile(i - n_tiles - 1, n_dec, _decode_geometry(dec_t), xs_hbm, cs_hbm, hs_hbm, pre_g, post_g, conv_w, conv_b,
                     wg, b_r, b_i, lam, sgu_g, w_s, bs_rows, ys_hbm, convs_hbm, hs_out_hbm, vs_hbm, w_in, w_out, z_scr.at[0],
                     xr_scr, xp_scr, a_scr, b_scr, hp_scr, pre_scr, out_scr, cat_scr.at[0], xs_buf, vs_buf,
                     h0_buf, hn_buf, in_sem, out_sem)

    @pl.when(i < n_tiles)
    def _():
        geo = PROMPT
        R = geo.n_seg

        xr = xr_nat[...]
        xr_tail[...] = xr[TM - SUBLANES:TM, :]
        _to_step_order(xr, xr_scr, xp_scr, geo)
        sub = lax.broadcasted_iota(jnp.int32, (R, WIDTH), 0)
        for m in range(HIST):
            src = HIST * R + (geo.seg_len - HIST + m) * R
            rs = slice(m * R, (m + 1) * R)
            for c in range(N_SLABS):
                ls = slice(c * LANES, (c + 1) * LANES)
                rolled = pltpu.roll(xp_scr[c, src:src + R, :], 1, 0)
                xp_scr[c, rs, :] = jnp.where(sub[:, 0:LANES] == 0, carry_x[rs, ls], rolled)
                carry_x[rs, ls] = rolled

        reset_rows = (lax.broadcasted_iota(jnp.int32, (SUBLANES, HEAD_DIM), 0) == 0) & (t == 0)
        _lru_coeffs(xp_scr, a_scr, b_scr, (pre_scr.at[U - 1], pre_scr.at[GS - 1]), conv_w, conv_b, wg, b_r, b_i, lam,
                    R, reset_rows)

        in_piece(GR)
        hl = jnp.zeros((R, WIDTH), F32)
        pr = jnp.ones((R, WIDTH), F32)
        for k in range(geo.seg_len):
            a_k = a_scr[k * R:(k + 1) * R, :]
            hl = a_k * hl + b_scr[k * R:(k + 1) * R, :]
            pr = a_k * pr
        c_in = carry_h[0:1, :]
        h0 = jnp.zeros((R, WIDTH), F32)
        for j in range(R):
            h0 = jnp.where(sub == j, c_in, h0)
            c_in = hl[j:j + 1, :] + pr[j:j + 1, :] * c_in
        carry_h[0:1, :] = c_in
        h = h0
        for k in range(geo.seg_len):
            h = a_scr[k * R:(k + 1) * R, :] * h + b_scr[k * R:(k + 1) * R, :]
            for c in range(N_SLABS):
                hp_scr[c, pl.ds(k, R, stride=geo.seg_pitch), :] = h[:, c * LANES:(c + 1) * LANES]
        in_piece(V)
        for j in range(R):
            rs = slice(j * geo.seg_len, (j + 1) * geo.seg_len)
            for c in range(N_SLABS):
                ls = slice(c * LANES, (c + 1) * LANES)
                h_nat = hp_scr[c, j * geo.seg_pitch:j * geo.seg_pitch + geo.seg_len, :]
                cat_cur[rs, ls] = (h_nat * _silu(pre_scr[GR - 1, rs, ls])).astype(BF16)

        in_piece(U)
        in_piece(GS)
        cat_cur[:, WIDTH:2 * WIDTH] = _rms_norm(_gelu(pre_scr[V - 1]), sgu_g[...]).astype(BF16)
        sgu_head = _sgu_head_fn(pre_scr, w_s, bs_rows, cat_cur)
        for hd in range(N_HEADS):
            if hd % (N_HEADS // (D_MODEL // WIDTH)) == 0:
                out_piece(hd // (N_HEADS // (D_MODEL // WIDTH)))
            sgu_head(hd)
        _pre_norm(x_ref, pre_g, z_next)
        xr_nat[...] = jnp.dot(z_next[...], w_in[XR], preferred_element_type=F32)
        finish_prev()

    @pl.when((i < n_tiles) & (t == tiles_per_seq - 1))
    def _():
        seq = i // tiles_per_seq
        for m in range(HIST):
            conv_out[m, pl.ds(seq, 1), :] = xr_tail[SUBLANES - HIST + m:SUBLANES - HIST + m + 1, :]
        h_out[pl.ds(seq, 1), :] = carry_h[0:1, :]


def _decode_copies_in(d, geo, xs_hbm, cs_hbm, hs_hbm, xs_buf, xp_scr, h0_buf, in_sem):
    R = geo.n_seg
    seqs = pl.ds(pl.multiple_of(d * R, R), R)
    x_in = pltpu.make_async_copy(xs_hbm.at[pl.ds(pl.multiple_of(d * TM, TM), TM), :], xs_buf.at[d % 2], in_sem.at[0])
    h_in = pltpu.make_async_copy(hs_hbm.at[seqs, :], h0_buf, in_sem.at[1])
    cs_in = [pltpu.make_async_copy(cs_hbm.at[m, seqs, pl.ds(c * LANES, LANES)], xp_scr.at[c, pl.ds(m * R, R), :],
                                   in_sem.at[2 + m * N_SLABS + c])
             for m in range(HIST) for c in range(N_SLABS)]
    return x_in, h_in, cs_in


def _decode_tile(d, n_dec, geo, xs_hbm, cs_hbm, hs_hbm, pre_g, post_g, conv_w, conv_b, wg, b_r, b_i, lam, sgu_g,
                 w_mix, bias_rows, ys_hbm, convs_hbm, hs_out_hbm, vs_hbm, w_in, w_out, z_scr, xr_scr, xp_scr, a_scr, b_scr,
                 hp_scr, pre_scr, out_scr, cat, xs_buf, vs_buf, h0_buf, hn_buf, in_sem, out_sem):
    R, L = geo.n_seg, geo.seg_len
    tile_rows = lambda dd: pl.ds(pl.multiple_of(dd * TM, TM), TM)
    tile_seqs = lambda dd: pl.ds(pl.multiple_of(dd * R, R), R)

    half_rows = [pl.ds(k * (TM // 2), TM // 2) for k in range(2)]

    def y_out(dd):
        return [pltpu.make_async_copy(out_scr.at[rs, :],
                                      ys_hbm.at[pl.ds(pl.multiple_of(dd * TM + k * (TM // 2), TM // 2), TM // 2), :],
                                      out_sem.at[(0, 3 + HIST * N_SLABS)[k]])
                for k, rs in enumerate(half_rows)]

    def copies_out(dd):
        main = [pltpu.make_async_copy(vs_buf, vs_hbm.at[tile_rows(dd), :], out_sem.at[1]),
                pltpu.make_async_copy(hn_buf, hs_out_hbm.at[tile_seqs(dd), :], out_sem.at[2])]
        conv = [pltpu.make_async_copy(xp_scr.at[c, pl.ds((L + m) * R, R), :],
                                      convs_hbm.at[m, tile_seqs(dd), pl.ds(c * LANES, LANES)],
                                      out_sem.at[3 + m * N_SLABS + c])
                for m in range(HIST) for c in range(N_SLABS)]
        return main + conv

    x_in, h_in, cs_in = _decode_copies_in(d, geo, xs_hbm, cs_hbm, hs_hbm, xs_buf, xp_scr, h0_buf, in_sem)
    x_next, h_next, cs_next = _decode_copies_in(d + 1, geo, xs_hbm, cs_hbm, hs_hbm, xs_buf, xp_scr, h0_buf, in_sem)
    has_next = d + 1 < n_dec
    xs_cur = xs_buf.at[d % 2]
    for cp in [x_in, h_in] + cs_in:
        cp.wait()

    @pl.when(has_next)
    def _():
        x_next.start()

    @pl.when(d > 0)
    def _():
        for cp in copies_out(d - 1):
            cp.wait()

    _pre_norm(xs_cur, pre_g, z_scr)

    xr = jnp.dot(z_scr[...], w_in[XR], preferred_element_type=F32)
    _to_step_order(xr, xr_scr, xp_scr, geo)
    _lru_coeffs(xp_scr, a_scr, b_scr, (pre_scr.at[U - 1], pre_scr.at[GS - 1]), conv_w, conv_b, wg, b_r, b_i, lam, R,
                None)

    def in_piece(slab):
        pre_scr[slab - 1] = jnp.dot(z_scr[...], w_in[slab], preferred_element_type=F32)

    in_piece(GR)
    for c in range(N_SLABS):
        ls = slice(c * LANES, (c + 1) * LANES)
        h = h0_buf[:, ls]
        for s in range(L):
            h = a_scr[s * R:(s + 1) * R, ls] * h + b_scr[s * R:(s + 1) * R, ls]
            hp_scr[c, s * geo.step_pitch:s * geo.step_pitch + R, :] = h
        hn_buf[:, ls] = h
    in_piece(V)
    _to_natural_order(hp_scr, a_scr, geo)
    cat[:, 0:WIDTH] = (a_scr[...] * _silu(pre_scr[GR - 1])).astype(BF16)

    in_piece(U)
    in_piece(GS)
    vs_buf[...] = _rms_norm(_gelu(pre_scr[V - 1]), sgu_g[...])
    cat[:, WIDTH:2 * WIDTH] = vs_buf[...].astype(BF16)
    sgu_head = _sgu_head_fn(pre_scr, w_mix, bias_rows, cat, period=L)
    for hd in range(N_HEADS):
        sgu_head(hd)
    for k, cp in enumerate(copies_out(d)):
        cp.start(priority=k % 2)

    @pl.when(d > 0)
    def _():
        for cp in y_out(d - 1):
            cp.wait()

    @pl.when(has_next)
    def _():
        for k, cp in enumerate([h_next] + cs_next):
            cp.start(priority=k % 2)

    for c in range(D_MODEL // WIDTH):
        out_scr[:, c * WIDTH:(c + 1) * WIDTH] = jnp.dot(cat[...], w_out[c], preferred_element_type=F32)
    for rs, cp in zip(half_rows, y_out(d)):
        out_scr[rs, :] = xs_cur[rs, :] + _rms_norm(out_scr[rs, :], post_g[...])
        cp.start()

    @pl.when(d == n_dec - 1)
    def _():
        for cp in copies_out(d) + y_out(d):
            cp.wait()


def _full(shape):
    return pl.BlockSpec(shape, lambda *_: (0,) * len(shape))


def _layer(x_prompt, x_sample, conv_state, h_state, pre_g, post_g, w_in, conv_w, conv_b, w_r, b_r, w_i, b_i, lam,
           sgu_g, w_s, b_s, w_out):
    n_seq, seq_len, d_model = x_prompt.shape
    dec_b, dec_t, _ = x_sample.shape
    assert d_model == D_MODEL and conv_w.shape == (CONV_WIDTH, WIDTH) and w_in.shape == (D_MODEL, 5 * WIDTH)
    assert seq_len % TM == 0 and dec_t * DEC_NB == TM and dec_b % DEC_NB == 0
    assert dec_t % SUBLANES == 0 and CHUNK % dec_t == 0 and HIST <= dec_t and dec_t & (dec_t - 1) == 0

    assert w_r.shape == w_i.shape == (N_SLABS, HEAD_DIM, HEAD_DIM)
    row = lambda p: p.reshape(1, -1)
    bs_prompt = jnp.repeat(b_s.T, HEAD_DIM, axis=1)
    params = (row(pre_g), row(post_g), conv_w, row(conv_b), row(b_r), row(b_i), row(lam), row(sgu_g), w_s, bs_prompt)

    tiles = seq_len // TM
    n_tiles = n_seq * tiles
    n_dec = dec_b // DEC_NB
    x_rows = x_prompt.reshape(n_seq * seq_len, d_model)
    xs_rows = x_sample.reshape(dec_b * dec_t, d_model)
    cs_steps = jnp.transpose(conv_state, (1, 0, 2))
    next_of = lambda i: jnp.minimum(i + 1, n_tiles - 1)
    prev_of = lambda i: jnp.clip(i - 1, 0, n_tiles - 1)
    any_spec = pl.BlockSpec(memory_space=pl.ANY)
    dec_geo = _decode_geometry(dec_t)
    relayout_rows = max(g.n_seg * g.seg_pitch for g in (PROMPT, dec_geo))
    assert relayout_rows >= max(g.seg_len * g.step_pitch for g in (PROMPT, dec_geo))
    y_rows, conv_p, h_p, ys_rows, conv_s, h_s, vs_rows = pl.pallas_call(
        functools.partial(_layer_kernel, tiles, n_tiles, n_dec, dec_t),
        grid=(n_tiles + 1 + n_dec,),
        in_specs=[pl.BlockSpec((TM, d_model), lambda i: (next_of(i), 0)),
                  pl.BlockSpec((TM, d_model), lambda i: (prev_of(i), 0)),
                  any_spec, any_spec, any_spec, any_spec, any_spec, any_spec, any_spec]
                 + [_full(p.shape) for p in params],
        out_specs=[pl.BlockSpec((TM, d_model), lambda i: (prev_of(i), 0)),
                   _full((HIST, n_seq, WIDTH)), _full((n_seq, WIDTH)),
                   any_spec, any_spec, any_spec, any_spec],
        out_shape=[jax.ShapeDtypeStruct(x_rows.shape, F32),
                   jax.ShapeDtypeStruct((HIST, n_seq, WIDTH), F32),
                   jax.ShapeDtypeStruct((n_seq, WIDTH), F32),
                   jax.ShapeDtypeStruct(xs_rows.shape, F32),
                   jax.ShapeDtypeStruct(cs_steps.shape, F32),
                   jax.ShapeDtypeStruct(h_state.shape, F32),
                   jax.ShapeDtypeStruct((dec_b * dec_t, WIDTH), F32)],
        scratch_shapes=[pltpu.VMEM((5, D_MODEL, WIDTH), BF16),
                        pltpu.VMEM((D_MODEL // WIDTH, D_MODEL, WIDTH), BF16),
                        pltpu.VMEM((N_HEADS, HEAD_DIM, 2 * HEAD_DIM), BF16),
                        pltpu.VMEM((2, TM, D_MODEL), BF16),
                        pltpu.VMEM((N_SLABS, relayout_rows, LANES), F32),
                        pltpu.VMEM((N_SLABS, TM + HIST * DEC_NB, LANES), F32),
                        pltpu.VMEM((TM, WIDTH), F32), pltpu.VMEM((TM, WIDTH), F32),
                        pltpu.VMEM((N_PRE, TM, WIDTH), F32),
                        pltpu.VMEM((TM, D_MODEL), F32),
                        pltpu.VMEM((2, TM, D_MODEL), BF16),
                        pltpu.VMEM((SUBLANES, WIDTH), F32),
                        pltpu.VMEM((HIST * SUBLANES, WIDTH), F32),
                        pltpu.VMEM((SUBLANES, WIDTH), F32),
                        pltpu.VMEM((2, TM, D_MODEL), F32),
                        pltpu.VMEM((DEC_NB, WIDTH), F32), pltpu.VMEM((DEC_NB, WIDTH), F32),
                        pltpu.SemaphoreType.DMA((N_STAGE,)),
                        pltpu.SemaphoreType.DMA((2 + HIST * N_SLABS,)),
                        pltpu.SemaphoreType.DMA((4 + HIST * N_SLABS,))],
        compiler_params=pltpu.CompilerParams(dimension_semantics=("arbitrary",),
                                             vmem_limit_bytes=VMEM_LIMIT_BYTES),
        name="hybrid_layer",
    )(x_rows, x_rows, xs_rows, cs_steps, h_state, w_in, w_out, w_r, w_i, *params)
    seq_major = lambda c: jnp.transpose(c, (1, 0, 2))
    return (y_rows.reshape(x_prompt.shape), ys_rows.reshape(x_sample.shape), seq_major(conv_p), h_p,
            seq_major(conv_s), h_s, vs_rows.reshape(dec_b, dec_t, WIDTH))


def kernel(x_prompt, x_sample, state_rglru_conv, state_rglru_h, pre_norm_g, post_norm_g, w_in, conv_w, conv_b,
           w_rgate, b_rgate, w_igate, b_igate, lru_lambda, sgu_norm_g, w_spatial, b_spatial, w_out):
    depth = w_in.shape[0]
    yp, ys = x_prompt, x_sample
    conv_p, h_p, conv_s, h_s, v_s = [], [], [], [], []
    for l in range(depth):
        yp, ys, cp, hp, cs, hs, vs = _layer(
            yp, ys, state_rglru_conv[l], state_rglru_h[l], pre_norm_g[l], post_norm_g[l], w_in[l], conv_w[l],
            conv_b[l], w_rgate[l], b_rgate[l], w_igate[l], b_igate[l], lru_lambda[l], sgu_norm_g[l], w_spatial[l],
            b_spatial[l], w_out[l])
        conv_p.append(cp)
        h_p.append(hp)
        conv_s.append(cs)
        h_s.append(hs)
        v_s.append(vs)
    return (yp, ys, jnp.stack(conv_p), jnp.stack(h_p), jnp.stack(conv_s), jnp.stack(h_s), jnp.stack(v_s))
```
